```python
import math, functools
import jax, jax.numpy as jnp
from jax import lax
import numpy as np

D_MODEL = 1024
BATCH = 2
SEQ = 8192
DEPTH = 1
DEC_BATCH = 128
DEC_SEQ = 1
PAST_LEN = 8192
PAGE_SIZE = 128

MLA_HEADS = 8
QK_NOPE = 64
QK_ROPE = 32
V_HEAD = 64
Q_LORA = 384
KV_LORA = 256
ROPE_THETA = 10000.0
RWKV_HEADS = 8
RWKV_HEAD = 64
RWKV_W = RWKV_HEADS * RWKV_HEAD
DECAY_LORA = 64
AAA_LORA = 64
RWKV_COLS = 3 * RWKV_W + DECAY_LORA + AAA_LORA
RWKV_SPLITS = (RWKV_W, 2 * RWKV_W, 3 * RWKV_W, 3 * RWKV_W + DECAY_LORA)
MEM_TOKENS = 256
MEM_HEADS = 4
MEM_HEAD = 128
MEM_W = MEM_HEADS * MEM_HEAD
N_BRANCH = 3
IN_SPLITS = (Q_LORA, Q_LORA + KV_LORA, Q_LORA + KV_LORA + QK_ROPE,
             Q_LORA + KV_LORA + QK_ROPE + RWKV_COLS,
             Q_LORA + KV_LORA + QK_ROPE + RWKV_COLS + MEM_W)
IN_COLS = Q_LORA + KV_LORA + QK_ROPE + RWKV_COLS + MEM_W + N_BRANCH * D_MODEL
N_EXPERTS = 32
TOP_K = 4
D_FF = 1024
SWIGLU_ALPHA = 1.702
SWIGLU_LIMIT = 7.0
Q_BLOCK = 128
MOE_BLOCK = 128
RMS_EPS = 1e-6
GN_EPS = 64e-5
L2_EPS = 1e-12

kernel_name = 'hybrid_mla_rwkv7_memxattn_moe_step'


def rmsnorm(x, g):
    x32 = x.astype(jnp.float32)
    y = x32 * lax.rsqrt(jnp.mean(x32 * x32, axis=-1, keepdims=True) + RMS_EPS)
    return (y * g.astype(jnp.float32)).astype(x.dtype)


def rope(x, pos):
    half = x.shape[-1] // 2
    inv = jnp.exp(-math.log(ROPE_THETA) * jnp.arange(half, dtype=jnp.float32) / half)
    ang = pos.astype(jnp.float32)[:, None] * inv[None, :]
    cos = jnp.cos(ang)[None, :, None, :]
    sin = jnp.sin(ang)[None, :, None, :]
    x32 = x.astype(jnp.float32)
    x1, x2 = x32[..., :half], x32[..., half:]
    return jnp.concatenate([x1 * cos - x2 * sin, x1 * sin + x2 * cos], axis=-1).astype(x.dtype)


def mla_prompt_attn(q_lat, q_pe, c_kv, k_pe):
    B, T, H, _ = q_lat.shape
    nb = T // Q_BLOCK
    scale = (QK_NOPE + QK_ROPE) ** -0.5
    key_pos = jnp.arange(T)

    def blocks(t):
        return jnp.moveaxis(t.reshape((B, nb, Q_BLOCK) + t.shape[2:]), 1, 0)

    def one_block(args):
        ql, qp, bi = args
        s = (jnp.einsum('bqhc,bkc->bhqk', ql, c_kv)
             + jnp.einsum('bqhr,bkr->bhqk', qp, k_pe)).astype(jnp.float32) * scale
        q_pos = bi * Q_BLOCK + jnp.arange(Q_BLOCK)
        s = jnp.where(key_pos[None, :] <= q_pos[:, None], s, -jnp.inf)
        p = jax.nn.softmax(s, axis=-1).astype(c_kv.dtype)
        return jnp.einsum('bhqk,bkc->bqhc', p, c_kv)

    o = lax.map(one_block, (blocks(q_lat), blocks(q_pe), jnp.arange(nb)))
    return jnp.moveaxis(o, 0, 1).reshape(B, T, H, KV_LORA)


def mla_decode_attn(q_lat, q_pe, c_kv, k_pe, past_lat, past_pe):
    scale = (QK_NOPE + QK_ROPE) ** -0.5
    n_new = q_lat.shape[1]
    n_past = past_lat.shape[1]
    s_past = (jnp.einsum('bqhc,bkc->bhqk', q_lat, past_lat)
              + jnp.einsum('bqhr,bkr->bhqk', q_pe, past_pe)).astype(jnp.float32) * scale
    s_new = (jnp.einsum('bqhc,bkc->bhqk', q_lat, c_kv)
             + jnp.einsum('bqhr,bkr->bhqk', q_pe, k_pe)).astype(jnp.float32) * scale
    causal = jnp.arange(n_new)[None, :] <= jnp.arange(n_new)[:, None]
    s_new = jnp.where(causal, s_new, -jnp.inf)
    p = jax.nn.softmax(jnp.concatenate([s_past, s_new], axis=-1), axis=-1).astype(c_kv.dtype)
    return (jnp.einsum('bhqk,bkc->bqhc', p[..., :n_past], past_lat)
            + jnp.einsum('bhqk,bkc->bqhc', p[..., n_past:], c_kv))


def rwkv7_step(S, inp):
    r_t, w_t, k_t, v_t, a_t, b_t = inp
    sa = jnp.einsum('bhij,bhj->bhi', S, a_t)
    S = S * w_t[:, :, None, :] + sa[..., None] * b_t[:, :, None, :] + v_t[..., None] * k_t[:, :, None, :]
    y = jnp.einsum('bhij,bhj->bhi', S, r_t)
    return S, y


def rwkv7_time_mix(rw, S0, W):
    B, T, _ = rw.shape
    f32 = jnp.float32
    r, k, v, wd, ad = jnp.split(rw.astype(f32), RWKV_SPLITS, axis=-1)
    w_log = -jax.nn.softplus(-(W['w0_decay'].astype(f32) + jnp.tanh(wd) @ W['w_decay_up'].astype(f32))) - 0.5
    decay = jnp.exp(-jnp.exp(w_log))
    a = jax.nn.sigmoid(W['a0'].astype(f32) + ad @ W['w_aaa_up'].astype(f32))

    def heads(t):
        return t.reshape(B, T, RWKV_HEADS, RWKV_HEAD)

    kk = heads(k * W['k_k'].astype(f32))
    kk = kk / jnp.maximum(jnp.linalg.norm(kk, axis=-1, keepdims=True), L2_EPS)
    k = k * (1.0 + (a - 1.0) * W['k_a'].astype(f32))
    r_h, k_h, v_h, w_h, a_h = heads(r), heads(k), heads(v), heads(decay), heads(a)
    xs = tuple(jnp.swapaxes(t, 0, 1) for t in (r_h, w_h, k_h, v_h, -kk, kk * a_h))
    S_fin, y = lax.scan(rwkv7_step, S0.astype(f32), xs)
    y = jnp.swapaxes(y, 0, 1)
    mu = jnp.mean(y, axis=-1, keepdims=True)
    var = jnp.mean(jnp.square(y - mu), axis=-1, keepdims=True)
    y = ((y - mu) * lax.rsqrt(var + GN_EPS)).reshape(B, T, RWKV_W) * W['ln_x_w'].astype(f32) + W['ln_x_b'].astype(f32)
    bonus = jnp.sum(r_h * k_h * W['r_k'].astype(f32), axis=-1, keepdims=True) * v_h
    out = y + bonus.reshape(B, T, RWKV_W)
    return out.astype(rw.dtype), S_fin.astype(S0.dtype)


def clamped_swiglu(h):
    x_glu = jnp.minimum(h[..., ::2], SWIGLU_LIMIT)
    x_lin = jnp.clip(h[..., 1::2], -SWIGLU_LIMIT, SWIGLU_LIMIT)
    return x_glu * jax.nn.sigmoid(SWIGLU_ALPHA * x_glu) * (x_lin + 1.0)


def moe_ffn(x, W):
    T, D = x.shape
    logits = (x @ W['w_router']).astype(jnp.float32) + W['b_router'].astype(jnp.float32)
    top_v, top_i = lax.top_k(logits, TOP_K)
    gate = jax.nn.softmax(top_v, axis=-1)
    n_assign = T * TOP_K
    flat_e = top_i.reshape(-1)
    flat_tok = jnp.arange(n_assign, dtype=jnp.int32) // TOP_K
    flat_gate = gate.reshape(-1)
    order = jnp.argsort(flat_e)
    sorted_e = flat_e[order]
    counts = jnp.zeros((N_EXPERTS,), jnp.int32).at[flat_e].add(1)
    padded = ((counts + MOE_BLOCK - 1) // MOE_BLOCK) * MOE_BLOCK
    start = jnp.cumsum(counts) - counts
    pend = jnp.cumsum(padded)
    pstart = pend - padded
    dest = pstart[sorted_e] + (jnp.arange(n_assign, dtype=jnp.int32) - start[sorted_e])
    n_rows = (-(-n_assign // MOE_BLOCK) + N_EXPERTS) * MOE_BLOCK
    n_blocks = n_rows // MOE_BLOCK
    row_tok = jnp.full((n_rows,), T, jnp.int32).at[dest].set(flat_tok[order])
    row_gate = jnp.zeros((n_rows,), jnp.float32).at[dest].set(flat_gate[order])
    block_e = jnp.minimum(jnp.searchsorted(pend, jnp.arange(n_blocks, dtype=jnp.int32) * MOE_BLOCK, side='right'), N_EXPERTS - 1)
    x_pad = jnp.concatenate([x, jnp.zeros((1, D), x.dtype)], axis=0)
    xb = x_pad[row_tok].reshape(n_blocks, MOE_BLOCK, D)

    def expert_block(args):
        xe, e = args
        hu = xe @ W['w_mlp1'][e] + W['b_mlp1'][e]
        return clamped_swiglu(hu) @ W['w_mlp2'][e] + W['b_mlp2'][e]

    yb = lax.map(expert_block, (xb, block_e)).reshape(n_rows, D)
    y = jax.ops.segment_sum(yb * row_gate[:, None].astype(yb.dtype), row_tok, num_segments=T + 1)
    return y[:T]


def trunk_layer(x, pos, mla_attend, S0, shift0, mem_k, mem_v, W):
    B, T, _ = x.shape
    xn = rmsnorm(x, W['g_attn_norm'])
    proj = xn @ W['w_in']
    c_q, c_kv, k_pe, rw, q_mem, gate_pre = jnp.split(proj, IN_SPLITS, axis=-1)
    c_q = rmsnorm(c_q, W['g_q_norm'])
    q = (c_q @ W['w_uq']).reshape(B, T, MLA_HEADS, QK_NOPE + QK_ROPE)
    q_pe = rope(q[..., QK_NOPE:], pos)
    q_lat = jnp.einsum('bthn,chn->bthc', q[..., :QK_NOPE], W['w_uk'])
    c_kv = rmsnorm(c_kv, W['g_kv_norm'])
    k_pe = rope(k_pe[:, :, None, :], pos)[:, :, 0, :]
    o_lat = mla_attend(q_lat, q_pe, c_kv, k_pe)
    o_a = jnp.einsum('bthc,chv->bthv', o_lat, W['w_uv']).reshape(B, T, MLA_HEADS * V_HEAD) @ W['w_o_mla']
    prev = jnp.concatenate([shift0[:, None, :].astype(rw.dtype), rw[:, :-1]], axis=1)
    o_b, S_new = rwkv7_time_mix(rw + (prev - rw) * W['mu_shift'], S0, W)
    o_b = o_b @ W['w_o_rwkv']
    q_m = q_mem.reshape(B, T, MEM_HEADS, MEM_HEAD)
    s = jnp.einsum('bthd,bmhd->bhtm', q_m, mem_k).astype(jnp.float32) * (MEM_HEAD ** -0.5)
    p = jax.nn.softmax(s, axis=-1).astype(mem_v.dtype)
    o_c = jnp.einsum('bhtm,bmhd->bthd', p, mem_v).reshape(B, T, MEM_W) @ W['w_o_mem']
    gates = jax.nn.sigmoid((gate_pre + W['b_gate']).astype(jnp.float32)).astype(x.dtype).reshape(B, T, N_BRANCH, D_MODEL)
    merged = gates[:, :, 0] * o_a + gates[:, :, 1] * o_b + gates[:, :, 2] * o_c
    h = x + merged @ W['w_out']
    hn = rmsnorm(h, W['g_ffn_norm'])
    h = h + moe_ffn(hn.reshape(B * T, D_MODEL), W).reshape(B, T, D_MODEL)
    return h, c_kv, k_pe, S_new, rw[:, -1]


def setup_inputs(seed: int = 0) -> dict:
    key = jax.random.key(seed)
    keys = iter(jax.random.split(key, 64))
    L = DEPTH
    D = D_MODEL

    def nrm(shape, scale=1.0):
        return jax.random.normal(next(keys), shape, jnp.float32) * scale

    def gain(shape):
        return 1.0 + nrm(shape, 0.02)

    n_pages = PAST_LEN // PAGE_SIZE
    n_used = DEC_BATCH * n_pages
    n_pool = n_used + (n_used + 3) // 4
    inp = {}
    inp['x_prompt'] = nrm((BATCH, SEQ, D))
    inp['x_sample'] = nrm((DEC_BATCH, DEC_SEQ, D))
    inp['mem_prompt'] = nrm((BATCH, MEM_TOKENS, D))
    inp['cache_kv_latent'] = nrm((L, n_pool, PAGE_SIZE, KV_LORA))
    inp['cache_k_rope'] = nrm((L, n_pool, PAGE_SIZE, QK_ROPE))
    inp['page_table'] = jax.random.permutation(next(keys), n_pool)[:n_used].reshape(DEC_BATCH, n_pages).astype(jnp.int32)
    inp['state_rwkv'] = nrm((L, DEC_BATCH, RWKV_HEADS, RWKV_HEAD, RWKV_HEAD), 0.5)
    inp['state_rwkv_shift'] = nrm((L, DEC_BATCH, RWKV_COLS))
    inp['cache_mem_k'] = nrm((L, DEC_BATCH, MEM_TOKENS, MEM_HEADS, MEM_HEAD))
    inp['cache_mem_v'] = nrm((L, DEC_BATCH, MEM_TOKENS, MEM_HEADS, MEM_HEAD))
    inp['g_attn_norm'] = gain((L, D))
    inp['w_in'] = nrm((L, D, IN_COLS), D ** -0.5)
    inp['b_gate'] = nrm((L, N_BRANCH * D), 0.01)
    inp['g_q_norm'] = gain((L, Q_LORA))
    inp['w_uq'] = nrm((L, Q_LORA, MLA_HEADS * (QK_NOPE + QK_ROPE)), Q_LORA ** -0.5)
    inp['g_kv_norm'] = gain((L, KV_LORA))
    inp['w_uk'] = nrm((L, KV_LORA, MLA_HEADS, QK_NOPE), KV_LORA ** -0.5)
    inp['w_uv'] = nrm((L, KV_LORA, MLA_HEADS, V_HEAD), KV_LORA ** -0.5)
    inp['w_o_mla'] = nrm((L, MLA_HEADS * V_HEAD, D), (MLA_HEADS * V_HEAD) ** -0.5)
    inp['mu_shift'] = jax.random.uniform(next(keys), (L, RWKV_COLS), jnp.float32)
    inp['w0_decay'] = jax.random.uniform(next(keys), (L, RWKV_W), jnp.float32, minval=-6.0, maxval=-1.0)
    inp['w_decay_up'] = nrm((L, DECAY_LORA, RWKV_W), 0.5 * DECAY_LORA ** -0.5)
    inp['a0'] = nrm((L, RWKV_W), 0.1)
    inp['w_aaa_up'] = nrm((L, AAA_LORA, RWKV_W), AAA_LORA ** -0.5)
    inp['k_k'] = 0.85 + nrm((L, RWKV_W), 0.02)
    inp['k_a'] = gain((L, RWKV_W))
    inp['r_k'] = nrm((L, RWKV_HEADS, RWKV_HEAD), 0.1)
    inp['ln_x_w'] = gain((L, RWKV_W))
    inp['ln_x_b'] = nrm((L, RWKV_W), 0.01)
    inp['w_o_rwkv'] = nrm((L, RWKV_W, D), RWKV_W ** -0.5)
    inp['g_mem_norm'] = gain((L, D))
    inp['w_mem_k'] = nrm((L, D, MEM_W), D ** -0.5)
    inp['w_mem_v'] = nrm((L, D, MEM_W), D ** -0.5)
    inp['w_o_mem'] = nrm((L, MEM_W, D), MEM_W ** -0.5)
    inp['w_out'] = nrm((L, D, D), D ** -0.5)
    inp['g_ffn_norm'] = gain((L, D))
    inp['w_router'] = nrm((L, D, N_EXPERTS), D ** -0.5)
    inp['b_router'] = nrm((L, N_EXPERTS), 0.01)
    inp['w_mlp1'] = nrm((L, N_EXPERTS, D, 2 * D_FF), D ** -0.5)
    inp['b_mlp1'] = nrm((L, N_EXPERTS, 2 * D_FF), 0.01)
    inp['w_mlp2'] = nrm((L, N_EXPERTS, D_FF, D), D_FF ** -0.5)
    inp['b_mlp2'] = nrm((L, N_EXPERTS, D), 0.01)
    inp['g_final'] = gain((D,))
    return inp


def reference(x_prompt, x_sample, mem_prompt, cache_kv_latent, cache_k_rope, page_table,
              state_rwkv, state_rwkv_shift, cache_mem_k, cache_mem_v,
              g_attn_norm, w_in, b_gate, g_q_norm, w_uq, g_kv_norm, w_uk, w_uv, w_o_mla,
              mu_shift, w0_decay, w_decay_up, a0, w_aaa_up, k_k, k_a, r_k, ln_x_w, ln_x_b, w_o_rwkv,
              g_mem_norm, w_mem_k, w_mem_v, w_o_mem, w_out,
              g_ffn_norm, w_router, b_router, w_mlp1, b_mlp1, w_mlp2, b_mlp2, g_final):
    layer_params = dict(g_attn_norm=g_attn_norm, w_in=w_in, b_gate=b_gate, g_q_norm=g_q_norm, w_uq=w_uq,
                        g_kv_norm=g_kv_norm, w_uk=w_uk, w_uv=w_uv, w_o_mla=w_o_mla, mu_shift=mu_shift,
                        w0_decay=w0_decay, w_decay_up=w_decay_up, a0=a0, w_aaa_up=w_aaa_up, k_k=k_k, k_a=k_a,
                        r_k=r_k, ln_x_w=ln_x_w, ln_x_b=ln_x_b, w_o_rwkv=w_o_rwkv, g_mem_norm=g_mem_norm,
                        w_mem_k=w_mem_k, w_mem_v=w_mem_v, w_o_mem=w_o_mem, w_out=w_out, g_ffn_norm=g_ffn_norm,
                        w_router=w_router, b_router=b_router, w_mlp1=w_mlp1, b_mlp1=b_mlp1,
                        w_mlp2=w_mlp2, b_mlp2=b_mlp2)
    B, T, _ = x_prompt.shape
    DB, TS, _ = x_sample.shape
    n_pages = PAST_LEN // PAGE_SIZE
    pos_p = jnp.arange(T, dtype=jnp.int32)
    pos_s = PAST_LEN + jnp.arange(TS, dtype=jnp.int32)
    h_p, h_s = x_prompt, x_sample
    kv_p, pe_p, st_p, sh_p, mk_p, mv_p = [], [], [], [], [], []
    kv_s, pe_s, st_s, sh_s = [], [], [], []
    for l in range(DEPTH):
        W = {name: arr[l] for name, arr in layer_params.items()}
        mem_n = rmsnorm(mem_prompt, W['g_mem_norm'])
        mk = (mem_n @ W['w_mem_k']).reshape(B, MEM_TOKENS, MEM_HEADS, MEM_HEAD)
        mv = (mem_n @ W['w_mem_v']).reshape(B, MEM_TOKENS, MEM_HEADS, MEM_HEAD)
        S0_p = jnp.zeros((B, RWKV_HEADS, RWKV_HEAD, RWKV_HEAD), jnp.float32)
        shift0_p = jnp.zeros((B, RWKV_COLS), x_prompt.dtype)
        h_p, c_p, k_p, S_p, last_p = trunk_layer(h_p, pos_p, mla_prompt_attn, S0_p, shift0_p, mk, mv, W)
        kv_p.append(c_p); pe_p.append(k_p); st_p.append(S_p); sh_p.append(last_p); mk_p.append(mk); mv_p.append(mv)
        past_lat = cache_kv_latent[l][page_table].reshape(DB, n_pages * PAGE_SIZE, KV_LORA)
        past_pe = cache_k_rope[l][page_table].reshape(DB, n_pages * PAGE_SIZE, QK_ROPE)
        attend_s = functools.partial(mla_decode_attn, past_lat=past_lat, past_pe=past_pe)
        h_s, c_s, k_s, S_s, last_s = trunk_layer(h_s, pos_s, attend_s, state_rwkv[l], state_rwkv_shift[l],
                                                 cache_mem_k[l], cache_mem_v[l], W)
        kv_s.append(c_s); pe_s.append(k_s); st_s.append(S_s); sh_s.append(last_s)
    y_prompt = rmsnorm(h_p, g_final)
    y_sample = rmsnorm(h_s, g_final)
    return (y_prompt, y_sample,
            jnp.stack(kv_p), jnp.stack(pe_p), jnp.stack(st_p), jnp.stack(sh_p), jnp.stack(mk_p), jnp.stack(mv_p),
            jnp.stack(kv_s), jnp.stack(pe_s), jnp.stack(st_s), jnp.stack(sh_s))
```

```python
import functools
import math

import numpy as np
import jax
import jax.numpy as jnp
from jax import lax
from jax.experimental import pallas as pl
from jax.experimental.pallas import tpu as pltpu

F32 = jnp.float32
BF16 = jnp.bfloat16

MLA_HEADS = 8
QK_NOPE = 64
QK_ROPE = 32
V_HEAD = 64
ROPE_THETA = 10000.0
RWKV_HEADS = 8
RWKV_HEAD = 64
DECAY_LORA = 64
AAA_LORA = 64
MEM_HEADS = 4
MEM_HEAD = 128
TOP_K = 4
SWIGLU_ALPHA = 1.702
SWIGLU_LIMIT = 7.0
RMS_EPS = 1e-6
GN_EPS = 64e-5
L2_EPS = 1e-12

LANES = 128
VMEM_LIMIT = 48 * 1024 * 1024

RWKV_CHUNK = 64
MOE_TM = 256
DECODE_PAGES = 16


def _cparams(sem, vmem=VMEM_LIMIT):
    return pltpu.CompilerParams(dimension_semantics=sem, vmem_limit_bytes=vmem)


def _rms(x, g, eps=RMS_EPS):
    return x * lax.rsqrt(jnp.mean(x * x, axis=-1, keepdims=True) + eps) * g


def _dot(a, b):
    return jnp.dot(a.astype(BF16), b.astype(BF16), preferred_element_type=F32)


def _dot_nt(a, b):
    return lax.dot_general(a.astype(BF16), b.astype(BF16), (((1,), (1,)), ((), ())),
                           preferred_element_type=F32)


def _split(a):
    hi = a.astype(BF16)
    lo = (a - hi.astype(F32)).astype(BF16)
    return hi, lo


def _dot3(a, b):
    ah, al = _split(a)
    bh, bl = _split(b)
    return (jnp.dot(ah, bh, preferred_element_type=F32)
            + (jnp.dot(ah, bl, preferred_element_type=F32)
               + jnp.dot(al, bh, preferred_element_type=F32)))


def _dot3_nt(a, b):
    ah, al = _split(a)
    bh, bl = _split(b)
    dn = (((1,), (1,)), ((), ()))
    return (lax.dot_general(ah, bh, dn, preferred_element_type=F32)
            + (lax.dot_general(ah, bl, dn, preferred_element_type=F32)
               + lax.dot_general(al, bh, dn, preferred_element_type=F32)))


def _sigmoid(x):
    return 1.0 / (1.0 + jnp.exp(-x))


def _mla_proj_prompt_kernel(x_ref, ga_ref, wcq_ref, gq_ref, wckv_ref, gkv_ref,
                            wkp_ref, wkpr_ref, wkpp_ref, wkprp_ref,
                            wq_ref, wqr_ref, wuk_ref, wuv_ref,
                            cos32_ref, sin32_ref, cost_ref, sint_ref,
                            ckv_out, kpe_out, q_out, k_out, v_out, *, scale):
    xn = _rms(x_ref[...], ga_ref[...]).astype(BF16)
    cqn = _rms(jnp.dot(xn, wcq_ref[...], preferred_element_type=F32), gq_ref[...]).astype(BF16)
    ckvn = _rms(jnp.dot(xn, wckv_ref[...], preferred_element_type=F32), gkv_ref[...])
    ckv_out[...] = ckvn
    kpe_out[...] = (jnp.dot(xn, wkp_ref[...], preferred_element_type=F32) * cos32_ref[...]
                    + jnp.dot(xn, wkpr_ref[...], preferred_element_type=F32) * sin32_ref[...])
    cost = cost_ref[...]
    sint = sint_ref[...]
    kpp = (jnp.dot(xn, wkpp_ref[...], preferred_element_type=F32) * cost
           + jnp.dot(xn, wkprp_ref[...], preferred_element_type=F32) * sint)
    qa = jnp.dot(cqn, wq_ref[...], preferred_element_type=F32)
    qb = jnp.dot(cqn, wqr_ref[...], preferred_element_type=F32)
    ckb = ckvn.astype(BF16)
    ka = jnp.dot(ckb, wuk_ref[...], preferred_element_type=F32)
    va = jnp.dot(ckb, wuv_ref[...], preferred_element_type=F32)
    for h in range(MLA_HEADS):
        sl = slice(LANES * h, LANES * (h + 1))
        q_out[h] = ((qa[:, sl] * cost + qb[:, sl] * sint) * scale).astype(BF16)
        k_out[h] = (ka[:, sl] + kpp).astype(BF16)
        v_out[h] = va[:, sl].astype(BF16)


def _mla_proj_sample_kernel(x_ref, ga_ref, wcq_ref, gq_ref, wckv_ref, gkv_ref,
                            wkp_ref, wkpr_ref, wqn_ref, wukt_ref, wqp_ref, wqpr_ref,
                            cos32_ref, sin32_ref,
                            ckv_out, kpe_out, qlat_out, qpe_out, *, scale):
    xn = _rms(x_ref[...], ga_ref[...]).astype(BF16)
    cqn = _rms(jnp.dot(xn, wcq_ref[...], preferred_element_type=F32), gq_ref[...]).astype(BF16)
    ckv_out[...] = _rms(jnp.dot(xn, wckv_ref[...], preferred_element_type=F32), gkv_ref[...])
    cos32 = cos32_ref[...]
    sin32 = sin32_ref[...]
    kpe_out[...] = (jnp.dot(xn, wkp_ref[...], preferred_element_type=F32) * cos32
                    + jnp.dot(xn, wkpr_ref[...], preferred_element_type=F32) * sin32)
    qn = jnp.dot(cqn, wqn_ref[...], preferred_element_type=F32)
    for h in range(MLA_HEADS):
        sl = slice(LANES * h, LANES * (h + 1))
        qlat_out[h] = (_dot(qn[:, sl], wukt_ref[h]) * scale).astype(BF16)
        qpe_out[h] = ((jnp.dot(cqn, wqp_ref[h], preferred_element_type=F32) * cos32
                       + jnp.dot(cqn, wqpr_ref[h], preferred_element_type=F32) * sin32) * scale)


def _full(shape):
    nd = len(shape)
    return pl.BlockSpec(shape, lambda *_: (0,) * nd)


def _rope_tables(pos):
    half = QK_ROPE // 2
    inv = jnp.exp(-math.log(ROPE_THETA) * jnp.arange(half, dtype=F32) / half)
    ang = pos.astype(F32)[:, None] * inv[None, :]
    cos = jnp.cos(ang)
    sin = jnp.sin(ang)
    return jnp.concatenate([cos, cos], axis=1), jnp.concatenate([sin, sin], axis=1)


def _rot_cols(w):
    half = QK_ROPE // 2
    return jnp.concatenate([-w[..., half:], w[..., :half]], axis=-1)


def _pad_cols(w, lo, total=LANES):
    pad = [(0, 0)] * (w.ndim - 1) + [(lo, total - lo - w.shape[-1])]
    return jnp.pad(w, pad)


def _mla_weights(W):
    d = W['w_in'].shape[0]
    q_lora = W['w_uq'].shape[0]
    kv_lora = W['w_uk'].shape[0]
    c0, c1, c2 = q_lora, q_lora + kv_lora, q_lora + kv_lora + QK_ROPE
    w_in = W['w_in']
    wcq = w_in[:, :c0].astype(BF16)
    wckv = w_in[:, c0:c1].astype(BF16)
    wkp = w_in[:, c1:c2]
    wuq = W['w_uq'].reshape(q_lora, MLA_HEADS, QK_NOPE + QK_ROPE)
    return dict(d=d, q_lora=q_lora, kv_lora=kv_lora, wcq=wcq, wckv=wckv, wkp=wkp, wuq=wuq,
                ga=W['g_attn_norm'].reshape(1, d), gq=W['g_q_norm'].reshape(1, q_lora),
                gkv=W['g_kv_norm'].reshape(1, kv_lora))


def mla_proj_prompt(x, pos_tables, W, tm):
    n, d = x.shape
    m = _mla_weights(W)
    q_lora, kv_lora = m['q_lora'], m['kv_lora']
    cos32, sin32 = pos_tables
    t = cos32.shape[0]
    ones = jnp.ones((t, QK_NOPE), F32)
    zeros_hi = jnp.zeros((t, LANES - QK_NOPE - QK_ROPE), F32)
    cost = jnp.concatenate([ones, cos32, zeros_hi], axis=1)
    sint = jnp.concatenate([jnp.zeros((t, QK_NOPE), F32), sin32, zeros_hi], axis=1)
    wkp = m['wkp']
    wkpr = _rot_cols(wkp)
    wuq = m['wuq']
    wq = _pad_cols(wuq, 0).reshape(q_lora, MLA_HEADS * LANES).astype(BF16)
    wqr = _pad_cols(_rot_cols(wuq[..., QK_NOPE:]), QK_NOPE).reshape(q_lora, MLA_HEADS * LANES).astype(BF16)
    wuk = _pad_cols(W['w_uk'], 0).reshape(kv_lora, MLA_HEADS * LANES).astype(BF16)
    wuv = _pad_cols(W['w_uv'], 0).reshape(kv_lora, MLA_HEADS * LANES).astype(BF16)
    args = (x, m['ga'], m['wcq'], m['gq'], m['wckv'], m['gkv'],
            wkp.astype(BF16), wkpr.astype(BF16),
            _pad_cols(wkp, QK_NOPE).astype(BF16), _pad_cols(wkpr, QK_NOPE).astype(BF16),
            wq, wqr, wuk, wuv, cos32, sin32, cost, sint)
    nt = t // tm
    row = lambda i: (i, 0)
    tab = lambda i: (i % nt, 0)
    in_specs = [pl.BlockSpec((tm, d), row)] + [_full(a.shape) for a in args[1:14]] + [
        pl.BlockSpec((tm, QK_ROPE), tab), pl.BlockSpec((tm, QK_ROPE), tab),
        pl.BlockSpec((tm, LANES), tab), pl.BlockSpec((tm, LANES), tab)]
    head = pl.BlockSpec((MLA_HEADS, tm, LANES), lambda i: (0, i, 0))
    out_shape = (jax.ShapeDtypeStruct((n, kv_lora), F32), jax.ShapeDtypeStruct((n, QK_ROPE), F32),
                 jax.ShapeDtypeStruct((MLA_HEADS, n, LANES), BF16),
                 jax.ShapeDtypeStruct((MLA_HEADS, n, LANES), BF16),
                 jax.ShapeDtypeStruct((MLA_HEADS, n, LANES), BF16))
    out_specs = (pl.BlockSpec((tm, kv_lora), row), pl.BlockSpec((tm, QK_ROPE), row), head, head, head)
    scale = (QK_NOPE + QK_ROPE) ** -0.5
    return pl.pallas_call(
        functools.partial(_mla_proj_prompt_kernel, scale=scale),
        grid=(n // tm,), in_specs=in_specs, out_specs=out_specs, out_shape=out_shape,
        compiler_params=_cparams(("parallel",)), name="mla_proj_prompt")(*args)


def mla_proj_sample(x, pos_tables, W):
    n, d = x.shape
    m = _mla_weights(W)
    q_lora, kv_lora = m['q_lora'], m['kv_lora']
    cos32, sin32 = pos_tables
    wkp = m['wkp']
    wuq = m['wuq']
    wqn = _pad_cols(wuq[..., :QK_NOPE], 0).reshape(q_lora, MLA_HEADS * LANES).astype(BF16)
    wukt = jnp.pad(jnp.transpose(W['w_uk'], (1, 2, 0)), ((0, 0), (0, LANES - QK_NOPE), (0, 0))).astype(BF16)
    wqp = jnp.transpose(wuq[..., QK_NOPE:], (1, 0, 2))
    args = (x, m['ga'], m['wcq'], m['gq'], m['wckv'], m['gkv'],
            wkp.astype(BF16), _rot_cols(wkp).astype(BF16), wqn, wukt,
            wqp.astype(BF16), _rot_cols(wqp).astype(BF16), cos32, sin32)
    out_shape = (jax.ShapeDtypeStruct((n, kv_lora), F32), jax.ShapeDtypeStruct((n, QK_ROPE), F32),
                 jax.ShapeDtypeStruct((MLA_HEADS, n, kv_lora), BF16),
                 jax.ShapeDtypeStruct((MLA_HEADS, n, QK_ROPE), F32))
    scale = (QK_NOPE + QK_ROPE) ** -0.5
    return pl.pallas_call(
        functools.partial(_mla_proj_sample_kernel, scale=scale),
        grid=(1,), in_specs=[_full(a.shape) for a in args],
        out_specs=tuple(_full(s.shape) for s in out_shape), out_shape=out_shape,
        compiler_params=_cparams(("arbitrary",)), name="mla_proj_sample")(*args)


def _in_proj_rest_kernel(x_ref, ga_ref, wrw_ref, wqm_ref, wg_ref, bg_ref,
                         rw_out, qm_out, g_out, *, mem_scale):
    xn = _rms(x_ref[...], ga_ref[...]).astype(BF16)
    rw_out[...] = jnp.dot(xn, wrw_ref[...], preferred_element_type=F32)
    qm_out[...] = (jnp.dot(xn, wqm_ref[...], preferred_element_type=F32) * mem_scale).astype(BF16)
    g_out[...] = _sigmoid(jnp.dot(xn, wg_ref[...], preferred_element_type=F32) + bg_ref[...]).astype(BF16)


def in_proj_rest(x, W, tm):
    n, d = x.shape
    q_lora = W['w_uq'].shape[0]
    kv_lora = W['w_uk'].shape[0]
    rw_cols = W['mu_shift'].shape[0]
    mem_w = MEM_HEADS * MEM_HEAD
    c2 = q_lora + kv_lora + QK_ROPE
    c3 = c2 + rw_cols
    c4 = c3 + mem_w
    w_in = W['w_in']
    n_gate = w_in.shape[1] - c4
    args = (x, W['g_attn_norm'].reshape(1, d), w_in[:, c2:c3].astype(BF16), w_in[:, c3:c4].astype(BF16),
            w_in[:, c4:].astype(BF16), W['b_gate'].reshape(1, n_gate))
    row = lambda i: (i, 0)
    out_shape = (jax.ShapeDtypeStruct((n, rw_cols), F32), jax.ShapeDtypeStruct((n, mem_w), BF16),
                 jax.ShapeDtypeStruct((n, n_gate), BF16))
    return pl.pallas_call(
        functools.partial(_in_proj_rest_kernel, mem_scale=MEM_HEAD ** -0.5),
        grid=(n // tm,),
        in_specs=[pl.BlockSpec((tm, d), row)] + [_full(a.shape) for a in args[1:]],
        out_specs=(pl.BlockSpec((tm, rw_cols), row), pl.BlockSpec((tm, mem_w), row),
                   pl.BlockSpec((tm, n_gate), row)),
        out_shape=out_shape, compiler_params=_cparams(("parallel",)), name="in_proj_rest")(*args)


def _mem_kv_kernel(m_ref, g_ref, wk_ref, wv_ref, k_out, v_out):
    mn = _rms(m_ref[...], g_ref[...]).astype(BF16)
    k_out[...] = jnp.dot(mn, wk_ref[...], preferred_element_type=F32)
    v_out[...] = jnp.dot(mn, wv_ref[...], preferred_element_type=F32)


def mem_kv(mem, W):
    n, d = mem.shape
    mem_w = MEM_HEADS * MEM_HEAD
    args = (mem, W['g_mem_norm'].reshape(1, d), W['w_mem_k'].astype(BF16), W['w_mem_v'].astype(BF16))
    out_shape = (jax.ShapeDtypeStruct((n, mem_w), F32), jax.ShapeDtypeStruct((n, mem_w), F32))
    return pl.pallas_call(
        _mem_kv_kernel, grid=(1,), in_specs=[_full(a.shape) for a in args],
        out_specs=(_full((n, mem_w)), _full((n, mem_w))), out_shape=out_shape,
        compiler_params=_cparams(("arbitrary",)), name="mem_kv")(*args)


def _flash_kernel(qi_tab, ki_tab, q_ref, k_ref, v_ref, o_ref, m_sc, l_sc, acc_sc, *, tq, tk):
    step = pl.program_id(2)
    qi = qi_tab[step]
    ki = ki_tab[step]

    @pl.when(ki == 0)
    def _():
        m_sc[...] = jnp.full(m_sc.shape, -jnp.inf, F32)
        l_sc[...] = jnp.zeros(l_sc.shape, F32)
        acc_sc[...] = jnp.zeros(acc_sc.shape, F32)

    def update(masked):
        s = lax.dot_general(q_ref[...], k_ref[...], (((1,), (1,)), ((), ())), preferred_element_type=F32)
        if masked:
            rows = qi * tq + lax.broadcasted_iota(jnp.int32, (tq, tk), 0)
            cols = ki * tk + lax.broadcasted_iota(jnp.int32, (tq, tk), 1)
            s = jnp.where(cols <= rows, s, -jnp.inf)
        m_prev = m_sc[...]
        m_new = jnp.maximum(m_prev, jnp.max(s, axis=-1, keepdims=True))
        alpha = jnp.exp(m_prev - m_new)
        p = jnp.exp(s - m_new)
        l_sc[...] = alpha * l_sc[...] + jnp.sum(p, axis=-1, keepdims=True)
        acc_sc[...] = alpha * acc_sc[...] + jnp.dot(p.astype(BF16), v_ref[...], preferred_element_type=F32)
        m_sc[...] = m_new

    crossing = (ki + 1) * tk - 1 > qi * tq
    pl.when(crossing)(lambda: update(True))
    pl.when(jnp.logical_not(crossing))(lambda: update(False))

    @pl.when(ki == ((qi + 1) * tq - 1) // tk)
    def _():
        o_ref[...] = (acc_sc[...] / l_sc[...]).astype(o_ref.dtype)


def flash_prompt(q, k, v, batch, tq, tk):
    h, n, dh = q.shape
    t = n // batch
    nq = t // tq
    pairs = [(i, j) for i in range(nq) for j in range(((i + 1) * tq - 1) // tk + 1)]
    qi_tab = jnp.asarray(np.array([p[0] for p in pairs], np.int32))
    ki_tab = jnp.asarray(np.array([p[1] for p in pairs], np.int32))
    nkb = t // tk
    grid_spec = pltpu.PrefetchScalarGridSpec(
        num_scalar_prefetch=2, grid=(batch, h, len(pairs)),
        in_specs=[pl.BlockSpec((None, tq, dh), lambda b, hh, s, qt, kt: (hh, b * nq + qt[s], 0)),
                  pl.BlockSpec((None, tk, dh), lambda b, hh, s, qt, kt: (hh, b * nkb + kt[s], 0)),
                  pl.BlockSpec((None, tk, dh), lambda b, hh, s, qt, kt: (hh, b * nkb + kt[s], 0))],
        out_specs=pl.BlockSpec((tq, dh), lambda b, hh, s, qt, kt: (b * nq + qt[s], hh)),
        scratch_shapes=[pltpu.VMEM((tq, 1), F32), pltpu.VMEM((tq, 1), F32), pltpu.VMEM((tq, dh), F32)])
    return pl.pallas_call(
        functools.partial(_flash_kernel, tq=tq, tk=tk), grid_spec=grid_spec,
        out_shape=jax.ShapeDtypeStruct((n, h * dh), BF16),
        compiler_params=_cparams(("parallel", "parallel", "arbitrary")), name="flash_prompt")(qi_tab, ki_tab, q, k, v)


def _decode_kernel(pt_ref, qlat_ref, qpe_ref, ckv_ref, kpe_ref, wuv_ref, *rest, n_pages_step, page):
    lat_refs = rest[:n_pages_step]
    rope_refs = rest[n_pages_step:2 * n_pages_step]
    o_ref, m_sc, l_sc, acc_sc = rest[2 * n_pages_step:]
    g = pl.program_id(1)

    @pl.when(g == 0)
    def _():
        m_sc[...] = jnp.full(m_sc.shape, -jnp.inf, F32)
        l_sc[...] = jnp.zeros(l_sc.shape, F32)
        acc_sc[...] = jnp.zeros(acc_sc.shape, F32)

    qlat = qlat_ref[0]
    qpe = qpe_ref[0]
    kall = jnp.concatenate([r[0, 0].astype(BF16) for r in lat_refs], axis=0)
    pall = jnp.concatenate([r[0, 0].astype(BF16) for r in rope_refs], axis=0)
    s = _dot_nt(qlat, kall) + _dot_nt(qpe, pall)
    m_prev = m_sc[...]
    m_new = jnp.maximum(m_prev, jnp.max(s, axis=-1, keepdims=True))
    alpha = jnp.exp(m_prev - m_new)
    p = jnp.exp(s - m_new)
    l_new = alpha * l_sc[...] + jnp.sum(p, axis=-1, keepdims=True)
    acc_new = alpha * acc_sc[...] + jnp.dot(p.astype(BF16), kall, preferred_element_type=F32)
    m_sc[...] = m_new
    l_sc[...] = l_new
    acc_sc[...] = acc_new

    @pl.when(g == pl.num_programs(1) - 1)
    def _():
        ckv = ckv_ref[0]
        kpe = kpe_ref[0]
        s_new = (jnp.sum(qlat.astype(F32) * ckv, axis=-1, keepdims=True)
                 + jnp.sum(qpe * kpe, axis=-1, keepdims=True))
        m_fin = jnp.maximum(m_new, s_new)
        a_fin = jnp.exp(m_new - m_fin)
        p_new = jnp.exp(s_new - m_fin)
        l_fin = a_fin * l_new + p_new
        o_lat = (a_fin * acc_new + p_new * ckv) / l_fin
        ov = _dot(o_lat, wuv_ref[...])
        hh = lax.broadcasted_iota(jnp.int32, ov.shape, 0)
        cc = lax.broadcasted_iota(jnp.int32, ov.shape, 1) // V_HEAD
        o_ref[0] = jnp.sum(jnp.where(hh == cc, ov, 0.0), axis=0, keepdims=True)


def decode_attn(qlat, qpe, ckv, kpe, cache_lat, cache_rope, page_table, w_uv):
    db, h, kv = qlat.shape
    n_pool, page, _ = cache_lat.shape
    n_pages = page_table.shape[1]
    gp = min(DECODE_PAGES, n_pages)
    assert n_pages % gp == 0
    wuv = w_uv.reshape(kv, h * V_HEAD).astype(BF16)

    def lat_spec(i):
        return pl.BlockSpec((1, 1, page, kv), lambda b, g, pt: (0, pt[b, g * gp + i], 0, 0))

    def rope_spec(i):
        return pl.BlockSpec((1, 1, page, QK_ROPE), lambda b, g, pt: (0, pt[b, g * gp + i], 0, 0))

    per_b = lambda shape: pl.BlockSpec((1,) + shape, lambda b, g, pt: (b, 0, 0))
    in_specs = ([per_b((h, kv)), per_b((h, QK_ROPE)), per_b((1, kv)), per_b((1, QK_ROPE)),
                 pl.BlockSpec(wuv.shape, lambda b, g, pt: (0, 0))]
                + [lat_spec(i) for i in range(gp)] + [rope_spec(i) for i in range(gp)])
    grid_spec = pltpu.PrefetchScalarGridSpec(
        num_scalar_prefetch=1, grid=(db, n_pages // gp), in_specs=in_specs,
        out_specs=per_b((1, h * V_HEAD)),
        scratch_shapes=[pltpu.VMEM((h, 1), F32), pltpu.VMEM((h, 1), F32), pltpu.VMEM((h, kv), F32)])
    lat4 = cache_lat.reshape(1, n_pool, page, kv)
    rope4 = cache_rope.reshape(1, n_pool, page, QK_ROPE)
    return pl.pallas_call(
        functools.partial(_decode_kernel, n_pages_step=gp, page=page), grid_spec=grid_spec,
        out_shape=jax.ShapeDtypeStruct((db, 1, h * V_HEAD), F32),
        compiler_params=_cparams(("parallel", "arbitrary")), name="decode_attn")(
            page_table, qlat, qpe, ckv, kpe, wuv, *([lat4] * gp), *([rope4] * gp))


def _mem_attn_prompt_kernel(q_ref, k_ref, v_ref, o_ref):
    q = q_ref[...]
    k = k_ref[0].astype(BF16)
    v = v_ref[0].astype(BF16)
    for h in range(MEM_HEADS):
        sl = slice(MEM_HEAD * h, MEM_HEAD * (h + 1))
        s = _dot_nt(q[:, sl], k[:, sl])
        p = jnp.exp(s - jnp.max(s, axis=-1, keepdims=True))
        o = jnp.dot(p.astype(BF16), v[:, sl], preferred_element_type=F32) / jnp.sum(p, axis=-1, keepdims=True)
        o_ref[:, sl] = o.astype(o_ref.dtype)


def mem_attn_prompt(qm, mk, mv, tm):
    n, w = qm.shape
    b, m, _ = mk.shape
    nt = n // b // tm
    kv_spec = pl.BlockSpec((1, m, w), lambda i: (i // nt, 0, 0))
    return pl.pallas_call(
        _mem_attn_prompt_kernel, grid=(n // tm,),
        in_specs=[pl.BlockSpec((tm, w), lambda i: (i, 0)), kv_spec, kv_spec],
        out_specs=pl.BlockSpec((tm, w), lambda i: (i, 0)),
        out_shape=jax.ShapeDtypeStruct((n, w), BF16),
        compiler_params=_cparams(("parallel",)), name="mem_attn_prompt")(qm, mk, mv)


def _mem_attn_sample_kernel(q_ref, k_ref, v_ref, o_ref, *, tb):
    for bl in range(tb):
        q = q_ref[bl:bl + 1, :].astype(F32)
        k = k_ref[bl]
        v = v_ref[bl]
        qk = k * q
        for h in range(MEM_HEADS):
            sl = slice(MEM_HEAD * h, MEM_HEAD * (h + 1))
            s = jnp.sum(qk[:, sl], axis=-1, keepdims=True)
            p = jnp.exp(s - jnp.max(s, axis=0, keepdims=True))
            o = jnp.sum(p * v[:, sl], axis=0, keepdims=True) / jnp.sum(p, axis=0, keepdims=True)
            o_ref[bl:bl + 1, sl] = o.astype(o_ref.dtype)


def mem_attn_sample(qm, mk, mv, tb=8):
    db, w = qm.shape
    m = mk.shape[1]
    kv_spec = pl.BlockSpec((tb, m, w), lambda i: (i, 0, 0))
    return pl.pallas_call(
        functools.partial(_mem_attn_sample_kernel, tb=tb), grid=(db // tb,),
        in_specs=[pl.BlockSpec((tb, w), lambda i: (i, 0)), kv_spec, kv_spec],
        out_specs=pl.BlockSpec((tb, w), lambda i: (i, 0)),
        out_shape=jax.ShapeDtypeStruct((db, w), BF16),
        compiler_params=_cparams(("parallel",)), name="mem_attn_sample")(qm, mk, mv)


def _rwkv_prep(xr, lora_w, w0, a0, kk_w, ka_w, gmat):
    w = RWKV_HEADS * RWKV_HEAD
    r = xr[:, 0:w]
    k = xr[:, w:2 * w]
    v = xr[:, 2 * w:3 * w]
    tail = xr[:, 3 * w:3 * w + DECAY_LORA + AAA_LORA]
    lane = lax.broadcasted_iota(jnp.int32, tail.shape, 1)
    z = jnp.where(lane < DECAY_LORA, jnp.tanh(tail), tail)
    lo = _dot3(z, lora_w)
    dec = lo[:, :w] + w0
    aaa = lo[:, w:] + a0
    u = -dec
    softplus = jnp.maximum(u, 0.0) + jnp.log(1.0 + jnp.exp(-jnp.abs(u)))
    logw = -jnp.exp(-softplus - 0.5)
    a_sig = _sigmoid(aaa)
    kk = k * kk_w
    norm = jnp.sqrt(_dot3(kk * kk, gmat))
    kkn = kk / jnp.maximum(norm, L2_EPS)
    k2 = k * (1.0 + (a_sig - 1.0) * ka_w)
    return r, k2, v, logw, -kkn, kkn * a_sig


def _rwkv_post(y, r, k2, v, rk, lnw, lnb, gmat):
    inv_n = 1.0 / RWKV_HEAD
    mu = _dot3(y, gmat) * inv_n
    d = y - mu
    var = _dot3(d * d, gmat) * inv_n
    yn = d * lax.rsqrt(var + GN_EPS) * lnw + lnb
    return yn + _dot3(r * k2 * rk, gmat) * v


def _rwkv_prompt_kernel(rw_ref, mu_ref, lora_ref, w0_ref, a0_ref, kk_ref, ka_ref, rk_ref,
                        lnw_ref, lnb_ref, g_ref, out_ref, zout_ref, prev_sc, z_sc, *, chunk):
    c = chunk
    n_pairs = RWKV_HEADS // 2

    @pl.when(pl.program_id(1) == 0)
    def _():
        prev_sc[...] = jnp.zeros(prev_sc.shape, F32)
        z_sc[...] = jnp.zeros(z_sc.shape, F32)

    rw = rw_ref[...]
    row = lax.broadcasted_iota(jnp.int32, rw.shape, 0)
    prev = jnp.where(row == 0, prev_sc[...], pltpu.roll(rw, 1, 0))
    prev_sc[...] = rw[c - 1:c, :]
    xr = rw + (prev - rw) * mu_ref[...]
    gmat = g_ref[...]
    r, k2, v, logw, av, bv = _rwkv_prep(xr, lora_ref[...], w0_ref[...], a0_ref[...],
                                        kk_ref[...], ka_ref[...], gmat)

    ti = lax.broadcasted_iota(jnp.int32, (c, c), 0)
    si = lax.broadcasted_iota(jnp.int32, (c, c), 1)
    cum = _dot3((si <= ti).astype(F32), logw)
    clast = cum[c - 1:c, :]
    e_neg = jnp.exp(-cum)
    e_rest = jnp.exp(clast - cum)
    at = av * jnp.exp(cum - logw)
    bt = bv * e_neg
    kt = k2 * e_neg
    rt = r * jnp.exp(cum)
    bg = bv * e_rest
    kg = k2 * e_rest
    glast = jnp.exp(clast)

    er = lax.broadcasted_iota(jnp.int32, (2 * c, LANES), 0) // c
    el = lax.broadcasted_iota(jnp.int32, (2 * c, LANES), 1) // RWKV_HEAD
    emask = er == el
    br = lax.broadcasted_iota(jnp.int32, (2 * c, 2 * c), 0)
    bc = lax.broadcasted_iota(jnp.int32, (2 * c, 2 * c), 1)
    same = (br // c) == (bc // c)
    strict = jnp.logical_and(same, (br % c) > (bc % c))
    incl = jnp.logical_and(same, (br % c) >= (bc % c))
    eye2c = (br == bc).astype(F32)
    lane_lo = lax.broadcasted_iota(jnp.int32, (c, LANES), 1) < RWKV_HEAD
    dr = lax.broadcasted_iota(jnp.int32, (LANES, LANES), 0)
    dc = lax.broadcasted_iota(jnp.int32, (LANES, LANES), 1)

    def expand(x):
        return jnp.where(emask, jnp.concatenate([x, x], axis=0), 0.0)

    def dup(x):
        return jnp.concatenate([x, x], axis=0)

    ys = []
    for p in range(n_pairs):
        sl = slice(LANES * p, LANES * (p + 1))
        btp = bt[:, sl]
        ktp = kt[:, sl]
        lhs = jnp.concatenate([at[:, sl], rt[:, sl]], axis=0)
        rhs = jnp.concatenate([jnp.where(lane_lo, btp, 0.0), jnp.where(lane_lo, 0.0, btp),
                               jnp.where(lane_lo, ktp, 0.0), jnp.where(lane_lo, 0.0, ktp)], axis=0)
        ap = _dot3_nt(lhs, rhs)
        nmat = jnp.where(strict, dup(ap[0:c, 0:2 * c]), 0.0)
        akm = jnp.where(strict, dup(ap[0:c, 2 * c:4 * c]), 0.0)
        rbm = jnp.where(incl, dup(ap[c:2 * c, 0:2 * c]), 0.0)
        rkm = jnp.where(incl, dup(ap[c:2 * c, 2 * c:4 * c]), 0.0)
        tmat = eye2c + nmat
        pw = nmat
        for _ in range(int(math.log2(c)) - 1):
            pw = _dot3(pw, pw)
            tmat = tmat + _dot3(tmat, pw)
        ve = expand(v[:, sl])
        ae = expand(at[:, sl])
        wu = _dot3(tmat, jnp.concatenate([ae, _dot3(akm, ve)], axis=1))
        z = z_sc[p]
        ue = _dot3(wu[:, :LANES], z) + wu[:, LANES:]
        bkt = jnp.concatenate([expand(bg[:, sl]), expand(kg[:, sl])], axis=0).T
        gcol = jnp.sum(jnp.where(dr == dc, glast[:, sl], 0.0), axis=1, keepdims=True)
        z_new = gcol * z + _dot3(bkt, jnp.concatenate([ue, ve], axis=0))
        z_sc[p] = z_new
        zout_ref[0, p] = z_new
        ye = (_dot3(jnp.concatenate([expand(rt[:, sl]), rbm, rkm], axis=1),
                    jnp.concatenate([z, ue, ve], axis=0)))
        ys.append(ye[0:c, :] + ye[c:2 * c, :])
    y = jnp.concatenate(ys, axis=1)
    out = _rwkv_post(y, r, k2, v, rk_ref[...], lnw_ref[...], lnb_ref[...], gmat)
    out_ref[...] = out.astype(out_ref.dtype)


def _rwkv_params(W):
    w = RWKV_HEADS * RWKV_HEAD
    lora = jnp.zeros((DECAY_LORA + AAA_LORA, 2 * w), F32)
    lora = lora.at[:DECAY_LORA, :w].set(W['w_decay_up']).at[DECAY_LORA:, w:].set(W['w_aaa_up'])
    head = np.arange(w) // RWKV_HEAD
    gmat = jnp.asarray((head[:, None] == head[None, :]).astype(np.float32))
    row = lambda a: a.reshape(1, -1)
    return dict(mu=row(W['mu_shift']), lora=lora, w0=row(W['w0_decay']), a0=row(W['a0']),
                kk=row(W['k_k']), ka=row(W['k_a']), rk=row(W['r_k']), lnw=row(W['ln_x_w']),
                lnb=row(W['ln_x_b']), gmat=gmat)


def rwkv_prompt(rw, batch, W):
    n, cols = rw.shape
    t = n // batch
    c = RWKV_CHUNK
    nc = t // c
    w = RWKV_HEADS * RWKV_HEAD
    pr = _rwkv_params(W)
    args = (rw, pr['mu'], pr['lora'], pr['w0'], pr['a0'], pr['kk'], pr['ka'], pr['rk'], pr['lnw'],
            pr['lnb'], pr['gmat'])
    in_specs = [pl.BlockSpec((c, cols), lambda b, i: (b * nc + i, 0))] + [
        pl.BlockSpec(a.shape, lambda b, i: (0, 0)) for a in args[1:]]
    n_pairs = RWKV_HEADS // 2
    out_shape = (jax.ShapeDtypeStruct((n, w), BF16),
                 jax.ShapeDtypeStruct((batch, n_pairs, LANES, LANES), F32))
    return pl.pallas_call(
        functools.partial(_rwkv_prompt_kernel, chunk=c), grid=(batch, nc), in_specs=in_specs,
        out_specs=(pl.BlockSpec((c, w), lambda b, i: (b * nc + i, 0)),
                   pl.BlockSpec((1, n_pairs, LANES, LANES), lambda b, i: (b, 0, 0, 0))),
        out_shape=out_shape,
        scratch_shapes=[pltpu.VMEM((1, cols), F32), pltpu.VMEM((n_pairs, LANES, LANES), F32)],
        compiler_params=_cparams(("parallel", "arbitrary")), name="rwkv_prompt")(*args)


def _rwkv_sample_prep_kernel(rw_ref, sh_ref, mu_ref, lora_ref, w0_ref, a0_ref, kk_ref, ka_ref, g_ref,
                             r_out, k_out, v_out, w_out, a_out, b_out):
    rw = rw_ref[...]
    xr = rw + (sh_ref[...] - rw) * mu_ref[...]
    r, k2, v, logw, av, bv = _rwkv_prep(xr, lora_ref[...], w0_ref[...], a0_ref[...],
                                        kk_ref[...], ka_ref[...], g_ref[...])
    r_out[...] = r
    k_out[...] = k2
    v_out[...] = v
    w_out[...] = jnp.exp(logw)
    a_out[...] = av
    b_out[...] = bv


def _rwkv_sample_state_kernel(s_ref, r_ref, k_ref, v_ref, w_ref, a_ref, b_ref, s_out, y_out, *, tb):
    n = RWKV_HEAD
    eye = lax.broadcasted_iota(jnp.int32, (n, n), 0) == lax.broadcasted_iota(jnp.int32, (n, n), 1)
    for bl in range(tb):
        for h in range(RWKV_HEADS):
            s = s_ref[bl, h]
            row = lambda ref: ref[bl, h:h + 1, :]
            sa = jnp.sum(s * row(a_ref), axis=1, keepdims=True)
            vcol = jnp.sum(jnp.where(eye, row(v_ref), 0.0), axis=1, keepdims=True)
            s_new = s * row(w_ref) + sa * row(b_ref) + vcol * row(k_ref)
            s_out[bl, h] = s_new
            ycol = jnp.sum(s_new * row(r_ref), axis=1, keepdims=True)
            y_out[bl, h:h + 1, :] = jnp.sum(jnp.where(eye, ycol, 0.0), axis=0, keepdims=True)


def _rwkv_sample_post_kernel(y_ref, r_ref, k_ref, v_ref, rk_ref, lnw_ref, lnb_ref, g_ref, o_ref):
    o_ref[...] = _rwkv_post(y_ref[...], r_ref[...], k_ref[...], v_ref[...], rk_ref[...],
                            lnw_ref[...], lnb_ref[...], g_ref[...]).astype(o_ref.dtype)


def rwkv_sample(rw, shift0, state, W, tb=8):
    db, cols = rw.shape
    w = RWKV_HEADS * RWKV_HEAD
    pr = _rwkv_params(W)
    args = (rw, shift0, pr['mu'], pr['lora'], pr['w0'], pr['a0'], pr['kk'], pr['ka'], pr['gmat'])
    vec = jax.ShapeDtypeStruct((db, w), F32)
    r, k2, v, wd, av, bv = pl.pallas_call(
        _rwkv_sample_prep_kernel, grid=(1,), in_specs=[_full(a.shape) for a in args],
        out_specs=tuple(_full((db, w)) for _ in range(6)), out_shape=(vec,) * 6,
        compiler_params=_cparams(("arbitrary",)), name="rwkv_sample_prep")(*args)
    heads = lambda a: a.reshape(db, RWKV_HEADS, RWKV_HEAD)
    st_spec = pl.BlockSpec((tb, RWKV_HEADS, RWKV_HEAD, RWKV_HEAD), lambda i: (i, 0, 0, 0))
    hv_spec = pl.BlockSpec((tb, RWKV_HEADS, RWKV_HEAD), lambda i: (i, 0, 0))
    s_new, y = pl.pallas_call(
        functools.partial(_rwkv_sample_state_kernel, tb=tb), grid=(db // tb,),
        in_specs=[st_spec] + [hv_spec] * 6, out_specs=(st_spec, hv_spec),
        out_shape=(jax.ShapeDtypeStruct(state.shape, F32),
                   jax.ShapeDtypeStruct((db, RWKV_HEADS, RWKV_HEAD), F32)),
        compiler_params=_cparams(("parallel",)), name="rwkv_sample_state")(
            state, heads(r), heads(k2), heads(v), heads(wd), heads(av), heads(bv))
    pargs = (y.reshape(db, w), r, k2, v, pr['rk'], pr['lnw'], pr['lnb'], pr['gmat'])
    ob = pl.pallas_call(
        _rwkv_sample_post_kernel, grid=(1,), in_specs=[_full(a.shape) for a in pargs],
        out_specs=_full((db, w)), out_shape=jax.ShapeDtypeStruct((db, w), BF16),
        compiler_params=_cparams(("arbitrary",)), name="rwkv_sample_post")(*pargs)
    return ob, s_new


def _merge_kernel(oa_ref, ob_ref, oc_ref, g_ref, x_ref, woa_ref, wob_ref, woc_ref, wout_ref,
                  gffn_ref, wr_ref, br_ref, h_out, hn_out, ti_out, tg_out, *, d):
    o_a = jnp.dot(oa_ref[...], woa_ref[...], preferred_element_type=F32)
    o_b = jnp.dot(ob_ref[...], wob_ref[...], preferred_element_type=F32)
    o_c = jnp.dot(oc_ref[...], woc_ref[...], preferred_element_type=F32)
    g = g_ref[...].astype(F32)
    merged = g[:, 0:d] * o_a + g[:, d:2 * d] * o_b + g[:, 2 * d:3 * d] * o_c
    h = x_ref[...] + _dot(merged, wout_ref[...])
    h_out[...] = h
    hn = _rms(h, gffn_ref[...])
    hn_out[...] = hn
    logits = _dot3(hn, wr_ref[...]) + br_ref[...]
    n_e = logits.shape[1]
    lane = lax.broadcasted_iota(jnp.int32, logits.shape, 1)
    cur = logits
    vals, idxs = [], []
    for _ in range(TOP_K):
        mx = jnp.max(cur, axis=-1, keepdims=True)
        ix = jnp.min(jnp.where(cur == mx, lane, n_e), axis=-1, keepdims=True)
        vals.append(mx)
        idxs.append(ix)
        cur = jnp.where(lane == ix, -jnp.inf, cur)
    es = [jnp.exp(vk - vals[0]) for vk in vals]
    den = es[0]
    for e in es[1:]:
        den = den + e
    lane_k = lax.broadcasted_iota(jnp.int32, ti_out.shape, 1)
    ti = jnp.zeros(ti_out.shape, jnp.int32)
    tg = jnp.zeros(tg_out.shape, F32)
    for kk in range(TOP_K):
        ti = jnp.where(lane_k == kk, idxs[kk], ti)
        tg = jnp.where(lane_k == kk, es[kk] / den, tg)
    ti_out[...] = ti
    tg_out[...] = tg


def merge_out(oa, w_oa, ob, oc, gates, x, W, tm):
    n, d = x.shape
    n_e = W['w_router'].shape[1]
    args = (oa, ob, oc, gates, x, w_oa.astype(BF16), W['w_o_rwkv'].astype(BF16), W['w_o_mem'].astype(BF16),
            W['w_out'].astype(BF16), W['g_ffn_norm'].reshape(1, d), W['w_router'], W['b_router'].reshape(1, n_e))
    row = lambda i: (i, 0)
    in_specs = [pl.BlockSpec((tm, a.shape[1]), row) for a in args[:5]] + [_full(a.shape) for a in args[5:]]
    out_shape = (jax.ShapeDtypeStruct((n, d), F32), jax.ShapeDtypeStruct((n, d), F32),
                 jax.ShapeDtypeStruct((n, TOP_K), jnp.int32), jax.ShapeDtypeStruct((n, TOP_K), F32))
    out_specs = (pl.BlockSpec((tm, d), row), pl.BlockSpec((tm, d), row),
                 pl.BlockSpec((tm, TOP_K), row), pl.BlockSpec((tm, TOP_K), row))
    return pl.pallas_call(
        functools.partial(_merge_kernel, d=d), grid=(n // tm,), in_specs=in_specs, out_specs=out_specs,
        out_shape=out_shape, compiler_params=_cparams(("parallel",)), name="merge_out")(*args)


def _gather_rows_kernel(tok_ref, src_ref, dst_ref, sem, *, rows):
    base = pl.program_id(0) * rows

    def issue(r, carry):
        pltpu.make_async_copy(src_ref.at[pl.ds(tok_ref[r], 1)], dst_ref.at[pl.ds(base + r, 1)], sem).start()
        return carry

    lax.fori_loop(0, rows, issue, 0)

    def drain(r, carry):
        pltpu.make_async_copy(src_ref.at[pl.ds(0, 1)], dst_ref.at[pl.ds(base, 1)], sem).wait()
        return carry

    lax.fori_loop(0, rows, drain, 0)


def gather_rows(src, row_tok, rows=256):
    n_rows = row_tok.shape[0]
    return pl.pallas_call(
        functools.partial(_gather_rows_kernel, rows=rows), grid=(n_rows // rows,),
        in_specs=[pl.BlockSpec((rows,), lambda i: (i,), memory_space=pltpu.SMEM),
                  pl.BlockSpec(memory_space=pl.ANY)],
        out_specs=pl.BlockSpec(memory_space=pl.ANY),
        out_shape=jax.ShapeDtypeStruct((n_rows, src.shape[1]), src.dtype),
        scratch_shapes=[pltpu.SemaphoreType.DMA(())],
        compiler_params=_cparams(("arbitrary",)), name="moe_gather")(row_tok, src)


def _gmm_kernel(be_ref, nu_ref, x_ref, w1g_ref, w1l_ref, b1g_ref, b1l_ref, w2_ref, b2_ref, y_ref):
    i = pl.program_id(0)

    @pl.when(i < nu_ref[0])
    def _():
        x = x_ref[...].astype(BF16)
        hg = jnp.dot(x, w1g_ref[0], preferred_element_type=F32) + b1g_ref[0]
        hl = jnp.dot(x, w1l_ref[0], preferred_element_type=F32) + b1l_ref[0]
        x_glu = jnp.minimum(hg, SWIGLU_LIMIT)
        x_lin = jnp.clip(hl, -SWIGLU_LIMIT, SWIGLU_LIMIT)
        act = x_glu * _sigmoid(SWIGLU_ALPHA * x_glu) * (x_lin + 1.0)
        y_ref[...] = _dot(act, w2_ref[0]) + b2_ref[0]

    @pl.when(i >= nu_ref[0])
    def _():
        y_ref[...] = jnp.zeros(y_ref.shape, y_ref.dtype)


def grouped_mlp(xb, block_e, n_used, W, tm):
    n_rows, d = xb.shape
    n_e = W['w_mlp1'].shape[0]
    d_ff = W['w_mlp2'].shape[1]
    w1 = W['w_mlp1']
    w1g = w1[:, :, 0::2].astype(BF16)
    w1l = w1[:, :, 1::2].astype(BF16)
    b1 = W['b_mlp1']
    b1g = b1[:, 0::2].reshape(n_e, 1, d_ff)
    b1l = b1[:, 1::2].reshape(n_e, 1, d_ff)
    w2 = W['w_mlp2'].astype(BF16)
    b2 = W['b_mlp2'].reshape(n_e, 1, d)
    ex = lambda i, be, nu: (be[i], 0, 0)
    grid_spec = pltpu.PrefetchScalarGridSpec(
        num_scalar_prefetch=2, grid=(n_rows // tm,),
        in_specs=[pl.BlockSpec((tm, d), lambda i, be, nu: (i, 0)),
                  pl.BlockSpec((1, d, d_ff), ex), pl.BlockSpec((1, d, d_ff), ex),
                  pl.BlockSpec((1, 1, d_ff), ex), pl.BlockSpec((1, 1, d_ff), ex),
                  pl.BlockSpec((1, d_ff, d), ex), pl.BlockSpec((1, 1, d), ex)],
        out_specs=pl.BlockSpec((tm, d), lambda i, be, nu: (i, 0)))
    return pl.pallas_call(
        _gmm_kernel, grid_spec=grid_spec, out_shape=jax.ShapeDtypeStruct((n_rows, d), F32),
        compiler_params=_cparams(("arbitrary",)), name="moe_gmm")(block_e, n_used, xb, w1g, w1l, b1g, b1l, w2, b2)


def _combine_kernel(dest_ref, gate_ref, h_ref, gf_ref, yb_ref, o_ref, buf, sem, *, tc):
    def issue(t, carry):
        for k in range(TOP_K):
            pltpu.make_async_copy(yb_ref.at[pl.ds(dest_ref[TOP_K * t + k], 1)],
                                  buf.at[k, pl.ds(t, 1)], sem).start()
        return carry

    lax.fori_loop(0, tc, issue, 0)

    def drain(t, carry):
        for k in range(TOP_K):
            pltpu.make_async_copy(yb_ref.at[pl.ds(0, 1)], buf.at[k, pl.ds(0, 1)], sem).wait()
        return carry

    lax.fori_loop(0, tc, drain, 0)
    gate = gate_ref[...]
    y = h_ref[...]
    for k in range(TOP_K):
        y = y + gate[:, k:k + 1] * buf[k]
    o_ref[...] = _rms(y, gf_ref[...])


def moe_combine(yb, dest, gate, h, g_final, tc):
    n, d = h.shape
    return pl.pallas_call(
        functools.partial(_combine_kernel, tc=tc), grid=(n // tc,),
        in_specs=[pl.BlockSpec((tc * TOP_K,), lambda i: (i,), memory_space=pltpu.SMEM),
                  pl.BlockSpec((tc, TOP_K), lambda i: (i, 0)),
                  pl.BlockSpec((tc, d), lambda i: (i, 0)),
                  pl.BlockSpec((1, d), lambda i: (0, 0)),
                  pl.BlockSpec(memory_space=pl.ANY)],
        out_specs=pl.BlockSpec((tc, d), lambda i: (i, 0)),
        out_shape=jax.ShapeDtypeStruct((n, d), F32),
        scratch_shapes=[pltpu.VMEM((TOP_K, tc, d), F32), pltpu.SemaphoreType.DMA(())],
        compiler_params=_cparams(("arbitrary",)), name="moe_combine")(dest, gate, h, g_final.reshape(1, d), yb)


def _routing(top_i, n_e, tm):
    n_tok = top_i.shape[0]
    n_assign = n_tok * TOP_K
    flat_e = top_i.reshape(-1)
    onehot = (flat_e[:, None] == jnp.arange(n_e, dtype=jnp.int32)[None, :]).astype(jnp.int32)
    csum = jnp.cumsum(onehot, axis=0)
    counts = csum[-1]
    padded = ((counts + tm - 1) // tm) * tm
    pend = jnp.cumsum(padded)
    pstart = pend - padded
    dest = jnp.sum(onehot * (csum - onehot + pstart[None, :]), axis=1).astype(jnp.int32)
    n_blocks = -(-n_assign // tm) + n_e
    n_rows = n_blocks * tm
    row_tok = jnp.zeros((n_rows,), jnp.int32).at[dest].set(jnp.arange(n_assign, dtype=jnp.int32) // TOP_K)
    block_e = jnp.minimum(
        jnp.searchsorted(pend, jnp.arange(n_blocks, dtype=jnp.int32) * tm, side='right'), n_e - 1).astype(jnp.int32)
    n_used = (pend[-1:] // tm).astype(jnp.int32)
    return dest, row_tok, block_e, n_used


def kernel(x_prompt, x_sample, mem_prompt, cache_kv_latent, cache_k_rope, page_table, state_rwkv, state_rwkv_shift, cache_mem_k, cache_mem_v, g_attn_norm, w_in, b_gate, g_q_norm, w_uq, g_kv_norm, w_uk, w_uv, w_o_mla, mu_shift, w0_decay, w_decay_up, a0, w_aaa_up, k_k, k_a, r_k, ln_x_w, ln_x_b, w_o_rwkv, g_mem_norm, w_mem_k, w_mem_v, w_o_mem, w_out, g_ffn_norm, w_router, b_router, w_mlp1, b_mlp1, w_mlp2, b_mlp2, g_final):
    layer_params = dict(g_attn_norm=g_attn_norm, w_in=w_in, b_gate=b_gate, g_q_norm=g_q_norm, w_uq=w_uq,
                        g_kv_norm=g_kv_norm, w_uk=w_uk, w_uv=w_uv, w_o_mla=w_o_mla, mu_shift=mu_shift,
                        w0_decay=w0_decay, w_decay_up=w_decay_up, a0=a0, w_aaa_up=w_aaa_up, k_k=k_k, k_a=k_a,
                        r_k=r_k, ln_x_w=ln_x_w, ln_x_b=ln_x_b, w_o_rwkv=w_o_rwkv, g_mem_norm=g_mem_norm,
                        w_mem_k=w_mem_k, w_mem_v=w_mem_v, w_o_mem=w_o_mem, w_out=w_out, g_ffn_norm=g_ffn_norm,
                        w_router=w_router, b_router=b_router, w_mlp1=w_mlp1, b_mlp1=b_mlp1,
                        w_mlp2=w_mlp2, b_mlp2=b_mlp2)
    depth = w_in.shape[0]
    assert depth == 1, "single-layer trunk"
    W = {name: arr[0] for name, arr in layer_params.items()}
    B, T, D = x_prompt.shape
    DB, TS, _ = x_sample.shape
    assert TS == 1, "sample group decodes one token per sequence"
    n_pages = page_table.shape[1]
    page = cache_kv_latent.shape[2]
    past_len = n_pages * page
    kv_lora = w_uk.shape[1]
    mem_tokens = mem_prompt.shape[1]
    mem_w = MEM_HEADS * MEM_HEAD
    n_e = w_router.shape[2]
    rw_w = RWKV_HEADS * RWKV_HEAD

    tm = min(256, T)
    xp = x_prompt.reshape(B * T, D)
    xs = x_sample.reshape(DB, D)

    tab_p = _rope_tables(jnp.arange(T, dtype=jnp.int32))
    ckv_p, kpe_p, q_h, k_h, v_h = mla_proj_prompt(xp, tab_p, W, tm)
    rw_p, qm_p, gates_p = in_proj_rest(xp, W, tm)
    mk_p, mv_p = mem_kv(mem_prompt.reshape(B * mem_tokens, D), W)
    tq = min(512, T)
    oa_p = flash_prompt(q_h, k_h, v_h, B, tq, tq)
    oc_p = mem_attn_prompt(qm_p, mk_p.reshape(B, mem_tokens, mem_w), mv_p.reshape(B, mem_tokens, mem_w), tm)
    ob_p, z_p = rwkv_prompt(rw_p, B, W)
    woa_p = jnp.pad(W['w_o_mla'].reshape(MLA_HEADS, V_HEAD, D),
                    ((0, 0), (0, LANES - V_HEAD), (0, 0))).reshape(MLA_HEADS * LANES, D)
    h_p, hn_p, ti_p, tg_p = merge_out(oa_p, woa_p, ob_p, oc_p, gates_p, xp, W, tm)

    tab_s = _rope_tables(jnp.full((DB,), past_len, jnp.int32))
    ckv_s, kpe_s, qlat_s, qpe_s = mla_proj_sample(xs, tab_s, W)
    rw_s, qm_s, gates_s = in_proj_rest(xs, W, DB)
    oa_s = decode_attn(jnp.transpose(qlat_s, (1, 0, 2)), jnp.transpose(qpe_s, (1, 0, 2)),
                       ckv_s.reshape(DB, 1, kv_lora), kpe_s.reshape(DB, 1, QK_ROPE),
                       cache_kv_latent[0], cache_k_rope[0], page_table, W['w_uv']).reshape(DB, MLA_HEADS * V_HEAD)
    oc_s = mem_attn_sample(qm_s, cache_mem_k[0].reshape(DB, mem_tokens, mem_w),
                           cache_mem_v[0].reshape(DB, mem_tokens, mem_w))
    ob_s, st_s = rwkv_sample(rw_s, state_rwkv_shift[0], state_rwkv[0], W)
    h_s, hn_s, ti_s, tg_s = merge_out(oa_s.astype(BF16), W['w_o_mla'], ob_s, oc_s, gates_s, xs, W, DB)

    hn_all = jnp.concatenate([hn_p, hn_s], axis=0)
    top_i = jnp.concatenate([ti_p, ti_s], axis=0)
    dest, row_tok, block_e, n_used = _routing(top_i, n_e, MOE_TM)
    xb = gather_rows(hn_all, row_tok)
    yb = grouped_mlp(xb, block_e, n_used, W, MOE_TM)
    n_p = B * T
    y_p = moe_combine(yb, dest[:n_p * TOP_K], tg_p, h_p, g_final, min(128, T))
    y_s = moe_combine(yb, dest[n_p * TOP_K:], tg_s, h_s, g_final, DB)

    zq = z_p.reshape(B, RWKV_HEADS // 2, 2, RWKV_HEAD, 2, RWKV_HEAD)
    st_p = jnp.stack([zq[:, :, 0, :, 0, :], zq[:, :, 1, :, 1, :]], axis=2)
    st_p = jnp.swapaxes(st_p.reshape(B, RWKV_HEADS, RWKV_HEAD, RWKV_HEAD), -1, -2)
    lead = lambda a: a[None]
    return (y_p.reshape(B, T, D), y_s.reshape(DB, TS, D),
            lead(ckv_p.reshape(B, T, kv_lora)), lead(kpe_p.reshape(B, T, QK_ROPE)),
            lead(st_p), lead(rw_p.reshape(B, T, -1)[:, -1]),
            lead(mk_p.reshape(B, mem_tokens, MEM_HEADS, MEM_HEAD)),
            lead(mv_p.reshape(B, mem_tokens, MEM_HEADS, MEM_HEAD)),
            lead(ckv_s.reshape(DB, TS, kv_lora)), lead(kpe_s.reshape(DB, TS, QK_ROPE)),
            lead(st_s), lead(rw_s))
```

```python
import functools
import math

import numpy as np
import jax
import jax.numpy as jnp
from jax import lax
from jax.experimental import pallas as pl
from jax.experimental.pallas import tpu as pltpu

F32 = jnp.float32
BF16 = jnp.bfloat16

MLA_HEADS = 8
QK_NOPE = 64
QK_ROPE = 32
V_HEAD = 64
ROPE_THETA = 10000.0
RWKV_HEADS = 8
RWKV_HEAD = 64
DECAY_LORA = 64
AAA_LORA = 64
MEM_HEADS = 4
MEM_HEAD = 128
TOP_K = 4
SWIGLU_ALPHA = 1.702
SWIGLU_LIMIT = 7.0
RMS_EPS = 1e-6
GN_EPS = 64e-5
L2_EPS = 1e-12

LANES = 128
VMEM_LIMIT = 48 * 1024 * 1024

RWKV_CHUNK = 64
MOE_TM = 256
DECODE_PAGES = 16
FLASH_TQ = 512
FLASH_SUBTILES = 2


def _cparams(sem, vmem=VMEM_LIMIT):
    return pltpu.CompilerParams(dimension_semantics=sem, vmem_limit_bytes=vmem)


def _rms(x, g, eps=RMS_EPS):
    return x * lax.rsqrt(jnp.mean(x * x, axis=-1, keepdims=True) + eps) * g


def _dot(a, b):
    return jnp.dot(a.astype(BF16), b.astype(BF16), preferred_element_type=F32)


def _dot_nt(a, b):
    return lax.dot_general(a.astype(BF16), b.astype(BF16), (((1,), (1,)), ((), ())),
                           preferred_element_type=F32)


def _split(a):
    hi = a.astype(BF16)
    lo = (a - hi.astype(F32)).astype(BF16)
    return hi, lo


def _dot3(a, b):
    ah, al = _split(a)
    bh, bl = _split(b)
    return (jnp.dot(ah, bh, preferred_element_type=F32)
            + (jnp.dot(ah, bl, preferred_element_type=F32)
               + jnp.dot(al, bh, preferred_element_type=F32)))


def _dot2_exact_rhs(a, b):
    ah, al = _split(a)
    bb = b.astype(BF16)
    return jnp.dot(ah, bb, preferred_element_type=F32) + jnp.dot(al, bb, preferred_element_type=F32)


def _dot2_exact_lhs(a, b):
    aa = a.astype(BF16)
    bh, bl = _split(b)
    return jnp.dot(aa, bh, preferred_element_type=F32) + jnp.dot(aa, bl, preferred_element_type=F32)


def _sigmoid(x):
    return 1.0 / (1.0 + jnp.exp(-x))


def _mla_proj_prompt_kernel(x_ref, ga_ref, wcq_ref, gq_ref, wckv_ref, gkv_ref,
                            wkp_ref, wkpr_ref, wkpp_ref, wkprp_ref,
                            wq_ref, wqr_ref, wuk_ref, wuv_ref,
                            cos32_ref, sin32_ref, cost_ref, sint_ref,
                            ckv_out, kpe_out, q_out, k_out, v_out, *, scale):
    xn = _rms(x_ref[...], ga_ref[...]).astype(BF16)
    cqn = _rms(jnp.dot(xn, wcq_ref[...], preferred_element_type=F32), gq_ref[...]).astype(BF16)
    ckvn = _rms(jnp.dot(xn, wckv_ref[...], preferred_element_type=F32), gkv_ref[...])
    ckv_out[...] = ckvn
    kpe_out[...] = (jnp.dot(xn, wkp_ref[...], preferred_element_type=F32) * cos32_ref[...]
                    + jnp.dot(xn, wkpr_ref[...], preferred_element_type=F32) * sin32_ref[...])
    cost = cost_ref[...]
    sint = sint_ref[...]
    kpp = (jnp.dot(xn, wkpp_ref[...], preferred_element_type=F32) * cost
           + jnp.dot(xn, wkprp_ref[...], preferred_element_type=F32) * sint)
    qa = jnp.dot(cqn, wq_ref[...], preferred_element_type=F32)
    qb = jnp.dot(cqn, wqr_ref[...], preferred_element_type=F32)
    ckb = ckvn.astype(BF16)
    ka = jnp.dot(ckb, wuk_ref[...], preferred_element_type=F32)
    va = jnp.dot(ckb, wuv_ref[...], preferred_element_type=F32)
    ones_lane = lax.broadcasted_iota(jnp.int32, cost.shape, 1) == V_HEAD
    for h in range(MLA_HEADS):
        sl = slice(LANES * h, LANES * (h + 1))
        q_out[h] = ((qa[:, sl] * cost + qb[:, sl] * sint) * scale).astype(BF16)
        k_out[h] = (ka[:, sl] + kpp).astype(BF16)
        v_out[h] = jnp.where(ones_lane, 1.0, va[:, sl]).astype(BF16)


def _mla_proj_sample_kernel(x_ref, ga_ref, wcq_ref, gq_ref, wckv_ref, gkv_ref,
                            wkp_ref, wkpr_ref, wqn_ref, wukt_ref, wqp_ref, wqpr_ref,
                            cos32_ref, sin32_ref,
                            ckv_out, kpe_out, qlat_out, qpe_out, *, scale):
    xn = _rms(x_ref[...], ga_ref[...]).astype(BF16)
    cqn = _rms(jnp.dot(xn, wcq_ref[...], preferred_element_type=F32), gq_ref[...]).astype(BF16)
    ckv_out[...] = _rms(jnp.dot(xn, wckv_ref[...], preferred_element_type=F32), gkv_ref[...])
    cos32 = cos32_ref[...]
    sin32 = sin32_ref[...]
    kpe_out[...] = (jnp.dot(xn, wkp_ref[...], preferred_element_type=F32) * cos32
                    + jnp.dot(xn, wkpr_ref[...], preferred_element_type=F32) * sin32)
    qn = jnp.dot(cqn, wqn_ref[...], preferred_element_type=F32)
    for h in range(MLA_HEADS):
        sl = slice(LANES * h, LANES * (h + 1))
        qlat_out[h] = (_dot(qn[:, sl], wukt_ref[h]) * scale).astype(BF16)
        qpe_out[h] = ((jnp.dot(cqn, wqp_ref[h], preferred_element_type=F32) * cos32
                       + jnp.dot(cqn, wqpr_ref[h], preferred_element_type=F32) * sin32) * scale)


def _full(shape):
    nd = len(shape)
    return pl.BlockSpec(shape, lambda *_: (0,) * nd)


def _rope_tables(pos):
    half = QK_ROPE // 2
    inv = jnp.exp(-math.log(ROPE_THETA) * jnp.arange(half, dtype=F32) / half)
    ang = pos.astype(F32)[:, None] * inv[None, :]
    cos = jnp.cos(ang)
    sin = jnp.sin(ang)
    return jnp.concatenate([cos, cos], axis=1), jnp.concatenate([sin, sin], axis=1)


def _rot_cols(w):
    half = QK_ROPE // 2
    return jnp.concatenate([-w[..., half:], w[..., :half]], axis=-1)


def _pad_cols(w, lo, total=LANES):
    pad = [(0, 0)] * (w.ndim - 1) + [(lo, total - lo - w.shape[-1])]
    return jnp.pad(w, pad)


def _mla_weights(W):
    d = W['w_in'].shape[0]
    q_lora = W['w_uq'].shape[0]
    kv_lora = W['w_uk'].shape[0]
    c0, c1, c2 = q_lora, q_lora + kv_lora, q_lora + kv_lora + QK_ROPE
    w_in = W['w_in']
    wcq = w_in[:, :c0].astype(BF16)
    wckv = w_in[:, c0:c1].astype(BF16)
    wkp = w_in[:, c1:c2]
    wuq = W['w_uq'].reshape(q_lora, MLA_HEADS, QK_NOPE + QK_ROPE)
    return dict(d=d, q_lora=q_lora, kv_lora=kv_lora, wcq=wcq, wckv=wckv, wkp=wkp, wuq=wuq,
                ga=W['g_attn_norm'].reshape(1, d), gq=W['g_q_norm'].reshape(1, q_lora),
                gkv=W['g_kv_norm'].reshape(1, kv_lora))


def mla_proj_prompt(x, pos_tables, W, tm):
    n, d = x.shape
    m = _mla_weights(W)
    q_lora, kv_lora = m['q_lora'], m['kv_lora']
    cos32, sin32 = pos_tables
    t = cos32.shape[0]
    ones = jnp.ones((t, QK_NOPE), F32)
    zeros_hi = jnp.zeros((t, LANES - QK_NOPE - QK_ROPE), F32)
    cost = jnp.concatenate([ones, cos32, zeros_hi], axis=1)
    sint = jnp.concatenate([jnp.zeros((t, QK_NOPE), F32), sin32, zeros_hi], axis=1)
    wkp = m['wkp']
    wkpr = _rot_cols(wkp)
    wuq = m['wuq']
    wq = _pad_cols(wuq, 0).reshape(q_lora, MLA_HEADS * LANES).astype(BF16)
    wqr = _pad_cols(_rot_cols(wuq[..., QK_NOPE:]), QK_NOPE).reshape(q_lora, MLA_HEADS * LANES).astype(BF16)
    wuk = _pad_cols(W['w_uk'], 0).reshape(kv_lora, MLA_HEADS * LANES).astype(BF16)
    wuv = _pad_cols(W['w_uv'], 0).reshape(kv_lora, MLA_HEADS * LANES).astype(BF16)
    args = (x, m['ga'], m['wcq'], m['gq'], m['wckv'], m['gkv'],
            wkp.astype(BF16), wkpr.astype(BF16),
            _pad_cols(wkp, QK_NOPE).astype(BF16), _pad_cols(wkpr, QK_NOPE).astype(BF16),
            wq, wqr, wuk, wuv, cos32, sin32, cost, sint)
    nt = t // tm
    row = lambda i: (i, 0)
    tab = lambda i: (i % nt, 0)
    in_specs = [pl.BlockSpec((tm, d), row)] + [_full(a.shape) for a in args[1:14]] + [
        pl.BlockSpec((tm, QK_ROPE), tab), pl.BlockSpec((tm, QK_ROPE), tab),
        pl.BlockSpec((tm, LANES), tab), pl.BlockSpec((tm, LANES), tab)]
    head = pl.BlockSpec((MLA_HEADS, tm, LANES), lambda i: (0, i, 0))
    out_shape = (jax.ShapeDtypeStruct((n, kv_lora), F32), jax.ShapeDtypeStruct((n, QK_ROPE), F32),
                 jax.ShapeDtypeStruct((MLA_HEADS, n, LANES), BF16),
                 jax.ShapeDtypeStruct((MLA_HEADS, n, LANES), BF16),
                 jax.ShapeDtypeStruct((MLA_HEADS, n, LANES), BF16))
    out_specs = (pl.BlockSpec((tm, kv_lora), row), pl.BlockSpec((tm, QK_ROPE), row), head, head, head)
    scale = (QK_NOPE + QK_ROPE) ** -0.5 * math.log2(math.e)
    return pl.pallas_call(
        functools.partial(_mla_proj_prompt_kernel, scale=scale),
        grid=(n // tm,), in_specs=in_specs, out_specs=out_specs, out_shape=out_shape,
        compiler_params=_cparams(("parallel",)), name="mla_proj_prompt")(*args)


def mla_proj_sample(x, pos_tables, W):
    n, d = x.shape
    m = _mla_weights(W)
    q_lora, kv_lora = m['q_lora'], m['kv_lora']
    cos32, sin32 = pos_tables
    wkp = m['wkp']
    wuq = m['wuq']
    wqn = _pad_cols(wuq[..., :QK_NOPE], 0).reshape(q_lora, MLA_HEADS * LANES).astype(BF16)
    wukt = jnp.pad(jnp.transpose(W['w_uk'], (1, 2, 0)), ((0, 0), (0, LANES - QK_NOPE), (0, 0))).astype(BF16)
    wqp = jnp.transpose(wuq[..., QK_NOPE:], (1, 0, 2))
    args = (x, m['ga'], m['wcq'], m['gq'], m['wckv'], m['gkv'],
            wkp.astype(BF16), _rot_cols(wkp).astype(BF16), wqn, wukt,
            wqp.astype(BF16), _rot_cols(wqp).astype(BF16), cos32, sin32)
    out_shape = (jax.ShapeDtypeStruct((n, kv_lora), F32), jax.ShapeDtypeStruct((n, QK_ROPE), F32),
                 jax.ShapeDtypeStruct((MLA_HEADS, n, kv_lora), BF16),
                 jax.ShapeDtypeStruct((MLA_HEADS, n, QK_ROPE), F32))
    scale = (QK_NOPE + QK_ROPE) ** -0.5
    return pl.pallas_call(
        functools.partial(_mla_proj_sample_kernel, scale=scale),
        grid=(1,), in_specs=[_full(a.shape) for a in args],
        out_specs=tuple(_full(s.shape) for s in out_shape), out_shape=out_shape,
        compiler_params=_cparams(("arbitrary",)), name="mla_proj_sample")(*args)


def _in_proj_rest_kernel(x_ref, ga_ref, wrw_ref, wqm_ref, wg_ref, bg_ref,
                         rw_out, qm_out, g_out, *, mem_scale):
    xn = _rms(x_ref[...], ga_ref[...]).astype(BF16)
    rw_out[...] = jnp.dot(xn, wrw_ref[...], preferred_element_type=F32)
    qm_out[...] = (jnp.dot(xn, wqm_ref[...], preferred_element_type=F32) * mem_scale).astype(BF16)
    g_out[...] = _sigmoid(jnp.dot(xn, wg_ref[...], preferred_element_type=F32) + bg_ref[...]).astype(BF16)


def in_proj_rest(x, W, tm):
    n, d = x.shape
    q_lora = W['w_uq'].shape[0]
    kv_lora = W['w_uk'].shape[0]
    rw_cols = W['mu_shift'].shape[0]
    mem_w = MEM_HEADS * MEM_HEAD
    c2 = q_lora + kv_lora + QK_ROPE
    c3 = c2 + rw_cols
    c4 = c3 + mem_w
    w_in = W['w_in']
    n_gate = w_in.shape[1] - c4
    args = (x, W['g_attn_norm'].reshape(1, d), w_in[:, c2:c3].astype(BF16), w_in[:, c3:c4].astype(BF16),
            w_in[:, c4:].astype(BF16), W['b_gate'].reshape(1, n_gate))
    row = lambda i: (i, 0)
    out_shape = (jax.ShapeDtypeStruct((n, rw_cols), F32), jax.ShapeDtypeStruct((n, mem_w), BF16),
                 jax.ShapeDtypeStruct((n, n_gate), BF16))
    return pl.pallas_call(
        functools.partial(_in_proj_rest_kernel, mem_scale=MEM_HEAD ** -0.5),
        grid=(n // tm,),
        in_specs=[pl.BlockSpec((tm, d), row)] + [_full(a.shape) for a in args[1:]],
        out_specs=(pl.BlockSpec((tm, rw_cols), row), pl.BlockSpec((tm, mem_w), row),
                   pl.BlockSpec((tm, n_gate), row)),
        out_shape=out_shape, compiler_params=_cparams(("parallel",)), name="in_proj_rest")(*args)


def _mem_kv_kernel(m_ref, g_ref, wk_ref, wv_ref, k_out, v_out):
    mn = _rms(m_ref[...], g_ref[...]).astype(BF16)
    k_out[...] = jnp.dot(mn, wk_ref[...], preferred_element_type=F32)
    v_out[...] = jnp.dot(mn, wv_ref[...], preferred_element_type=F32)


def mem_kv(mem, W):
    n, d = mem.shape
    mem_w = MEM_HEADS * MEM_HEAD
    args = (mem, W['g_mem_norm'].reshape(1, d), W['w_mem_k'].astype(BF16), W['w_mem_v'].astype(BF16))
    out_shape = (jax.ShapeDtypeStruct((n, mem_w), F32), jax.ShapeDtypeStruct((n, mem_w), F32))
    return pl.pallas_call(
        _mem_kv_kernel, grid=(1,), in_specs=[_full(a.shape) for a in args],
        out_specs=(_full((n, mem_w)), _full((n, mem_w))), out_shape=out_shape,
        compiler_params=_cparams(("arbitrary",)), name="mem_kv")(*args)


def _flash_kernel(q_ref, k_ref, v_ref, o_ref, *scratch, tq, n_sub):
    qi = pl.program_id(2)
    sub = tq // n_sub
    m_scs = scratch[:n_sub]
    acc_scs = scratch[n_sub:2 * n_sub]
    s_scs = scratch[2 * n_sub:]
    for u in range(n_sub):
        m_scs[u][...] = jnp.full(m_scs[u].shape, -jnp.inf, F32)
        acc_scs[u][...] = jnp.zeros(acc_scs[u].shape, F32)

    def scores(j, slot):
        kb = k_ref[pl.ds(pl.multiple_of(j * tq, tq), tq), :]
        for u in range(n_sub):
            s_scs[slot][u] = lax.dot_general(q_ref[u * sub:(u + 1) * sub, :], kb, (((1,), (1,)), ((), ())),
                                            preferred_element_type=F32)

    def consume(j, slot, masked):
        vb = v_ref[pl.ds(pl.multiple_of(j * tq, tq), tq), :]
        for u in range(n_sub):
            s = s_scs[slot][u]
            if masked:
                rows = u * sub + lax.broadcasted_iota(jnp.int32, (sub, tq), 0)
                cols = lax.broadcasted_iota(jnp.int32, (sub, tq), 1)
                s = jnp.where(cols <= rows, s, -jnp.inf)
            m_prev = m_scs[u][...]
            m_new = jnp.maximum(m_prev, jnp.max(s, axis=-1, keepdims=True))
            alpha = jnp.exp2(m_prev - m_new)
            p = jnp.exp2(s - jnp.tile(m_new, (1, tq // LANES)))
            acc_scs[u][...] = alpha * acc_scs[u][...] + jnp.dot(p.astype(BF16), vb, preferred_element_type=F32)
            m_scs[u][...] = m_new

    scores(0, 0)

    def body(i, carry):
        scores(2 * i + 1, 1)
        consume(2 * i, 0, False)
        scores(2 * i + 2, 0)
        consume(2 * i + 1, 1, False)
        return carry

    lax.fori_loop(0, qi // 2, body, 0)

    @pl.when(qi % 2 == 1)
    def _():
        scores(qi, 1)
        consume(qi - 1, 0, False)
        consume(qi, 1, True)

    @pl.when(qi % 2 == 0)
    def _():
        consume(qi, 0, True)

    for u in range(n_sub):
        acc = acc_scs[u][...]
        o_ref[u * sub:(u + 1) * sub, :] = (acc / acc[:, V_HEAD:V_HEAD + 1]).astype(o_ref.dtype)


def flash_prompt(q, k, v, batch, tq):
    h, n, dh = q.shape
    t = n // batch
    nq = t // tq
    kv_spec = pl.BlockSpec((None, t, dh), lambda b, hh, i: (hh, b, 0))
    n_sub = FLASH_SUBTILES
    sub = tq // n_sub
    return pl.pallas_call(
        functools.partial(_flash_kernel, tq=tq, n_sub=n_sub), grid=(batch, h, nq),
        in_specs=[pl.BlockSpec((None, tq, dh), lambda b, hh, i: (hh, b * nq + i, 0)), kv_spec, kv_spec],
        out_specs=pl.BlockSpec((tq, dh), lambda b, hh, i: (b * nq + i, hh)),
        out_shape=jax.ShapeDtypeStruct((n, h * dh), BF16),
        scratch_shapes=([pltpu.VMEM((sub, LANES), F32)] * n_sub + [pltpu.VMEM((sub, dh), F32)] * n_sub
                        + [pltpu.VMEM((n_sub, sub, tq), F32)] * 2),
        compiler_params=_cparams(("parallel", "parallel", "arbitrary")), name="flash_prompt")(q, k, v)


def _decode_kernel(pt_ref, qlat_ref, qpe_ref, ckv_ref, kpe_ref, wuv_ref, *rest, n_pages_step, page):
    lat_refs = rest[:n_pages_step]
    rope_refs = rest[n_pages_step:2 * n_pages_step]
    o_ref, m_sc, l_sc, acc_sc = rest[2 * n_pages_step:]
    g = pl.program_id(1)

    @pl.when(g == 0)
    def _():
        m_sc[...] = jnp.full(m_sc.shape, -jnp.inf, F32)
        l_sc[...] = jnp.zeros(l_sc.shape, F32)
        acc_sc[...] = jnp.zeros(acc_sc.shape, F32)

    qlat = qlat_ref[0]
    qpe = qpe_ref[0]
    kall = jnp.concatenate([r[0, 0].astype(BF16) for r in lat_refs], axis=0)
    pall = jnp.concatenate([r[0, 0].astype(BF16) for r in rope_refs], axis=0)
    s = _dot_nt(qlat, kall) + _dot_nt(qpe, pall)
    m_prev = m_sc[...]
    m_new = jnp.maximum(m_prev, jnp.max(s, axis=-1, keepdims=True))
    alpha = jnp.exp(m_prev - m_new)
    p = jnp.exp(s - m_new)
    l_new = alpha * l_sc[...] + jnp.sum(p, axis=-1, keepdims=True)
    acc_new = alpha * acc_sc[...] + jnp.dot(p.astype(BF16), kall, preferred_element_type=F32)
    m_sc[...] = m_new
    l_sc[...] = l_new
    acc_sc[...] = acc_new

    @pl.when(g == pl.num_programs(1) - 1)
    def _():
        ckv = ckv_ref[0]
        kpe = kpe_ref[0]
        s_new = (jnp.sum(qlat.astype(F32) * ckv, axis=-1, keepdims=True)
                 + jnp.sum(qpe * kpe, axis=-1, keepdims=True))
        m_fin = jnp.maximum(m_new, s_new)
        a_fin = jnp.exp(m_new - m_fin)
        p_new = jnp.exp(s_new - m_fin)
        l_fin = a_fin * l_new + p_new
        o_lat = (a_fin * acc_new + p_new * ckv) / l_fin
        ov = _dot(o_lat, wuv_ref[...])
        hh = lax.broadcasted_iota(jnp.int32, ov.shape, 0)
        cc = lax.broadcasted_iota(jnp.int32, ov.shape, 1) // V_HEAD
        o_ref[0] = jnp.sum(jnp.where(hh == cc, ov, 0.0), axis=0, keepdims=True)


def decode_attn(qlat, qpe, ckv, kpe, cache_lat, cache_rope, page_table, w_uv):
    db, h, kv = qlat.shape
    _, n_pool, page, _ = cache_lat.shape
    n_pages = page_table.shape[1]
    gp = min(DECODE_PAGES, n_pages)
    assert n_pages % gp == 0
    wuv = w_uv.reshape(kv, h * V_HEAD).astype(BF16)

    def lat_spec(i):
        return pl.BlockSpec((1, 1, page, kv), lambda b, g, pt: (0, pt[b, g * gp + i], 0, 0))

    def rope_spec(i):
        return pl.BlockSpec((1, 1, page, QK_ROPE), lambda b, g, pt: (0, pt[b, g * gp + i], 0, 0))

    per_b = lambda shape: pl.BlockSpec((1,) + shape, lambda b, g, pt: (b, 0, 0))
    in_specs = ([per_b((h, kv)), per_b((h, QK_ROPE)), per_b((1, kv)), per_b((1, QK_ROPE)),
                 pl.BlockSpec(wuv.shape, lambda b, g, pt: (0, 0))]
                + [lat_spec(i) for i in range(gp)] + [rope_spec(i) for i in range(gp)])
    grid_spec = pltpu.PrefetchScalarGridSpec(
        num_scalar_prefetch=1, grid=(db, n_pages // gp), in_specs=in_specs,
        out_specs=per_b((1, h * V_HEAD)),
        scratch_shapes=[pltpu.VMEM((h, 1), F32), pltpu.VMEM((h, 1), F32), pltpu.VMEM((h, kv), F32)])
    return pl.pallas_call(
        functools.partial(_decode_kernel, n_pages_step=gp, page=page), grid_spec=grid_spec,
        out_shape=jax.ShapeDtypeStruct((db, 1, h * V_HEAD), F32),
        compiler_params=_cparams(("parallel", "arbitrary")), name="decode_attn")(
            page_table, qlat, qpe, ckv, kpe, wuv, *([cache_lat] * gp), *([cache_rope] * gp))


def _mem_attn_prompt_kernel(q_ref, k_ref, v_ref, o_ref):
    q = q_ref[...]
    k = k_ref[0].astype(BF16)
    v = v_ref[0].astype(BF16)
    for h in range(MEM_HEADS):
        sl = slice(MEM_HEAD * h, MEM_HEAD * (h + 1))
        s = _dot_nt(q[:, sl], k[:, sl])
        p = jnp.exp(s - jnp.max(s, axis=-1, keepdims=True))
        o = jnp.dot(p.astype(BF16), v[:, sl], preferred_element_type=F32) / jnp.sum(p, axis=-1, keepdims=True)
        o_ref[:, sl] = o.astype(o_ref.dtype)


def mem_attn_prompt(qm, mk, mv, tm):
    n, w = qm.shape
    b, m, _ = mk.shape
    nt = n // b // tm
    kv_spec = pl.BlockSpec((1, m, w), lambda i: (i // nt, 0, 0))
    return pl.pallas_call(
        _mem_attn_prompt_kernel, grid=(n // tm,),
        in_specs=[pl.BlockSpec((tm, w), lambda i: (i, 0)), kv_spec, kv_spec],
        out_specs=pl.BlockSpec((tm, w), lambda i: (i, 0)),
        out_shape=jax.ShapeDtypeStruct((n, w), BF16),
        compiler_params=_cparams(("parallel",)), name="mem_attn_prompt")(qm, mk, mv)


def _mem_attn_sample_kernel(q_ref, k_ref, v_ref, o_ref, *, tb):
    for bl in range(tb):
        q = q_ref[bl:bl + 1, :].astype(F32)
        k = k_ref[bl]
        v = v_ref[bl]
        qk = k * q
        for h in range(MEM_HEADS):
            sl = slice(MEM_HEAD * h, MEM_HEAD * (h + 1))
            s = jnp.sum(qk[:, sl], axis=-1, keepdims=True)
            p = jnp.exp(s - jnp.max(s, axis=0, keepdims=True))
            o = jnp.sum(p * v[:, sl], axis=0, keepdims=True) / jnp.sum(p, axis=0, keepdims=True)
            o_ref[bl:bl + 1, sl] = o.astype(o_ref.dtype)


def mem_attn_sample(qm, mk, mv, tb=8):
    db, w = qm.shape
    m = mk.shape[1]
    kv_spec = pl.BlockSpec((tb, m, w), lambda i: (i, 0, 0))
    return pl.pallas_call(
        functools.partial(_mem_attn_sample_kernel, tb=tb), grid=(db // tb,),
        in_specs=[pl.BlockSpec((tb, w), lambda i: (i, 0)), kv_spec, kv_spec],
        out_specs=pl.BlockSpec((tb, w), lambda i: (i, 0)),
        out_shape=jax.ShapeDtypeStruct((db, w), BF16),
        compiler_params=_cparams(("parallel",)), name="mem_attn_sample")(qm, mk, mv)


def _rwkv_prep(xr, lora_w, w0, a0, kk_w, ka_w, gmat):
    w = RWKV_HEADS * RWKV_HEAD
    r = xr[:, 0:w]
    k = xr[:, w:2 * w]
    v = xr[:, 2 * w:3 * w]
    tail = xr[:, 3 * w:3 * w + DECAY_LORA + AAA_LORA]
    lane = lax.broadcasted_iota(jnp.int32, tail.shape, 1)
    z = jnp.where(lane < DECAY_LORA, jnp.tanh(tail), tail)
    lo = _dot3(z, lora_w)
    dec = lo[:, :w] + w0
    aaa = lo[:, w:] + a0
    u = -dec
    softplus = jnp.maximum(u, 0.0) + jnp.log(1.0 + jnp.exp(-jnp.abs(u)))
    logw = -jnp.exp(-softplus - 0.5)
    a_sig = _sigmoid(aaa)
    kk = k * kk_w
    norm = jnp.sqrt(_dot2_exact_rhs(kk * kk, gmat))
    kkn = kk / jnp.maximum(norm, L2_EPS)
    k2 = k * (1.0 + (a_sig - 1.0) * ka_w)
    return r, k2, v, logw, -kkn, kkn * a_sig


def _rwkv_post(y, r, k2, v, rk, lnw, lnb, gmat):
    inv_n = 1.0 / RWKV_HEAD
    mu = _dot2_exact_rhs(y, gmat) * inv_n
    d = y - mu
    var = _dot2_exact_rhs(d * d, gmat) * inv_n
    yn = d * lax.rsqrt(var + GN_EPS) * lnw + lnb
    return yn + _dot2_exact_rhs(r * k2 * rk, gmat) * v


def _rwkv_prompt_kernel(rw_ref, mu_ref, lora_ref, w0_ref, a0_ref, kk_ref, ka_ref, rk_ref,
                        lnw_ref, lnb_ref, g_ref, out_ref, zout_ref, prev_sc, z_sc, *, chunk):
    c = chunk
    n_pairs = RWKV_HEADS // 2

    @pl.when(pl.program_id(1) == 0)
    def _():
        prev_sc[...] = jnp.zeros(prev_sc.shape, F32)
        z_sc[...] = jnp.zeros(z_sc.shape, F32)

    rw = rw_ref[...]
    row = lax.broadcasted_iota(jnp.int32, rw.shape, 0)
    prev = jnp.where(row == 0, prev_sc[...], pltpu.roll(rw, 1, 0))
    prev_sc[...] = rw[c - 1:c, :]
    xr = rw + (prev - rw) * mu_ref[...]
    gmat = g_ref[...]
    r, k2, v, logw, av, bv = _rwkv_prep(xr, lora_ref[...], w0_ref[...], a0_ref[...],
                                        kk_ref[...], ka_ref[...], gmat)

    ti = lax.broadcasted_iota(jnp.int32, (c, c), 0)
    si = lax.broadcasted_iota(jnp.int32, (c, c), 1)
    cum = _dot2_exact_lhs((si <= ti).astype(F32), logw)
    clast = cum[c - 1:c, :]
    e_neg = jnp.exp(-cum)
    e_rest = jnp.exp(clast - cum)
    at = av * jnp.exp(cum - logw)
    bt = bv * e_neg
    kt = k2 * e_neg
    rt = r * jnp.exp(cum)
    bg = bv * e_rest
    kg = k2 * e_rest
    glast = jnp.exp(clast)

    er = lax.broadcasted_iota(jnp.int32, (2 * c, LANES), 0) // c
    el = lax.broadcasted_iota(jnp.int32, (2 * c, LANES), 1) // RWKV_HEAD
    emask = er == el
    br = lax.broadcasted_iota(jnp.int32, (2 * c, 2 * c), 0)
    bc = lax.broadcasted_iota(jnp.int32, (2 * c, 2 * c), 1)
    same = (br // c) == (bc // c)
    strict = jnp.logical_and(same, (br % c) > (bc % c))
    incl = jnp.logical_and(same, (br % c) >= (bc % c))
    eye2c = (br == bc).astype(F32)
    lane_lo = lax.broadcasted_iota(jnp.int32, (c, LANES), 1) < RWKV_HEAD
    dr = lax.broadcasted_iota(jnp.int32, (LANES, LANES), 0)
    dc = lax.broadcasted_iota(jnp.int32, (LANES, LANES), 1)

    def expand(x):
        return jnp.where(emask, jnp.concatenate([x, x], axis=0), 0.0)

    def dup(x):
        return jnp.concatenate([x, x], axis=0)

    pairs = range(n_pairs)
    sls = [slice(LANES * p, LANES * (p + 1)) for p in pairs]
    aps = []
    for sl in sls:
        btp = bt[:, sl]
        ktp = kt[:, sl]
        lhs = jnp.concatenate([at[:, sl], rt[:, sl]], axis=0)
        rhs = jnp.concatenate([jnp.where(lane_lo, btp, 0.0), jnp.where(lane_lo, 0.0, btp),
                               jnp.where(lane_lo, ktp, 0.0), jnp.where(lane_lo, 0.0, ktp)], axis=0)
        aps.append(_dot_nt(lhs, rhs))
    nmats = [jnp.where(strict, dup(ap[0:c, 0:2 * c]), 0.0) for ap in aps]
    akms = [jnp.where(strict, dup(ap[0:c, 2 * c:4 * c]), 0.0) for ap in aps]
    rbms = [jnp.where(incl, dup(ap[c:2 * c, 0:2 * c]), 0.0) for ap in aps]
    rkms = [jnp.where(incl, dup(ap[c:2 * c, 2 * c:4 * c]), 0.0) for ap in aps]
    tmats = [eye2c + nm for nm in nmats]
    pws = nmats
    for _ in range(int(math.log2(c)) - 1):
        pws = [_dot(pw, pw) for pw in pws]
        tmats = [tm + _dot(tm, pw) for tm, pw in zip(tmats, pws)]
    ves = [expand(v[:, sl]) for sl in sls]
    akvs = [_dot(akm, ve) for akm, ve in zip(akms, ves)]
    wus = [_dot(tm, jnp.concatenate([expand(at[:, sl]), akv], axis=1))
           for tm, sl, akv in zip(tmats, sls, akvs)]
    bkts = [jnp.concatenate([expand(bg[:, sl]), expand(kg[:, sl])], axis=0).T
            for sl in sls]
    gcols = [jnp.sum(jnp.where(dr == dc, glast[:, sl], 0.0), axis=1, keepdims=True) for sl in sls]
    zs = [z_sc[p] for p in pairs]
    ues = [_dot(wu[:, :LANES], z) + wu[:, LANES:] for wu, z in zip(wus, zs)]
    for p in pairs:
        z_new = gcols[p] * zs[p] + _dot3(bkts[p], jnp.concatenate([ues[p], ves[p]], axis=0))
        z_sc[p] = z_new
        zout_ref[0, p] = z_new
    ys = []
    for p in pairs:
        ye = _dot(jnp.concatenate([expand(rt[:, sls[p]]), rbms[p], rkms[p]], axis=1),
                    jnp.concatenate([zs[p], ues[p], ves[p]], axis=0))
        ys.append(ye[0:c, :] + ye[c:2 * c, :])
    y = jnp.concatenate(ys, axis=1)
    out = _rwkv_post(y, r, k2, v, rk_ref[...], lnw_ref[...], lnb_ref[...], gmat)
    out_ref[...] = out.astype(out_ref.dtype)


def _rwkv_params(W):
    w = RWKV_HEADS * RWKV_HEAD
    lora = jnp.zeros((DECAY_LORA + AAA_LORA, 2 * w), F32)
    lora = lora.at[:DECAY_LORA, :w].set(W['w_decay_up']).at[DECAY_LORA:, w:].set(W['w_aaa_up'])
    head = np.arange(w) // RWKV_HEAD
    gmat = jnp.asarray((head[:, None] == head[None, :]).astype(np.float32))
    row = lambda a: a.reshape(1, -1)
    return dict(mu=row(W['mu_shift']), lora=lora, w0=row(W['w0_decay']), a0=row(W['a0']),
                kk=row(W['k_k']), ka=row(W['k_a']), rk=row(W['r_k']), lnw=row(W['ln_x_w']),
                lnb=row(W['ln_x_b']), gmat=gmat)


def rwkv_prompt(rw, batch, W):
    n, cols = rw.shape
    t = n // batch
    c = RWKV_CHUNK
    nc = t // c
    w = RWKV_HEADS * RWKV_HEAD
    pr = _rwkv_params(W)
    args = (rw, pr['mu'], pr['lora'], pr['w0'], pr['a0'], pr['kk'], pr['ka'], pr['rk'], pr['lnw'],
            pr['lnb'], pr['gmat'])
    in_specs = [pl.BlockSpec((c, cols), lambda b, i: (b * nc + i, 0))] + [
        pl.BlockSpec(a.shape, lambda b, i: (0, 0)) for a in args[1:]]
    n_pairs = RWKV_HEADS // 2
    out_shape = (jax.ShapeDtypeStruct((n, w), BF16),
                 jax.ShapeDtypeStruct((batch, n_pairs, LANES, LANES), F32))
    return pl.pallas_call(
        functools.partial(_rwkv_prompt_kernel, chunk=c), grid=(batch, nc), in_specs=in_specs,
        out_specs=(pl.BlockSpec((c, w), lambda b, i: (b * nc + i, 0)),
                   pl.BlockSpec((1, n_pairs, LANES, LANES), lambda b, i: (b, 0, 0, 0))),
        out_shape=out_shape,
        scratch_shapes=[pltpu.VMEM((1, cols), F32), pltpu.VMEM((n_pairs, LANES, LANES), F32)],
        compiler_params=_cparams(("parallel", "arbitrary")), name="rwkv_prompt")(*args)


def _rwkv_sample_prep_kernel(rw_ref, sh_ref, mu_ref, lora_ref, w0_ref, a0_ref, kk_ref, ka_ref, g_ref,
                             r_out, k_out, v_out, w_out, a_out, b_out):
    rw = rw_ref[...]
    xr = rw + (sh_ref[...] - rw) * mu_ref[...]
    r, k2, v, logw, av, bv = _rwkv_prep(xr, lora_ref[...], w0_ref[...], a0_ref[...],
                                        kk_ref[...], ka_ref[...], g_ref[...])
    r_out[...] = r
    k_out[...] = k2
    v_out[...] = v
    w_out[...] = jnp.exp(logw)
    a_out[...] = av
    b_out[...] = bv


def _rwkv_sample_state_kernel(s_ref, r_ref, k_ref, v_ref, w_ref, a_ref, b_ref, s_out, y_out, *, tb):
    n = RWKV_HEAD
    eye = lax.broadcasted_iota(jnp.int32, (n, n), 0) == lax.broadcasted_iota(jnp.int32, (n, n), 1)
    for bl in range(tb):
        for h in range(RWKV_HEADS):
            s = s_ref[bl, h]
            row = lambda ref: ref[bl, h:h + 1, :]
            sa = jnp.sum(s * row(a_ref), axis=1, keepdims=True)
            vcol = jnp.sum(jnp.where(eye, row(v_ref), 0.0), axis=1, keepdims=True)
            s_new = s * row(w_ref) + sa * row(b_ref) + vcol * row(k_ref)
            s_out[bl, h] = s_new
            ycol = jnp.sum(s_new * row(r_ref), axis=1, keepdims=True)
            y_out[bl, h:h + 1, :] = jnp.sum(jnp.where(eye, ycol, 0.0), axis=0, keepdims=True)


def _rwkv_sample_post_kernel(y_ref, r_ref, k_ref, v_ref, rk_ref, lnw_ref, lnb_ref, g_ref, o_ref):
    o_ref[...] = _rwkv_post(y_ref[...], r_ref[...], k_ref[...], v_ref[...], rk_ref[...],
                            lnw_ref[...], lnb_ref[...], g_ref[...]).astype(o_ref.dtype)


def rwkv_sample(rw, shift0, state, W, tb=8):
    db, cols = rw.shape
    w = RWKV_HEADS * RWKV_HEAD
    pr = _rwkv_params(W)
    args = (rw, shift0, pr['mu'], pr['lora'], pr['w0'], pr['a0'], pr['kk'], pr['ka'], pr['gmat'])
    vec = jax.ShapeDtypeStruct((db, w), F32)
    r, k2, v, wd, av, bv = pl.pallas_call(
        _rwkv_sample_prep_kernel, grid=(1,), in_specs=[_full(a.shape) for a in args],
        out_specs=tuple(_full((db, w)) for _ in range(6)), out_shape=(vec,) * 6,
        compiler_params=_cparams(("arbitrary",)), name="rwkv_sample_prep")(*args)
    heads = lambda a: a.reshape(db, RWKV_HEADS, RWKV_HEAD)
    st_spec = pl.BlockSpec((tb, RWKV_HEADS, RWKV_HEAD, RWKV_HEAD), lambda i: (i, 0, 0, 0))
    hv_spec = pl.BlockSpec((tb, RWKV_HEADS, RWKV_HEAD), lambda i: (i, 0, 0))
    s_new, y = pl.pallas_call(
        functools.partial(_rwkv_sample_state_kernel, tb=tb), grid=(db // tb,),
        in_specs=[st_spec] + [hv_spec] * 6, out_specs=(st_spec, hv_spec),
        out_shape=(jax.ShapeDtypeStruct(state.shape, F32),
                   jax.ShapeDtypeStruct((db, RWKV_HEADS, RWKV_HEAD), F32)),
        compiler_params=_cparams(("parallel",)), name="rwkv_sample_state")(
            state, heads(r), heads(k2), heads(v), heads(wd), heads(av), heads(bv))
    pargs = (y.reshape(db, w), r, k2, v, pr['rk'], pr['lnw'], pr['lnb'], pr['gmat'])
    ob = pl.pallas_call(
        _rwkv_sample_post_kernel, grid=(1,), in_specs=[_full(a.shape) for a in pargs],
        out_specs=_full((db, w)), out_shape=jax.ShapeDtypeStruct((db, w), BF16),
        compiler_params=_cparams(("arbitrary",)), name="rwkv_sample_post")(*pargs)
    return ob, s_new


def _merge_kernel(oa_ref, ob_ref, oc_ref, g_ref, x_ref, woa_ref, wob_ref, woc_ref, wout_ref,
                  gffn_ref, wr_ref, br_ref, h_out, hn_out, ti_out, tg_out, *, d):
    o_a = jnp.dot(oa_ref[...], woa_ref[...], preferred_element_type=F32)
    o_b = jnp.dot(ob_ref[...], wob_ref[...], preferred_element_type=F32)
    o_c = jnp.dot(oc_ref[...], woc_ref[...], preferred_element_type=F32)
    g = g_ref[...].astype(F32)
    merged = g[:, 0:d] * o_a + g[:, d:2 * d] * o_b + g[:, 2 * d:3 * d] * o_c
    h = x_ref[...] + _dot(merged, wout_ref[...])
    h_out[...] = h
    hn = _rms(h, gffn_ref[...])
    hn_out[...] = hn
    logits = _dot3(hn, wr_ref[...]) + br_ref[...]
    n_e = logits.shape[1]
    lane = lax.broadcasted_iota(jnp.int32, logits.shape, 1)
    cur = logits
    vals, idxs = [], []
    for _ in range(TOP_K):
        mx = jnp.max(cur, axis=-1, keepdims=True)
        ix = jnp.min(jnp.where(cur == mx, lane, n_e), axis=-1, keepdims=True)
        vals.append(mx)
        idxs.append(ix)
        cur = jnp.where(lane == ix, -jnp.inf, cur)
    es = [jnp.exp(vk - vals[0]) for vk in vals]
    den = es[0]
    for e in es[1:]:
        den = den + e
    lane_k = lax.broadcasted_iota(jnp.int32, ti_out.shape, 1)
    ti = jnp.zeros(ti_out.shape, jnp.int32)
    tg = jnp.zeros(tg_out.shape, F32)
    for kk in range(TOP_K):
        ti = jnp.where(lane_k == kk, idxs[kk], ti)
        tg = jnp.where(lane_k == kk, es[kk] / den, tg)
    ti_out[...] = ti
    tg_out[...] = tg


def merge_out(oa, w_oa, ob, oc, gates, x, W, tm):
    n, d = x.shape
    n_e = W['w_router'].shape[1]
    args = (oa, ob, oc, gates, x, w_oa.astype(BF16), W['w_o_rwkv'].astype(BF16), W['w_o_mem'].astype(BF16),
            W['w_out'].astype(BF16), W['g_ffn_norm'].reshape(1, d), W['w_router'], W['b_router'].reshape(1, n_e))
    row = lambda i: (i, 0)
    in_specs = [pl.BlockSpec((tm, a.shape[1]), row) for a in args[:5]] + [_full(a.shape) for a in args[5:]]
    out_shape = (jax.ShapeDtypeStruct((n, d), F32), jax.ShapeDtypeStruct((n, d), F32),
                 jax.ShapeDtypeStruct((n, TOP_K), jnp.int32), jax.ShapeDtypeStruct((n, TOP_K), F32))
    out_specs = (pl.BlockSpec((tm, d), row), pl.BlockSpec((tm, d), row),
                 pl.BlockSpec((tm, TOP_K), row), pl.BlockSpec((tm, TOP_K), row))
    return pl.pallas_call(
        functools.partial(_merge_kernel, d=d), grid=(n // tm,), in_specs=in_specs, out_specs=out_specs,
        out_shape=out_shape, compiler_params=_cparams(("parallel",)), name="merge_out")(*args)


def _gmm_kernel(be_ref, nu_ref, tok_ref, src_ref, w1g_ref, w1l_ref, b1g_ref, b1l_ref, w2_ref, b2_ref,
                y_ref, xbuf, sem, *, tm):
    i = pl.program_id(0)
    n_used = nu_ref[0]

    def row_copy(blk, r, slot):
        return pltpu.make_async_copy(src_ref.at[pl.ds(tok_ref[blk * tm + r], 1)],
                                     xbuf.at[slot, pl.ds(r, 1)], sem.at[slot])

    def drain(slot):
        def step(r, carry):
            pltpu.make_async_copy(src_ref.at[pl.ds(0, 1)], xbuf.at[slot, pl.ds(0, 1)], sem.at[slot]).wait()
            return carry
        lax.fori_loop(0, tm, step, 0, unroll=8)

    @pl.when(jnp.logical_and(i == 0, n_used > 0))
    def _():
        def issue(r, carry):
            row_copy(0, r, 0).start()
            return carry
        lax.fori_loop(0, tm, issue, 0, unroll=8)

    @pl.when(i < n_used)
    def _():
        slot = i % 2
        drain(slot)
        x = xbuf[slot].astype(BF16)
        nxt = jnp.minimum(i + 1, n_used - 1)
        for r in range(tm):
            row_copy(nxt, r, 1 - slot).start()
        hg = jnp.dot(x, w1g_ref[0], preferred_element_type=F32) + b1g_ref[0]
        hl = jnp.dot(x, w1l_ref[0], preferred_element_type=F32) + b1l_ref[0]
        x_glu = jnp.minimum(hg, SWIGLU_LIMIT)
        x_lin = jnp.clip(hl, -SWIGLU_LIMIT, SWIGLU_LIMIT)
        act = x_glu * _sigmoid(SWIGLU_ALPHA * x_glu) * (x_lin + 1.0)
        y_ref[...] = _dot(act, w2_ref[0]) + b2_ref[0]

    @pl.when(i == n_used - 1)
    def _():
        drain((i + 1) % 2)

    @pl.when(i >= n_used)
    def _():
        y_ref[...] = jnp.zeros(y_ref.shape, y_ref.dtype)


def grouped_mlp(src, row_tok, block_e, n_used, W, tm):
    n_rows = row_tok.shape[0]
    d = src.shape[1]
    n_e = W['w_mlp1'].shape[0]
    d_ff = W['w_mlp2'].shape[1]
    w1 = W['w_mlp1']
    w1g = w1[:, :, 0::2].astype(BF16)
    w1l = w1[:, :, 1::2].astype(BF16)
    b1 = W['b_mlp1']
    b1g = b1[:, 0::2].reshape(n_e, 1, d_ff)
    b1l = b1[:, 1::2].reshape(n_e, 1, d_ff)
    w2 = W['w_mlp2'].astype(BF16)
    b2 = W['b_mlp2'].reshape(n_e, 1, d)
    ex = lambda i, be, nu, tok: (be[i], 0, 0)
    grid_spec = pltpu.PrefetchScalarGridSpec(
        num_scalar_prefetch=3, grid=(n_rows // tm,),
        in_specs=[pl.BlockSpec(memory_space=pl.ANY),
                  pl.BlockSpec((1, d, d_ff), ex), pl.BlockSpec((1, d, d_ff), ex),
                  pl.BlockSpec((1, 1, d_ff), ex), pl.BlockSpec((1, 1, d_ff), ex),
                  pl.BlockSpec((1, d_ff, d), ex), pl.BlockSpec((1, 1, d), ex)],
        out_specs=pl.BlockSpec((tm, d), lambda i, be, nu, tok: (i, 0)),
        scratch_shapes=[pltpu.VMEM((2, tm, d), F32), pltpu.SemaphoreType.DMA((2,))])
    return pl.pallas_call(
        functools.partial(_gmm_kernel, tm=tm), grid_spec=grid_spec,
        out_shape=jax.ShapeDtypeStruct((n_rows, d), F32),
        compiler_params=_cparams(("arbitrary",)), name="moe_gmm")(
            block_e, n_used, row_tok, src, w1g, w1l, b1g, b1l, w2, b2)


def _combine_kernel(dest_ref, gate_ref, h_ref, gf_ref, yb_ref, o_ref, buf, sem, *, tc):
    def issue(t, carry):
        for k in range(TOP_K):
            pltpu.make_async_copy(yb_ref.at[pl.ds(dest_ref[TOP_K * t + k], 1)],
                                  buf.at[k, pl.ds(t, 1)], sem).start()
        return carry

    lax.fori_loop(0, tc, issue, 0)

    def drain(t, carry):
        for k in range(TOP_K):
            pltpu.make_async_copy(yb_ref.at[pl.ds(0, 1)], buf.at[k, pl.ds(0, 1)], sem).wait()
        return carry

    lax.fori_loop(0, tc, drain, 0)
    gate = gate_ref[...]
    y = h_ref[...]
    for k in range(TOP_K):
        y = y + gate[:, k:k + 1] * buf[k]
    o_ref[...] = _rms(y, gf_ref[...])


def moe_combine(yb, dest, gate, h, g_final, tc):
    n, d = h.shape
    return pl.pallas_call(
        functools.partial(_combine_kernel, tc=tc), grid=(n // tc,),
        in_specs=[pl.BlockSpec((tc * TOP_K,), lambda i: (i,), memory_space=pltpu.SMEM),
                  pl.BlockSpec((tc, TOP_K), lambda i: (i, 0)),
                  pl.BlockSpec((tc, d), lambda i: (i, 0)),
                  pl.BlockSpec((1, d), lambda i: (0, 0)),
                  pl.BlockSpec(memory_space=pl.ANY)],
        out_specs=pl.BlockSpec((tc, d), lambda i: (i, 0)),
        out_shape=jax.ShapeDtypeStruct((n, d), F32),
        scratch_shapes=[pltpu.VMEM((TOP_K, tc, d), F32), pltpu.SemaphoreType.DMA(())],
        compiler_params=_cparams(("arbitrary",)), name="moe_combine")(dest, gate, h, g_final.reshape(1, d), yb)


def _routing(top_i, n_e, tm):
    n_tok = top_i.shape[0]
    n_assign = n_tok * TOP_K
    flat_e = top_i.reshape(-1)
    onehot = (flat_e[:, None] == jnp.arange(n_e, dtype=jnp.int32)[None, :]).astype(jnp.int32)
    csum = jnp.cumsum(onehot, axis=0)
    counts = csum[-1]
    padded = ((counts + tm - 1) // tm) * tm
    pend = jnp.cumsum(padded)
    pstart = pend - padded
    dest = jnp.sum(onehot * (csum - onehot + pstart[None, :]), axis=1).astype(jnp.int32)
    n_blocks = -(-n_assign // tm) + n_e
    n_rows = n_blocks * tm
    row_tok = jnp.zeros((n_rows,), jnp.int32).at[dest].set(jnp.arange(n_assign, dtype=jnp.int32) // TOP_K)
    block_e = jnp.minimum(
        jnp.searchsorted(pend, jnp.arange(n_blocks, dtype=jnp.int32) * tm, side='right'), n_e - 1).astype(jnp.int32)
    n_used = (pend[-1:] // tm).astype(jnp.int32)
    return dest, row_tok, block_e, n_used


def kernel(x_prompt, x_sample, mem_prompt, cache_kv_latent, cache_k_rope, page_table, state_rwkv, state_rwkv_shift, cache_mem_k, cache_mem_v, g_attn_norm, w_in, b_gate, g_q_norm, w_uq, g_kv_norm, w_uk, w_uv, w_o_mla, mu_shift, w0_decay, w_decay_up, a0, w_aaa_up, k_k, k_a, r_k, ln_x_w, ln_x_b, w_o_rwkv, g_mem_norm, w_mem_k, w_mem_v, w_o_mem, w_out, g_ffn_norm, w_router, b_router, w_mlp1, b_mlp1, w_mlp2, b_mlp2, g_final):
    layer_params = dict(g_attn_norm=g_attn_norm, w_in=w_in, b_gate=b_gate, g_q_norm=g_q_norm, w_uq=w_uq,
                        g_kv_norm=g_kv_norm, w_uk=w_uk, w_uv=w_uv, w_o_mla=w_o_mla, mu_shift=mu_shift,
                        w0_decay=w0_decay, w_decay_up=w_decay_up, a0=a0, w_aaa_up=w_aaa_up, k_k=k_k, k_a=k_a,
                        r_k=r_k, ln_x_w=ln_x_w, ln_x_b=ln_x_b, w_o_rwkv=w_o_rwkv, g_mem_norm=g_mem_norm,
                        w_mem_k=w_mem_k, w_mem_v=w_mem_v, w_o_mem=w_o_mem, w_out=w_out, g_ffn_norm=g_ffn_norm,
                        w_router=w_router, b_router=b_router, w_mlp1=w_mlp1, b_mlp1=b_mlp1,
                        w_mlp2=w_mlp2, b_mlp2=b_mlp2)
    depth = w_in.shape[0]
    assert depth == 1, "single-layer trunk"
    W = {name: arr.reshape(arr.shape[1:]) for name, arr in layer_params.items()}
    B, T, D = x_prompt.shape
    DB, TS, _ = x_sample.shape
    assert TS == 1, "sample group decodes one token per sequence"
    n_pages = page_table.shape[1]
    page = cache_kv_latent.shape[2]
    past_len = n_pages * page
    kv_lora = w_uk.shape[1]
    mem_tokens = mem_prompt.shape[1]
    mem_w = MEM_HEADS * MEM_HEAD
    n_e = w_router.shape[2]
    rw_w = RWKV_HEADS * RWKV_HEAD

    tm = min(256, T)
    xp = x_prompt.reshape(B * T, D)
    xs = x_sample.reshape(DB, D)

    tab_p = _rope_tables(jnp.arange(T, dtype=jnp.int32))
    ckv_p, kpe_p, q_h, k_h, v_h = mla_proj_prompt(xp, tab_p, W, tm)
    rw_p, qm_p, gates_p = in_proj_rest(xp, W, tm)
    mk_p, mv_p = mem_kv(mem_prompt.reshape(B * mem_tokens, D), W)
    tq = min(FLASH_TQ, T)
    oa_p = flash_prompt(q_h, k_h, v_h, B, tq)
    oc_p = mem_attn_prompt(qm_p, mk_p.reshape(B, mem_tokens, mem_w), mv_p.reshape(B, mem_tokens, mem_w), tm)
    ob_p, z_p = rwkv_prompt(rw_p, B, W)
    woa_p = jnp.pad(W['w_o_mla'].reshape(MLA_HEADS, V_HEAD, D),
                    ((0, 0), (0, LANES - V_HEAD), (0, 0))).reshape(MLA_HEADS * LANES, D)
    h_p, hn_p, ti_p, tg_p = merge_out(oa_p, woa_p, ob_p, oc_p, gates_p, xp, W, tm)

    tab_s = _rope_tables(jnp.full((DB,), past_len, jnp.int32))
    ckv_s, kpe_s, qlat_s, qpe_s = mla_proj_sample(xs, tab_s, W)
    rw_s, qm_s, gates_s = in_proj_rest(xs, W, DB)
    oa_s = decode_attn(jnp.transpose(qlat_s, (1, 0, 2)), jnp.transpose(qpe_s, (1, 0, 2)),
                       ckv_s.reshape(DB, 1, kv_lora), kpe_s.reshape(DB, 1, QK_ROPE),
                       cache_kv_latent, cache_k_rope, page_table, W['w_uv']).reshape(DB, MLA_HEADS * V_HEAD)
    oc_s = mem_attn_sample(qm_s, cache_mem_k.reshape(DB, mem_tokens, mem_w),
                           cache_mem_v.reshape(DB, mem_tokens, mem_w))
    ob_s, st_s = rwkv_sample(rw_s, state_rwkv_shift.reshape(DB, -1),
                             state_rwkv.reshape(DB, RWKV_HEADS, RWKV_HEAD, RWKV_HEAD), W)
    h_s, hn_s, ti_s, tg_s = merge_out(oa_s.astype(BF16), W['w_o_mla'], ob_s, oc_s, gates_s, xs, W, DB)

    hn_all = jnp.concatenate([hn_p, hn_s], axis=0)
    top_i = jnp.concatenate([ti_p, ti_s], axis=0)
    dest, row_tok, block_e, n_used = _routing(top_i, n_e, MOE_TM)
    yb = grouped_mlp(hn_all, row_tok, block_e, n_used, W, MOE_TM)
    n_p = B * T
    y_p = moe_combine(yb, dest[:n_p * TOP_K], tg_p, h_p, g_final, min(128, T))
    y_s = moe_combine(yb, dest[n_p * TOP_K:], tg_s, h_s, g_final, DB)

    zq = z_p.reshape(B, RWKV_HEADS // 2, 2, RWKV_HEAD, 2, RWKV_HEAD)
    st_p = jnp.stack([zq[:, :, 0, :, 0, :], zq[:, :, 1, :, 1, :]], axis=2)
    st_p = jnp.swapaxes(st_p.reshape(B, RWKV_HEADS, RWKV_HEAD, RWKV_HEAD), -1, -2)
    lead = lambda a: a[None]
    return (y_p.reshape(B, T, D), y_s.reshape(DB, TS, D),
            lead(ckv_p.reshape(B, T, kv_lora)), lead(kpe_p.reshape(B, T, QK_ROPE)),
            lead(st_p), lead(rw_p.reshape(B, T, -1)[:, -1]),
            lead(mk_p.reshape(B, mem_tokens, MEM_HEADS, MEM_HEAD)),
            lead(mv_p.reshape(B, mem_tokens, MEM_HEADS, MEM_HEAD)),
            lead(ckv_s.reshape(DB, TS, kv_lora)), lead(kpe_s.reshape(DB, TS, QK_ROPE)),
            lead(st_s), lead(rw_s))
```

```python
import functools
import math

import numpy as np
import jax
import jax.numpy as jnp
from jax import lax
from jax.experimental import pallas as pl
from jax.experimental.pallas import tpu as pltpu

F32 = jnp.float32
BF16 = jnp.bfloat16

MLA_HEADS = 8
QK_NOPE = 64
QK_ROPE = 32
V_HEAD = 64
ROPE_THETA = 10000.0
RWKV_HEADS = 8
RWKV_HEAD = 64
DECAY_LORA = 64
AAA_LORA = 64
MEM_HEADS = 4
MEM_HEAD = 128
TOP_K = 4
SWIGLU_ALPHA = 1.702
SWIGLU_LIMIT = 7.0
RMS_EPS = 1e-6
GN_EPS = 64e-5
L2_EPS = 1e-12

LANES = 128
VMEM_LIMIT = 48 * 1024 * 1024

RWKV_CHUNK = 64
MOE_TM = 256
FLASH_TQ = 1024
FLASH_SUBTILES = 2


def _cparams(sem, vmem=VMEM_LIMIT):
    return pltpu.CompilerParams(dimension_semantics=sem, vmem_limit_bytes=vmem)


def _rms(x, g, eps=RMS_EPS):
    return x * lax.rsqrt(jnp.mean(x * x, axis=-1, keepdims=True) + eps) * g


def _dot(a, b):
    return jnp.dot(a.astype(BF16), b.astype(BF16), preferred_element_type=F32)


def _dot_nt(a, b):
    return lax.dot_general(a.astype(BF16), b.astype(BF16), (((1,), (1,)), ((), ())),
                           preferred_element_type=F32)


def _split(a):
    hi = a.astype(BF16)
    lo = (a - hi.astype(F32)).astype(BF16)
    return hi, lo


def _dot3(a, b):
    ah, al = _split(a)
    bh, bl = _split(b)
    return (jnp.dot(ah, bh, preferred_element_type=F32)
            + (jnp.dot(ah, bl, preferred_element_type=F32)
               + jnp.dot(al, bh, preferred_element_type=F32)))


def _dot2_exact_rhs(a, b):
    ah, al = _split(a)
    bb = b.astype(BF16)
    return jnp.dot(ah, bb, preferred_element_type=F32) + jnp.dot(al, bb, preferred_element_type=F32)


def _dot2_exact_lhs(a, b):
    aa = a.astype(BF16)
    bh, bl = _split(b)
    return jnp.dot(aa, bh, preferred_element_type=F32) + jnp.dot(aa, bl, preferred_element_type=F32)


def _sigmoid(x):
    return 1.0 / (1.0 + jnp.exp(-x))


def _mla_proj_prompt_kernel(x_ref, ga_ref, wcq_ref, gq_ref, wckv_ref, gkv_ref,
                            wkp_ref, wkpr_ref, wkpp_ref, wkprp_ref,
                            wq_ref, wqr_ref, wuk_ref, wuv_ref,
                            cos32_ref, sin32_ref, cost_ref, sint_ref,
                            ckv_out, kpe_out, q_out, k_out, v_out, *, scale):
    xn = _rms(x_ref[...], ga_ref[...]).astype(BF16)
    cqn = _rms(jnp.dot(xn, wcq_ref[...], preferred_element_type=F32), gq_ref[...]).astype(BF16)
    ckvn = _rms(jnp.dot(xn, wckv_ref[...], preferred_element_type=F32), gkv_ref[...])
    ckv_out[...] = ckvn
    kpe_out[...] = (jnp.dot(xn, wkp_ref[...], preferred_element_type=F32) * cos32_ref[...]
                    + jnp.dot(xn, wkpr_ref[...], preferred_element_type=F32) * sin32_ref[...])
    cost = cost_ref[...]
    sint = sint_ref[...]
    kpp = (jnp.dot(xn, wkpp_ref[...], preferred_element_type=F32) * cost
           + jnp.dot(xn, wkprp_ref[...], preferred_element_type=F32) * sint)
    qa = jnp.dot(cqn, wq_ref[...], preferred_element_type=F32)
    qb = jnp.dot(cqn, wqr_ref[...], preferred_element_type=F32)
    ckb = ckvn.astype(BF16)
    ka = jnp.dot(ckb, wuk_ref[...], preferred_element_type=F32)
    va = jnp.dot(ckb, wuv_ref[...], preferred_element_type=F32)
    ones_lane = lax.broadcasted_iota(jnp.int32, cost.shape, 1) == V_HEAD
    for h in range(MLA_HEADS):
        sl = slice(LANES * h, LANES * (h + 1))
        q_out[h] = ((qa[:, sl] * cost + qb[:, sl] * sint) * scale).astype(BF16)
        k_out[h] = (ka[:, sl] + kpp).astype(BF16)
        v_out[h] = jnp.where(ones_lane, 1.0, va[:, sl]).astype(BF16)


def _mla_proj_sample_kernel(x_ref, ga_ref, wcq_ref, gq_ref, wckv_ref, gkv_ref,
                            wkp_ref, wkpr_ref, wqn_ref, wukt_ref, wqp_ref, wqpr_ref,
                            cos32_ref, sin32_ref,
                            ckv_out, kpe_out, qlat_out, qpe_out, *, scale):
    xn = _rms(x_ref[...], ga_ref[...]).astype(BF16)
    cqn = _rms(jnp.dot(xn, wcq_ref[...], preferred_element_type=F32), gq_ref[...]).astype(BF16)
    ckv_out[...] = _rms(jnp.dot(xn, wckv_ref[...], preferred_element_type=F32), gkv_ref[...])
    cos32 = cos32_ref[...]
    sin32 = sin32_ref[...]
    kpe_out[...] = (jnp.dot(xn, wkp_ref[...], preferred_element_type=F32) * cos32
                    + jnp.dot(xn, wkpr_ref[...], preferred_element_type=F32) * sin32)
    qn = jnp.dot(cqn, wqn_ref[...], preferred_element_type=F32)
    for h in range(MLA_HEADS):
        sl = slice(LANES * h, LANES * (h + 1))
        qlat_out[h] = (_dot(qn[:, sl], wukt_ref[h]) * scale).astype(BF16)
        qpe_out[h] = ((jnp.dot(cqn, wqp_ref[h], preferred_element_type=F32) * cos32
                       + jnp.dot(cqn, wqpr_ref[h], preferred_element_type=F32) * sin32) * scale)


def _full(shape):
    nd = len(shape)
    return pl.BlockSpec(shape, lambda *_: (0,) * nd)


def _rope_tables(pos):
    half = QK_ROPE // 2
    inv = jnp.exp(-math.log(ROPE_THETA) * jnp.arange(half, dtype=F32) / half)
    ang = pos.astype(F32)[:, None] * inv[None, :]
    cos = jnp.cos(ang)
    sin = jnp.sin(ang)
    return jnp.concatenate([cos, cos], axis=1), jnp.concatenate([sin, sin], axis=1)


def _rot_cols(w):
    half = QK_ROPE // 2
    return jnp.concatenate([-w[..., half:], w[..., :half]], axis=-1)


def _pad_cols(w, lo, total=LANES):
    pad = [(0, 0)] * (w.ndim - 1) + [(lo, total - lo - w.shape[-1])]
    return jnp.pad(w, pad)


def _mla_weights(W):
    d = W['w_in'].shape[0]
    q_lora = W['w_uq'].shape[0]
    kv_lora = W['w_uk'].shape[0]
    c0, c1, c2 = q_lora, q_lora + kv_lora, q_lora + kv_lora + QK_ROPE
    w_in = W['w_in']
    wcq = w_in[:, :c0].astype(BF16)
    wckv = w_in[:, c0:c1].astype(BF16)
    wkp = w_in[:, c1:c2]
    wuq = W['w_uq'].reshape(q_lora, MLA_HEADS, QK_NOPE + QK_ROPE)
    return dict(d=d, q_lora=q_lora, kv_lora=kv_lora, wcq=wcq, wckv=wckv, wkp=wkp, wuq=wuq,
                ga=W['g_attn_norm'].reshape(1, d), gq=W['g_q_norm'].reshape(1, q_lora),
                gkv=W['g_kv_norm'].reshape(1, kv_lora))


def mla_proj_prompt(x, pos_tables, W, tm):
    n, d = x.shape
    m = _mla_weights(W)
    q_lora, kv_lora = m['q_lora'], m['kv_lora']
    cos32, sin32 = pos_tables
    t = cos32.shape[0]
    ones = jnp.ones((t, QK_NOPE), F32)
    zeros_hi = jnp.zeros((t, LANES - QK_NOPE - QK_ROPE), F32)
    cost = jnp.concatenate([ones, cos32, zeros_hi], axis=1)
    sint = jnp.concatenate([jnp.zeros((t, QK_NOPE), F32), sin32, zeros_hi], axis=1)
    wkp = m['wkp']
    wkpr = _rot_cols(wkp)
    wuq = m['wuq']
    wq = _pad_cols(wuq, 0).reshape(q_lora, MLA_HEADS * LANES).astype(BF16)
    wqr = _pad_cols(_rot_cols(wuq[..., QK_NOPE:]), QK_NOPE).reshape(q_lora, MLA_HEADS * LANES).astype(BF16)
    wuk = _pad_cols(W['w_uk'], 0).reshape(kv_lora, MLA_HEADS * LANES).astype(BF16)
    wuv = _pad_cols(W['w_uv'], 0).reshape(kv_lora, MLA_HEADS * LANES).astype(BF16)
    args = (x, m['ga'], m['wcq'], m['gq'], m['wckv'], m['gkv'],
            wkp.astype(BF16), wkpr.astype(BF16),
            _pad_cols(wkp, QK_NOPE).astype(BF16), _pad_cols(wkpr, QK_NOPE).astype(BF16),
            wq, wqr, wuk, wuv, cos32, sin32, cost, sint)
    nt = t // tm
    row = lambda i: (i, 0)
    tab = lambda i: (i % nt, 0)
    in_specs = [pl.BlockSpec((tm, d), row)] + [_full(a.shape) for a in args[1:14]] + [
        pl.BlockSpec((tm, QK_ROPE), tab), pl.BlockSpec((tm, QK_ROPE), tab),
        pl.BlockSpec((tm, LANES), tab), pl.BlockSpec((tm, LANES), tab)]
    head = pl.BlockSpec((MLA_HEADS, tm, LANES), lambda i: (0, i, 0))
    out_shape = (jax.ShapeDtypeStruct((n, kv_lora), F32), jax.ShapeDtypeStruct((n, QK_ROPE), F32),
                 jax.ShapeDtypeStruct((MLA_HEADS, n, LANES), BF16),
                 jax.ShapeDtypeStruct((MLA_HEADS, n, LANES), BF16),
                 jax.ShapeDtypeStruct((MLA_HEADS, n, LANES), BF16))
    out_specs = (pl.BlockSpec((tm, kv_lora), row), pl.BlockSpec((tm, QK_ROPE), row), head, head, head)
    scale = (QK_NOPE + QK_ROPE) ** -0.5 * math.log2(math.e)
    return pl.pallas_call(
        functools.partial(_mla_proj_prompt_kernel, scale=scale),
        grid=(n // tm,), in_specs=in_specs, out_specs=out_specs, out_shape=out_shape,
        compiler_params=_cparams(("parallel",)), name="mla_proj_prompt")(*args)


def mla_proj_sample(x, pos_tables, W):
    n, d = x.shape
    m = _mla_weights(W)
    q_lora, kv_lora = m['q_lora'], m['kv_lora']
    cos32, sin32 = pos_tables
    wkp = m['wkp']
    wuq = m['wuq']
    wqn = _pad_cols(wuq[..., :QK_NOPE], 0).reshape(q_lora, MLA_HEADS * LANES).astype(BF16)
    wukt = jnp.pad(jnp.transpose(W['w_uk'], (1, 2, 0)), ((0, 0), (0, LANES - QK_NOPE), (0, 0))).astype(BF16)
    wqp = jnp.transpose(wuq[..., QK_NOPE:], (1, 0, 2))
    args = (x, m['ga'], m['wcq'], m['gq'], m['wckv'], m['gkv'],
            wkp.astype(BF16), _rot_cols(wkp).astype(BF16), wqn, wukt,
            wqp.astype(BF16), _rot_cols(wqp).astype(BF16), cos32, sin32)
    out_shape = (jax.ShapeDtypeStruct((n, kv_lora), F32), jax.ShapeDtypeStruct((n, QK_ROPE), F32),
                 jax.ShapeDtypeStruct((MLA_HEADS, n, kv_lora), BF16),
                 jax.ShapeDtypeStruct((MLA_HEADS, n, QK_ROPE), F32))
    scale = (QK_NOPE + QK_ROPE) ** -0.5
    return pl.pallas_call(
        functools.partial(_mla_proj_sample_kernel, scale=scale),
        grid=(1,), in_specs=[_full(a.shape) for a in args],
        out_specs=tuple(_full(s.shape) for s in out_shape), out_shape=out_shape,
        compiler_params=_cparams(("arbitrary",)), name="mla_proj_sample")(*args)


def _in_proj_rest_kernel(x_ref, ga_ref, wrw_ref, wqm_ref, wg_ref, bg_ref,
                         rw_out, qm_out, g_out, *, mem_scale):
    xn = _rms(x_ref[...], ga_ref[...]).astype(BF16)
    rw_out[...] = jnp.dot(xn, wrw_ref[...], preferred_element_type=F32)
    qm_out[...] = (jnp.dot(xn, wqm_ref[...], preferred_element_type=F32) * mem_scale).astype(BF16)
    g_out[...] = _sigmoid(jnp.dot(xn, wg_ref[...], preferred_element_type=F32) + bg_ref[...]).astype(BF16)


def in_proj_rest(x, W, tm):
    n, d = x.shape
    q_lora = W['w_uq'].shape[0]
    kv_lora = W['w_uk'].shape[0]
    rw_cols = W['mu_shift'].shape[0]
    mem_w = MEM_HEADS * MEM_HEAD
    c2 = q_lora + kv_lora + QK_ROPE
    c3 = c2 + rw_cols
    c4 = c3 + mem_w
    w_in = W['w_in']
    n_gate = w_in.shape[1] - c4
    args = (x, W['g_attn_norm'].reshape(1, d), w_in[:, c2:c3].astype(BF16), w_in[:, c3:c4].astype(BF16),
            w_in[:, c4:].astype(BF16), W['b_gate'].reshape(1, n_gate))
    row = lambda i: (i, 0)
    out_shape = (jax.ShapeDtypeStruct((n, rw_cols), F32), jax.ShapeDtypeStruct((n, mem_w), BF16),
                 jax.ShapeDtypeStruct((n, n_gate), BF16))
    return pl.pallas_call(
        functools.partial(_in_proj_rest_kernel, mem_scale=MEM_HEAD ** -0.5),
        grid=(n // tm,),
        in_specs=[pl.BlockSpec((tm, d), row)] + [_full(a.shape) for a in args[1:]],
        out_specs=(pl.BlockSpec((tm, rw_cols), row), pl.BlockSpec((tm, mem_w), row),
                   pl.BlockSpec((tm, n_gate), row)),
        out_shape=out_shape, compiler_params=_cparams(("parallel",)), name="in_proj_rest")(*args)


def _mem_kv_kernel(m_ref, g_ref, wk_ref, wv_ref, k_out, v_out):
    mn = _rms(m_ref[...], g_ref[...]).astype(BF16)
    k_out[...] = jnp.dot(mn, wk_ref[...], preferred_element_type=F32)
    v_out[...] = jnp.dot(mn, wv_ref[...], preferred_element_type=F32)


def mem_kv(mem, W):
    n, d = mem.shape
    mem_w = MEM_HEADS * MEM_HEAD
    args = (mem, W['g_mem_norm'].reshape(1, d), W['w_mem_k'].astype(BF16), W['w_mem_v'].astype(BF16))
    out_shape = (jax.ShapeDtypeStruct((n, mem_w), F32), jax.ShapeDtypeStruct((n, mem_w), F32))
    return pl.pallas_call(
        _mem_kv_kernel, grid=(1,), in_specs=[_full(a.shape) for a in args],
        out_specs=(_full((n, mem_w)), _full((n, mem_w))), out_shape=out_shape,
        compiler_params=_cparams(("arbitrary",)), name="mem_kv")(*args)


def _flash_kernel(q_ref, k_ref, v_ref, o_ref, *scratch, tq, n_sub):
    qi = pl.program_id(2)
    sub = tq // n_sub
    m_scs = scratch[:n_sub]
    acc_scs = scratch[n_sub:2 * n_sub]
    s_scs = scratch[2 * n_sub:]
    for u in range(n_sub):
        m_scs[u][...] = jnp.full(m_scs[u].shape, -jnp.inf, F32)
        acc_scs[u][...] = jnp.zeros(acc_scs[u].shape, F32)

    def scores(j, slot):
        kb = k_ref[pl.ds(pl.multiple_of(j * tq, tq), tq), :]
        for u in range(n_sub):
            s_scs[slot][u] = lax.dot_general(q_ref[u * sub:(u + 1) * sub, :], kb, (((1,), (1,)), ((), ())),
                                            preferred_element_type=F32)

    def consume(j, slot, masked):
        vb = v_ref[pl.ds(pl.multiple_of(j * tq, tq), tq), :]
        for u in range(n_sub):
            s = s_scs[slot][u]
            if masked:
                rows = u * sub + lax.broadcasted_iota(jnp.int32, (sub, tq), 0)
                cols = lax.broadcasted_iota(jnp.int32, (sub, tq), 1)
                s = jnp.where(cols <= rows, s, -jnp.inf)
            m_prev = m_scs[u][...]
            m_new = jnp.maximum(m_prev, jnp.max(s, axis=-1, keepdims=True))
            alpha = jnp.exp2(m_prev - m_new)
            p = jnp.exp2(s - jnp.tile(m_new, (1, tq // LANES)))
            acc_scs[u][...] = alpha * acc_scs[u][...] + jnp.dot(p.astype(BF16), vb, preferred_element_type=F32)
            m_scs[u][...] = m_new

    scores(0, 0)

    def body(i, carry):
        scores(2 * i + 1, 1)
        consume(2 * i, 0, False)
        scores(2 * i + 2, 0)
        consume(2 * i + 1, 1, False)
        return carry

    lax.fori_loop(0, qi // 2, body, 0)

    @pl.when(qi % 2 == 1)
    def _():
        scores(qi, 1)
        consume(qi - 1, 0, False)
        consume(qi, 1, True)

    @pl.when(qi % 2 == 0)
    def _():
        consume(qi, 0, True)

    for u in range(n_sub):
        acc = acc_scs[u][...]
        o_ref[u * sub:(u + 1) * sub, :] = (acc / acc[:, V_HEAD:V_HEAD + 1]).astype(o_ref.dtype)


def flash_prompt(q, k, v, batch, tq):
    h, n, dh = q.shape
    t = n // batch
    nq = t // tq
    kv_spec = pl.BlockSpec((None, t, dh), lambda b, hh, i: (hh, b, 0))
    n_sub = FLASH_SUBTILES
    sub = tq // n_sub
    return pl.pallas_call(
        functools.partial(_flash_kernel, tq=tq, n_sub=n_sub), grid=(batch, h, nq),
        in_specs=[pl.BlockSpec((None, tq, dh), lambda b, hh, i: (hh, b * nq + i, 0)), kv_spec, kv_spec],
        out_specs=pl.BlockSpec((tq, dh), lambda b, hh, i: (b * nq + i, hh)),
        out_shape=jax.ShapeDtypeStruct((n, h * dh), BF16),
        scratch_shapes=([pltpu.VMEM((sub, LANES), F32)] * n_sub + [pltpu.VMEM((sub, dh), F32)] * n_sub
                        + [pltpu.VMEM((n_sub, sub, tq), F32)] * 2),
        compiler_params=_cparams(("parallel", "parallel", "arbitrary")), name="flash_prompt")(q, k, v)


def _decode_kernel(pt_ref, qlat_ref, qpe_ref, ckv_ref, kpe_ref, wuv_ref, lat_hbm, rope_hbm, o_ref,
                   lat_buf, rope_buf, sem, *, n_pages, page):
    b = pl.program_id(0)

    def page_copies(seq, slot, i):
        pg = pt_ref[seq, i]
        rows = pl.ds(i * page, page)
        return (pltpu.make_async_copy(lat_hbm.at[0, pg], lat_buf.at[slot, rows], sem.at[slot]),
                pltpu.make_async_copy(rope_hbm.at[0, pg], rope_buf.at[slot, :, rows], sem.at[slot]))

    def start_all(seq, slot):
        for i in range(n_pages):
            for c in page_copies(seq, slot, i):
                c.start()

    @pl.when(b == 0)
    def _():
        start_all(0, 0)

    @pl.when(b + 1 < pl.num_programs(0))
    def _():
        start_all(b + 1, (b + 1) % 2)

    slot = b % 2

    def drain(i, carry):
        for c in page_copies(b, slot, 0):
            c.wait()
        return carry

    lax.fori_loop(0, n_pages, drain, 0)

    qlat = qlat_ref[0]
    qpe = qpe_ref[0]
    kall = lat_buf[slot].astype(BF16)
    s = _dot_nt(qlat, kall) + _dot(qpe, rope_buf[slot])
    ckv = ckv_ref[0]
    kpe = kpe_ref[0]
    s_new = (jnp.sum(qlat.astype(F32) * ckv, axis=-1, keepdims=True)
             + jnp.sum(qpe * kpe, axis=-1, keepdims=True))
    m = jnp.maximum(jnp.max(s, axis=-1, keepdims=True), s_new)
    p = jnp.exp(s - m)
    p_new = jnp.exp(s_new - m)
    l = jnp.sum(p, axis=-1, keepdims=True) + p_new
    o_lat = (jnp.dot(p.astype(BF16), kall, preferred_element_type=F32) + p_new * ckv) / l
    ov = _dot(o_lat, wuv_ref[...])
    hh = lax.broadcasted_iota(jnp.int32, ov.shape, 0)
    cc = lax.broadcasted_iota(jnp.int32, ov.shape, 1) // V_HEAD
    o_ref[0] = jnp.sum(jnp.where(hh == cc, ov, 0.0), axis=0, keepdims=True)


def decode_attn(qlat, qpe, ckv, kpe, cache_lat, cache_rope, page_table, w_uv):
    db, h, kv = qlat.shape
    _, n_pool, page, _ = cache_lat.shape
    n_pages = page_table.shape[1]
    past = n_pages * page
    wuv = w_uv.reshape(kv, h * V_HEAD).astype(BF16)
    per_b = lambda shape: pl.BlockSpec((1,) + shape, lambda b, pt: (b, 0, 0))
    in_specs = [per_b((h, kv)), per_b((h, QK_ROPE)), per_b((1, kv)), per_b((1, QK_ROPE)),
                pl.BlockSpec(wuv.shape, lambda b, pt: (0, 0)),
                pl.BlockSpec(memory_space=pl.ANY), pl.BlockSpec(memory_space=pl.ANY)]
    grid_spec = pltpu.PrefetchScalarGridSpec(
        num_scalar_prefetch=1, grid=(db,), in_specs=in_specs,
        out_specs=per_b((1, h * V_HEAD)),
        scratch_shapes=[pltpu.VMEM((2, past, kv), F32), pltpu.VMEM((2, QK_ROPE, past), F32),
                        pltpu.SemaphoreType.DMA((2,))])
    return pl.pallas_call(
        functools.partial(_decode_kernel, n_pages=n_pages, page=page), grid_spec=grid_spec,
        out_shape=jax.ShapeDtypeStruct((db, 1, h * V_HEAD), F32),
        compiler_params=_cparams(("arbitrary",)), name="decode_attn")(
            page_table, qlat, qpe, ckv, kpe, wuv, cache_lat, cache_rope)


def _mem_attn_prompt_kernel(q_ref, k_ref, v_ref, o_ref):
    q = q_ref[...]
    k = k_ref[0].astype(BF16)
    v = v_ref[0].astype(BF16)
    for h in range(MEM_HEADS):
        sl = slice(MEM_HEAD * h, MEM_HEAD * (h + 1))
        s = _dot_nt(q[:, sl], k[:, sl])
        p = jnp.exp(s - jnp.max(s, axis=-1, keepdims=True))
        o = jnp.dot(p.astype(BF16), v[:, sl], preferred_element_type=F32) / jnp.sum(p, axis=-1, keepdims=True)
        o_ref[:, sl] = o.astype(o_ref.dtype)


def mem_attn_prompt(qm, mk, mv, tm):
    n, w = qm.shape
    b, m, _ = mk.shape
    nt = n // b // tm
    kv_spec = pl.BlockSpec((1, m, w), lambda i: (i // nt, 0, 0))
    return pl.pallas_call(
        _mem_attn_prompt_kernel, grid=(n // tm,),
        in_specs=[pl.BlockSpec((tm, w), lambda i: (i, 0)), kv_spec, kv_spec],
        out_specs=pl.BlockSpec((tm, w), lambda i: (i, 0)),
        out_shape=jax.ShapeDtypeStruct((n, w), BF16),
        compiler_params=_cparams(("parallel",)), name="mem_attn_prompt")(qm, mk, mv)


def _mem_attn_sample_kernel(q_ref, k_ref, v_ref, o_ref, *, tb):
    for bl in range(tb):
        q = q_ref[bl].astype(F32)
        s = jnp.sum(k_ref[bl] * q, axis=-1, keepdims=True)
        p = jnp.exp(s - jnp.max(s, axis=0, keepdims=True))
        o = jnp.sum(p * v_ref[bl], axis=0) / jnp.sum(p, axis=0)
        o_ref[bl] = o.astype(o_ref.dtype)


def mem_attn_sample(qm, mk, mv, tb=8):
    db, nh, hd = qm.shape
    m = mk.shape[1]
    kv_spec = pl.BlockSpec((tb, m, nh, hd), lambda i: (i, 0, 0, 0))
    q_spec = pl.BlockSpec((tb, nh, hd), lambda i: (i, 0, 0))
    return pl.pallas_call(
        functools.partial(_mem_attn_sample_kernel, tb=tb), grid=(db // tb,),
        in_specs=[q_spec, kv_spec, kv_spec], out_specs=q_spec,
        out_shape=jax.ShapeDtypeStruct((db, nh, hd), BF16),
        compiler_params=_cparams(("parallel",)), name="mem_attn_sample")(qm, mk, mv)


def _rwkv_prep(xr, lora_w, w0, a0, kk_w, ka_w, gmat):
    w = RWKV_HEADS * RWKV_HEAD
    r = xr[:, 0:w]
    k = xr[:, w:2 * w]
    v = xr[:, 2 * w:3 * w]
    tail = xr[:, 3 * w:3 * w + DECAY_LORA + AAA_LORA]
    lane = lax.broadcasted_iota(jnp.int32, tail.shape, 1)
    z = jnp.where(lane < DECAY_LORA, jnp.tanh(tail), tail)
    lo = _dot3(z, lora_w)
    dec = lo[:, :w] + w0
    aaa = lo[:, w:] + a0
    u = -dec
    softplus = jnp.maximum(u, 0.0) + jnp.log(1.0 + jnp.exp(-jnp.abs(u)))
    logw = -jnp.exp(-softplus - 0.5)
    a_sig = _sigmoid(aaa)
    kk = k * kk_w
    norm = jnp.sqrt(_dot2_exact_rhs(kk * kk, gmat))
    kkn = kk / jnp.maximum(norm, L2_EPS)
    k2 = k * (1.0 + (a_sig - 1.0) * ka_w)
    return r, k2, v, logw, -kkn, kkn * a_sig


def _rwkv_post(y, r, k2, v, rk, lnw, lnb, gmat):
    inv_n = 1.0 / RWKV_HEAD
    mu = _dot2_exact_rhs(y, gmat) * inv_n
    d = y - mu
    var = _dot2_exact_rhs(d * d, gmat) * inv_n
    yn = d * lax.rsqrt(var + GN_EPS) * lnw + lnb
    return yn + _dot2_exact_rhs(r * k2 * rk, gmat) * v


def _rwkv_prompt_kernel(rw_ref, mu_ref, lora_ref, w0_ref, a0_ref, kk_ref, ka_ref, rk_ref,
                        lnw_ref, lnb_ref, g_ref, out_ref, zout_ref, prev_sc, z_sc, *, chunk):
    c = chunk
    n_pairs = RWKV_HEADS // 2

    @pl.when(pl.program_id(1) == 0)
    def _():
        prev_sc[...] = jnp.zeros(prev_sc.shape, F32)
        z_sc[...] = jnp.zeros(z_sc.shape, F32)

    rw = rw_ref[...]
    row = lax.broadcasted_iota(jnp.int32, rw.shape, 0)
    prev = jnp.where(row == 0, prev_sc[...], pltpu.roll(rw, 1, 0))
    prev_sc[...] = rw[c - 1:c, :]
    xr = rw + (prev - rw) * mu_ref[...]
    gmat = g_ref[...]
    r, k2, v, logw, av, bv = _rwkv_prep(xr, lora_ref[...], w0_ref[...], a0_ref[...],
                                        kk_ref[...], ka_ref[...], gmat)

    ti = lax.broadcasted_iota(jnp.int32, (c, c), 0)
    si = lax.broadcasted_iota(jnp.int32, (c, c), 1)
    cum = _dot2_exact_lhs((si <= ti).astype(F32), logw)
    clast = cum[c - 1:c, :]
    e_neg = jnp.exp(-cum)
    e_rest = jnp.exp(clast - cum)
    at = av * jnp.exp(cum - logw)
    bt = bv * e_neg
    kt = k2 * e_neg
    rt = r * jnp.exp(cum)
    bg = bv * e_rest
    kg = k2 * e_rest
    glast = jnp.exp(clast)

    er = lax.broadcasted_iota(jnp.int32, (2 * c, LANES), 0) // c
    el = lax.broadcasted_iota(jnp.int32, (2 * c, LANES), 1) // RWKV_HEAD
    emask = er == el
    br = lax.broadcasted_iota(jnp.int32, (2 * c, 2 * c), 0)
    bc = lax.broadcasted_iota(jnp.int32, (2 * c, 2 * c), 1)
    same = (br // c) == (bc // c)
    strict = jnp.logical_and(same, (br % c) > (bc % c))
    incl = jnp.logical_and(same, (br % c) >= (bc % c))
    eye2c = (br == bc).astype(F32)
    lane_lo = lax.broadcasted_iota(jnp.int32, (c, LANES), 1) < RWKV_HEAD
    dr = lax.broadcasted_iota(jnp.int32, (LANES, LANES), 0)
    dc = lax.broadcasted_iota(jnp.int32, (LANES, LANES), 1)

    def expand(x):
        return jnp.where(emask, jnp.concatenate([x, x], axis=0), 0.0)

    def dup(x):
        return jnp.concatenate([x, x], axis=0)

    pairs = range(n_pairs)
    sls = [slice(LANES * p, LANES * (p + 1)) for p in pairs]
    aps = []
    for sl in sls:
        btp = bt[:, sl]
        ktp = kt[:, sl]
        lhs = jnp.concatenate([at[:, sl], rt[:, sl]], axis=0)
        rhs = jnp.concatenate([jnp.where(lane_lo, btp, 0.0), jnp.where(lane_lo, 0.0, btp),
                               jnp.where(lane_lo, ktp, 0.0), jnp.where(lane_lo, 0.0, ktp)], axis=0)
        aps.append(_dot_nt(lhs, rhs))
    nmats = [jnp.where(strict, dup(ap[0:c, 0:2 * c]), 0.0) for ap in aps]
    akms = [jnp.where(strict, dup(ap[0:c, 2 * c:4 * c]), 0.0) for ap in aps]
    rbms = [jnp.where(incl, dup(ap[c:2 * c, 0:2 * c]), 0.0) for ap in aps]
    rkms = [jnp.where(incl, dup(ap[c:2 * c, 2 * c:4 * c]), 0.0) for ap in aps]
    tmats = [eye2c + nm for nm in nmats]
    pws = nmats
    for _ in range(int(math.log2(c)) - 1):
        pws = [_dot(pw, pw) for pw in pws]
        tmats = [tm + _dot(tm, pw) for tm, pw in zip(tmats, pws)]
    ves = [expand(v[:, sl]) for sl in sls]
    akvs = [_dot(akm, ve) for akm, ve in zip(akms, ves)]
    wus = [_dot(tm, jnp.concatenate([expand(at[:, sl]), akv], axis=1))
           for tm, sl, akv in zip(tmats, sls, akvs)]
    bkts = [jnp.concatenate([expand(bg[:, sl]), expand(kg[:, sl])], axis=0).T
            for sl in sls]
    gcols = [jnp.sum(jnp.where(dr == dc, glast[:, sl], 0.0), axis=1, keepdims=True) for sl in sls]
    zs = [z_sc[p] for p in pairs]
    ues = [_dot(wu[:, :LANES], z) + wu[:, LANES:] for wu, z in zip(wus, zs)]
    for p in pairs:
        z_new = gcols[p] * zs[p] + _dot3(bkts[p], jnp.concatenate([ues[p], ves[p]], axis=0))
        z_sc[p] = z_new
        zout_ref[0, p] = z_new
    ys = []
    for p in pairs:
        ye = _dot(jnp.concatenate([expand(rt[:, sls[p]]), rbms[p], rkms[p]], axis=1),
                    jnp.concatenate([zs[p], ues[p], ves[p]], axis=0))
        ys.append(ye[0:c, :] + ye[c:2 * c, :])
    y = jnp.concatenate(ys, axis=1)
    out = _rwkv_post(y, r, k2, v, rk_ref[...], lnw_ref[...], lnb_ref[...], gmat)
    out_ref[...] = out.astype(out_ref.dtype)


def _rwkv_params(W):
    w = RWKV_HEADS * RWKV_HEAD
    lora = jnp.zeros((DECAY_LORA + AAA_LORA, 2 * w), F32)
    lora = lora.at[:DECAY_LORA, :w].set(W['w_decay_up']).at[DECAY_LORA:, w:].set(W['w_aaa_up'])
    head = np.arange(w) // RWKV_HEAD
    gmat = jnp.asarray((head[:, None] == head[None, :]).astype(np.float32))
    row = lambda a: a.reshape(1, -1)
    return dict(mu=row(W['mu_shift']), lora=lora, w0=row(W['w0_decay']), a0=row(W['a0']),
                kk=row(W['k_k']), ka=row(W['k_a']), rk=row(W['r_k']), lnw=row(W['ln_x_w']),
                lnb=row(W['ln_x_b']), gmat=gmat)


def rwkv_prompt(rw, batch, W):
    n, cols = rw.shape
    t = n // batch
    c = RWKV_CHUNK
    nc = t // c
    w = RWKV_HEADS * RWKV_HEAD
    pr = _rwkv_params(W)
    args = (rw, pr['mu'], pr['lora'], pr['w0'], pr['a0'], pr['kk'], pr['ka'], pr['rk'], pr['lnw'],
            pr['lnb'], pr['gmat'])
    in_specs = [pl.BlockSpec((c, cols), lambda b, i: (b * nc + i, 0))] + [
        pl.BlockSpec(a.shape, lambda b, i: (0, 0)) for a in args[1:]]
    n_pairs = RWKV_HEADS // 2
    out_shape = (jax.ShapeDtypeStruct((n, w), BF16),
                 jax.ShapeDtypeStruct((batch, n_pairs, LANES, LANES), F32))
    return pl.pallas_call(
        functools.partial(_rwkv_prompt_kernel, chunk=c), grid=(batch, nc), in_specs=in_specs,
        out_specs=(pl.BlockSpec((c, w), lambda b, i: (b * nc + i, 0)),
                   pl.BlockSpec((1, n_pairs, LANES, LANES), lambda b, i: (b, 0, 0, 0))),
        out_shape=out_shape,
        scratch_shapes=[pltpu.VMEM((1, cols), F32), pltpu.VMEM((n_pairs, LANES, LANES), F32)],
        compiler_params=_cparams(("parallel", "arbitrary")), name="rwkv_prompt")(*args)


def _rwkv_sample_prep_kernel(rw_ref, sh_ref, mu_ref, lora_ref, w0_ref, a0_ref, kk_ref, ka_ref, g_ref,
                             r_out, k_out, v_out, w_out, a_out, b_out):
    rw = rw_ref[...]
    xr = rw + (sh_ref[...] - rw) * mu_ref[...]
    r, k2, v, logw, av, bv = _rwkv_prep(xr, lora_ref[...], w0_ref[...], a0_ref[...],
                                        kk_ref[...], ka_ref[...], g_ref[...])
    r_out[...] = r
    k_out[...] = k2
    v_out[...] = v
    w_out[...] = jnp.exp(logw)
    a_out[...] = av
    b_out[...] = bv


def _rwkv_sample_state_kernel(s_ref, r_ref, k_ref, v_ref, w_ref, a_ref, b_ref, s_out, y_out, *, tb):
    n = RWKV_HEAD
    eye = lax.broadcasted_iota(jnp.int32, (n, n), 0) == lax.broadcasted_iota(jnp.int32, (n, n), 1)
    for bl in range(tb):
        for h in range(RWKV_HEADS):
            s = s_ref[bl, h]
            row = lambda ref: ref[bl, h:h + 1, :]
            sa = jnp.sum(s * row(a_ref), axis=1, keepdims=True)
            vcol = jnp.sum(jnp.where(eye, row(v_ref), 0.0), axis=1, keepdims=True)
            s_new = s * row(w_ref) + sa * row(b_ref) + vcol * row(k_ref)
            s_out[bl, h] = s_new
            ycol = jnp.sum(s_new * row(r_ref), axis=1, keepdims=True)
            y_out[bl, h:h + 1, :] = jnp.sum(jnp.where(eye, ycol, 0.0), axis=0, keepdims=True)


def _rwkv_sample_post_kernel(y_ref, r_ref, k_ref, v_ref, rk_ref, lnw_ref, lnb_ref, g_ref, o_ref):
    o_ref[...] = _rwkv_post(y_ref[...], r_ref[...], k_ref[...], v_ref[...], rk_ref[...],
                            lnw_ref[...], lnb_ref[...], g_ref[...]).astype(o_ref.dtype)


def rwkv_sample(rw, shift0, state, W, tb=8):
    db, cols = rw.shape
    w = RWKV_HEADS * RWKV_HEAD
    pr = _rwkv_params(W)
    args = (rw, shift0, pr['mu'], pr['lora'], pr['w0'], pr['a0'], pr['kk'], pr['ka'], pr['gmat'])
    vec = jax.ShapeDtypeStruct((db, w), F32)
    r, k2, v, wd, av, bv = pl.pallas_call(
        _rwkv_sample_prep_kernel, grid=(1,), in_specs=[_full(a.shape) for a in args],
        out_specs=tuple(_full((db, w)) for _ in range(6)), out_shape=(vec,) * 6,
        compiler_params=_cparams(("arbitrary",)), name="rwkv_sample_prep")(*args)
    heads = lambda a: a.reshape(db, RWKV_HEADS, RWKV_HEAD)
    st_spec = pl.BlockSpec((tb, RWKV_HEADS, RWKV_HEAD, RWKV_HEAD), lambda i: (i, 0, 0, 0))
    hv_spec = pl.BlockSpec((tb, RWKV_HEADS, RWKV_HEAD), lambda i: (i, 0, 0))
    s_new, y = pl.pallas_call(
        functools.partial(_rwkv_sample_state_kernel, tb=tb), grid=(db // tb,),
        in_specs=[st_spec] + [hv_spec] * 6, out_specs=(st_spec, hv_spec),
        out_shape=(jax.ShapeDtypeStruct(state.shape, F32),
                   jax.ShapeDtypeStruct((db, RWKV_HEADS, RWKV_HEAD), F32)),
        compiler_params=_cparams(("parallel",)), name="rwkv_sample_state")(
            state, heads(r), heads(k2), heads(v), heads(wd), heads(av), heads(bv))
    pargs = (y.reshape(db, w), r, k2, v, pr['rk'], pr['lnw'], pr['lnb'], pr['gmat'])
    ob = pl.pallas_call(
        _rwkv_sample_post_kernel, grid=(1,), in_specs=[_full(a.shape) for a in pargs],
        out_specs=_full((db, w)), out_shape=jax.ShapeDtypeStruct((db, w), BF16),
        compiler_params=_cparams(("arbitrary",)), name="rwkv_sample_post")(*pargs)
    return ob, s_new


def _merge_kernel(oa_ref, ob_ref, oc_ref, g_ref, x_ref, woa_ref, wob_ref, woc_ref, wout_ref,
                  gffn_ref, wr_ref, br_ref, h_out, hn_out, ti_out, tg_out, *, d):
    o_a = jnp.dot(oa_ref[...], woa_ref[...], preferred_element_type=F32)
    o_b = jnp.dot(ob_ref[...], wob_ref[...], preferred_element_type=F32)
    o_c = jnp.dot(oc_ref[...], woc_ref[...], preferred_element_type=F32)
    g = g_ref[...].astype(F32)
    merged = g[:, 0:d] * o_a + g[:, d:2 * d] * o_b + g[:, 2 * d:3 * d] * o_c
    h = x_ref[...] + _dot(merged, wout_ref[...])
    h_out[...] = h
    hn = _rms(h, gffn_ref[...])
    hn_out[...] = hn
    logits = _dot3(hn, wr_ref[...]) + br_ref[...]
    n_e = logits.shape[1]
    lane = lax.broadcasted_iota(jnp.int32, logits.shape, 1)
    cur = logits
    vals, idxs = [], []
    for _ in range(TOP_K):
        mx = jnp.max(cur, axis=-1, keepdims=True)
        ix = jnp.min(jnp.where(cur == mx, lane, n_e), axis=-1, keepdims=True)
        vals.append(mx)
        idxs.append(ix)
        cur = jnp.where(lane == ix, -jnp.inf, cur)
    es = [jnp.exp(vk - vals[0]) for vk in vals]
    den = es[0]
    for e in es[1:]:
        den = den + e
    lane_k = lax.broadcasted_iota(jnp.int32, ti_out.shape, 1)
    ti = jnp.zeros(ti_out.shape, jnp.int32)
    tg = jnp.zeros(tg_out.shape, F32)
    for kk in range(TOP_K):
        ti = jnp.where(lane_k == kk, idxs[kk], ti)
        tg = jnp.where(lane_k == kk, es[kk] / den, tg)
    ti_out[...] = ti
    tg_out[...] = tg


def merge_out(oa, w_oa, ob, oc, gates, x, W, tm):
    n, d = x.shape
    n_e = W['w_router'].shape[1]
    args = (oa, ob, oc, gates, x, w_oa.astype(BF16), W['w_o_rwkv'].astype(BF16), W['w_o_mem'].astype(BF16),
            W['w_out'].astype(BF16), W['g_ffn_norm'].reshape(1, d), W['w_router'], W['b_router'].reshape(1, n_e))
    row = lambda i: (i, 0)
    in_specs = [pl.BlockSpec((tm, a.shape[1]), row) for a in args[:5]] + [_full(a.shape) for a in args[5:]]
    out_shape = (jax.ShapeDtypeStruct((n, d), F32), jax.ShapeDtypeStruct((n, d), F32),
                 jax.ShapeDtypeStruct((n, TOP_K), jnp.int32), jax.ShapeDtypeStruct((n, TOP_K), F32))
    out_specs = (pl.BlockSpec((tm, d), row), pl.BlockSpec((tm, d), row),
                 pl.BlockSpec((tm, TOP_K), row), pl.BlockSpec((tm, TOP_K), row))
    return pl.pallas_call(
        functools.partial(_merge_kernel, d=d), grid=(n // tm,), in_specs=in_specs, out_specs=out_specs,
        out_shape=out_shape, compiler_params=_cparams(("parallel",)), name="merge_out")(*args)


def _pair_split_kernel(w_ref, p_ref, o_ref):
    p = p_ref[...]
    for t in range(w_ref.shape[2] // (2 * LANES)):
        sl = slice(2 * LANES * t, 2 * LANES * (t + 1))
        o_ref[0, :, sl] = jnp.dot(w_ref[0, :, sl].astype(BF16), p, preferred_element_type=F32).astype(BF16)


def pair_split_cast(w, tr=512):
    n_e, d, cols = w.shape
    j = np.arange(2 * LANES)
    src = np.where(j < LANES, 2 * j, 2 * (j - LANES) + 1)
    perm = jnp.asarray((np.arange(2 * LANES)[:, None] == src[None, :]).astype(np.float32), dtype=BF16)
    blk = pl.BlockSpec((1, tr, cols), lambda e, i: (e, i, 0))
    return pl.pallas_call(
        _pair_split_kernel, grid=(n_e, d // tr),
        in_specs=[blk, pl.BlockSpec(perm.shape, lambda e, i: (0, 0))], out_specs=blk,
        out_shape=jax.ShapeDtypeStruct(w.shape, BF16),
        compiler_params=_cparams(("parallel", "parallel")), name="moe_w1_prep")(w, perm)


def _gmm_kernel(be_ref, nu_ref, tok_ref, src_ref, w1_ref, b1_ref, w2_ref, b2_ref,
                y_ref, xbuf, sem, *, tm):
    i = pl.program_id(0)
    n_used = nu_ref[0]

    def row_copy(blk, r, slot):
        return pltpu.make_async_copy(src_ref.at[pl.ds(tok_ref[blk * tm + r], 1)],
                                     xbuf.at[slot, pl.ds(r, 1)], sem.at[slot])

    def drain(slot):
        def step(r, carry):
            pltpu.make_async_copy(src_ref.at[pl.ds(0, 1)], xbuf.at[slot, pl.ds(0, 1)], sem.at[slot]).wait()
            return carry
        lax.fori_loop(0, tm, step, 0, unroll=8)

    @pl.when(jnp.logical_and(i == 0, n_used > 0))
    def _():
        def issue(r, carry):
            row_copy(0, r, 0).start()
            return carry
        lax.fori_loop(0, tm, issue, 0, unroll=8)

    @pl.when(i < n_used)
    def _():
        slot = i % 2
        drain(slot)
        x = xbuf[slot].astype(BF16)
        nxt = jnp.minimum(i + 1, n_used - 1)
        for r in range(tm):
            row_copy(nxt, r, 1 - slot).start()
        h = jnp.dot(x, w1_ref[0], preferred_element_type=F32) + b1_ref[0]
        acts = []
        for t in range(w2_ref.shape[1] // LANES):
            x_glu = jnp.minimum(h[:, 2 * LANES * t:2 * LANES * t + LANES], SWIGLU_LIMIT)
            x_lin = jnp.clip(h[:, 2 * LANES * t + LANES:2 * LANES * (t + 1)], -SWIGLU_LIMIT, SWIGLU_LIMIT)
            acts.append((x_glu * _sigmoid(SWIGLU_ALPHA * x_glu) * (x_lin + 1.0)).astype(BF16))
        y_ref[...] = jnp.dot(jnp.concatenate(acts, axis=1), w2_ref[0], preferred_element_type=F32) + b2_ref[0]

    @pl.when(i == n_used - 1)
    def _():
        drain((i + 1) % 2)

    @pl.when(i >= n_used)
    def _():
        y_ref[...] = jnp.zeros(y_ref.shape, y_ref.dtype)


def grouped_mlp(src, row_tok, block_e, n_used, W, tm):
    n_rows = row_tok.shape[0]
    d = src.shape[1]
    n_e = W['w_mlp1'].shape[0]
    d_ff = W['w_mlp2'].shape[1]
    w1 = pair_split_cast(W['w_mlp1'])
    b1 = W['b_mlp1'].reshape(n_e, d_ff // LANES, LANES, 2)
    b1 = jnp.swapaxes(b1, 2, 3).reshape(n_e, 1, 2 * d_ff)
    w2 = W['w_mlp2'].astype(BF16)
    b2 = W['b_mlp2'].reshape(n_e, 1, d)
    ex = lambda i, be, nu, tok: (be[i], 0, 0)
    grid_spec = pltpu.PrefetchScalarGridSpec(
        num_scalar_prefetch=3, grid=(n_rows // tm,),
        in_specs=[pl.BlockSpec(memory_space=pl.ANY),
                  pl.BlockSpec((1, d, 2 * d_ff), ex), pl.BlockSpec((1, 1, 2 * d_ff), ex),
                  pl.BlockSpec((1, d_ff, d), ex), pl.BlockSpec((1, 1, d), ex)],
        out_specs=pl.BlockSpec((tm, d), lambda i, be, nu, tok: (i, 0)),
        scratch_shapes=[pltpu.VMEM((2, tm, d), F32), pltpu.SemaphoreType.DMA((2,))])
    return pl.pallas_call(
        functools.partial(_gmm_kernel, tm=tm), grid_spec=grid_spec,
        out_shape=jax.ShapeDtypeStruct((n_rows, d), F32),
        compiler_params=_cparams(("arbitrary",)), name="moe_gmm")(
            block_e, n_used, row_tok, src, w1, b1, w2, b2)


def _combine_kernel(dest_ref, gate_ref, h_ref, gf_ref, yb_ref, o_ref, buf, sem, *, tc):
    i = pl.program_id(0)

    def start(step, slot):
        def issue(t, carry):
            for k in range(TOP_K):
                pltpu.make_async_copy(yb_ref.at[pl.ds(dest_ref[(step * tc + t) * TOP_K + k], 1)],
                                      buf.at[slot, k, pl.ds(t, 1)], sem.at[slot]).start()
            return carry
        lax.fori_loop(0, tc, issue, 0, unroll=4)

    @pl.when(i == 0)
    def _():
        start(0, 0)

    @pl.when(i + 1 < pl.num_programs(0))
    def _():
        start(i + 1, (i + 1) % 2)

    slot = i % 2

    def drain(t, carry):
        for k in range(TOP_K):
            pltpu.make_async_copy(yb_ref.at[pl.ds(0, 1)], buf.at[slot, k, pl.ds(0, 1)], sem.at[slot]).wait()
        return carry

    lax.fori_loop(0, tc, drain, 0, unroll=4)
    gate = gate_ref[...]
    y = h_ref[...]
    for k in range(TOP_K):
        y = y + gate[:, k:k + 1] * buf[slot, k]
    o_ref[...] = _rms(y, gf_ref[...])


def moe_combine(yb, dest, gate, h, g_final, tc):
    n, d = h.shape
    grid_spec = pltpu.PrefetchScalarGridSpec(
        num_scalar_prefetch=1, grid=(n // tc,),
        in_specs=[pl.BlockSpec((tc, TOP_K), lambda i, ds: (i, 0)),
                  pl.BlockSpec((tc, d), lambda i, ds: (i, 0)),
                  pl.BlockSpec((1, d), lambda i, ds: (0, 0)),
                  pl.BlockSpec(memory_space=pl.ANY)],
        out_specs=pl.BlockSpec((tc, d), lambda i, ds: (i, 0)),
        scratch_shapes=[pltpu.VMEM((2, TOP_K, tc, d), F32), pltpu.SemaphoreType.DMA((2,))])
    return pl.pallas_call(
        functools.partial(_combine_kernel, tc=tc), grid_spec=grid_spec,
        out_shape=jax.ShapeDtypeStruct((n, d), F32),
        compiler_params=_cparams(("arbitrary",)), name="moe_combine")(dest, gate, h, g_final.reshape(1, d), yb)


def _routing(top_i, n_e, tm):
    n_tok = top_i.shape[0]
    n_assign = n_tok * TOP_K
    flat_e = top_i.reshape(-1)
    onehot = (flat_e[:, None] == jnp.arange(n_e, dtype=jnp.int32)[None, :]).astype(jnp.int32)
    csum = jnp.cumsum(onehot, axis=0)
    counts = csum[-1]
    padded = ((counts + tm - 1) // tm) * tm
    pend = jnp.cumsum(padded)
    pstart = pend - padded
    dest = jnp.sum(onehot * (csum - onehot + pstart[None, :]), axis=1).astype(jnp.int32)
    n_blocks = -(-n_assign // tm) + n_e
    n_rows = n_blocks * tm
    row_tok = jnp.zeros((n_rows,), jnp.int32).at[dest].set(jnp.arange(n_assign, dtype=jnp.int32) // TOP_K)
    block_row0 = jnp.arange(n_blocks, dtype=jnp.int32) * tm
    block_e = jnp.minimum(jnp.sum((pend[None, :] <= block_row0[:, None]).astype(jnp.int32), axis=1), n_e - 1)
    n_used = (pend[-1:] // tm).astype(jnp.int32)
    return dest, row_tok, block_e, n_used


def kernel(x_prompt, x_sample, mem_prompt, cache_kv_latent, cache_k_rope, page_table, state_rwkv, state_rwkv_shift, cache_mem_k, cache_mem_v, g_attn_norm, w_in, b_gate, g_q_norm, w_uq, g_kv_norm, w_uk, w_uv, w_o_mla, mu_shift, w0_decay, w_decay_up, a0, w_aaa_up, k_k, k_a, r_k, ln_x_w, ln_x_b, w_o_rwkv, g_mem_norm, w_mem_k, w_mem_v, w_o_mem, w_out, g_ffn_norm, w_router, b_router, w_mlp1, b_mlp1, w_mlp2, b_mlp2, g_final):
    layer_params = dict(g_attn_norm=g_attn_norm, w_in=w_in, b_gate=b_gate, g_q_norm=g_q_norm, w_uq=w_uq,
                        g_kv_norm=g_kv_norm, w_uk=w_uk, w_uv=w_uv, w_o_mla=w_o_mla, mu_shift=mu_shift,
                        w0_decay=w0_decay, w_decay_up=w_decay_up, a0=a0, w_aaa_up=w_aaa_up, k_k=k_k, k_a=k_a,
                        r_k=r_k, ln_x_w=ln_x_w, ln_x_b=ln_x_b, w_o_rwkv=w_o_rwkv, g_mem_norm=g_mem_norm,
                        w_mem_k=w_mem_k, w_mem_v=w_mem_v, w_o_mem=w_o_mem, w_out=w_out, g_ffn_norm=g_ffn_norm,
                        w_router=w_router, b_router=b_router, w_mlp1=w_mlp1, b_mlp1=b_mlp1,
                        w_mlp2=w_mlp2, b_mlp2=b_mlp2)
    depth = w_in.shape[0]
    assert depth == 1, "single-layer trunk"
    W = {name: arr.reshape(arr.shape[1:]) for name, arr in layer_params.items()}
    B, T, D = x_prompt.shape
    DB, TS, _ = x_sample.shape
    assert TS == 1, "sample group decodes one token per sequence"
    n_pages = page_table.shape[1]
    page = cache_kv_latent.shape[2]
    past_len = n_pages * page
    kv_lora = w_uk.shape[1]
    mem_tokens = mem_prompt.shape[1]
    mem_w = MEM_HEADS * MEM_HEAD
    n_e = w_router.shape[2]
    rw_w = RWKV_HEADS * RWKV_HEAD

    tm = min(256, T)
    xp = x_prompt.reshape(B * T, D)
    xs = x_sample.reshape(DB, D)

    tab_p = _rope_tables(jnp.arange(T, dtype=jnp.int32))
    ckv_p, kpe_p, q_h, k_h, v_h = mla_proj_prompt(xp, tab_p, W, tm)
    rw_p, qm_p, gates_p = in_proj_rest(xp, W, tm)
    mk_p, mv_p = mem_kv(mem_prompt.reshape(B * mem_tokens, D), W)
    tq = min(FLASH_TQ, T)
    oa_p = flash_prompt(q_h, k_h, v_h, B, tq)
    oc_p = mem_attn_prompt(qm_p, mk_p.reshape(B, mem_tokens, mem_w), mv_p.reshape(B, mem_tokens, mem_w), tm)
    ob_p, z_p = rwkv_prompt(rw_p, B, W)
    woa_p = jnp.pad(W['w_o_mla'].reshape(MLA_HEADS, V_HEAD, D),
                    ((0, 0), (0, LANES - V_HEAD), (0, 0))).reshape(MLA_HEADS * LANES, D)
    h_p, hn_p, ti_p, tg_p = merge_out(oa_p, woa_p, ob_p, oc_p, gates_p, xp, W, tm)

    tab_s = _rope_tables(jnp.full((DB,), past_len, jnp.int32))
    ckv_s, kpe_s, qlat_s, qpe_s = mla_proj_sample(xs, tab_s, W)
    rw_s, qm_s, gates_s = in_proj_rest(xs, W, DB)
    oa_s = decode_attn(jnp.transpose(qlat_s, (1, 0, 2)), jnp.transpose(qpe_s, (1, 0, 2)),
                       ckv_s.reshape(DB, 1, kv_lora), kpe_s.reshape(DB, 1, QK_ROPE),
                       cache_kv_latent, jnp.swapaxes(cache_k_rope, 2, 3), page_table,
                       W['w_uv']).reshape(DB, MLA_HEADS * V_HEAD)
    oc_s = mem_attn_sample(qm_s.reshape(DB, MEM_HEADS, MEM_HEAD),
                           cache_mem_k.reshape(DB, mem_tokens, MEM_HEADS, MEM_HEAD),
                           cache_mem_v.reshape(DB, mem_tokens, MEM_HEADS, MEM_HEAD)).reshape(DB, mem_w)
    ob_s, st_s = rwkv_sample(rw_s, state_rwkv_shift.reshape(DB, -1),
                             state_rwkv.reshape(DB, RWKV_HEADS, RWKV_HEAD, RWKV_HEAD), W)
    h_s, hn_s, ti_s, tg_s = merge_out(oa_s.astype(BF16), W['w_o_mla'], ob_s, oc_s, gates_s, xs, W, DB)

    hn_all = jnp.concatenate([hn_p, hn_s], axis=0)
    top_i = jnp.concatenate([ti_p, ti_s], axis=0)
    dest, row_tok, block_e, n_used = _routing(top_i, n_e, MOE_TM)
    yb = grouped_mlp(hn_all, row_tok, block_e, n_used, W, MOE_TM)
    n_p = B * T
    y_p = moe_combine(yb, dest[:n_p * TOP_K], tg_p, h_p, g_final, min(128, T))
    y_s = moe_combine(yb, dest[n_p * TOP_K:], tg_s, h_s, g_final, DB)

    zq = z_p.reshape(B, RWKV_HEADS // 2, 2, RWKV_HEAD, 2, RWKV_HEAD)
    st_p = jnp.stack([zq[:, :, 0, :, 0, :], zq[:, :, 1, :, 1, :]], axis=2)
    st_p = jnp.swapaxes(st_p.reshape(B, RWKV_HEADS, RWKV_HEAD, RWKV_HEAD), -1, -2)
    lead = lambda a: a[None]
    return (y_p.reshape(B, T, D), y_s.reshape(DB, TS, D),
            lead(ckv_p.reshape(B, T, kv_lora)), lead(kpe_p.reshape(B, T, QK_ROPE)),
            lead(st_p), lead(rw_p.reshape(B, T, -1)[:, -1]),
            lead(mk_p.reshape(B, mem_tokens, MEM_HEADS, MEM_HEAD)),
            lead(mv_p.reshape(B, mem_tokens, MEM_HEADS, MEM_HEAD)),
            lead(ckv_s.reshape(DB, TS, kv_lora)), lead(kpe_s.reshape(DB, TS, QK_ROPE)),
            lead(st_s), lead(rw_s))
```

```python
import functools
import math

import numpy as np
import jax
import jax.numpy as jnp
from jax import lax
from jax.experimental import pallas as pl
from jax.experimental.pallas import tpu as pltpu

F32 = jnp.float32
BF16 = jnp.bfloat16

MLA_HEADS = 8
QK_NOPE = 64
QK_ROPE = 32
V_HEAD = 64
ROPE_THETA = 10000.0
RWKV_HEADS = 8
RWKV_HEAD = 64
DECAY_LORA = 64
AAA_LORA = 64
MEM_HEADS = 4
MEM_HEAD = 128
TOP_K = 4
SWIGLU_ALPHA = 1.702
SWIGLU_LIMIT = 7.0
RMS_EPS = 1e-6
GN_EPS = 64e-5
L2_EPS = 1e-12

LANES = 128
SUBLANES = 8
VMEM_LIMIT = 48 * 1024 * 1024

RWKV_CHUNK = 64
RWKV_GROUP = 2
MOE_TM = 256
FLASH_TQ = 1024
FLASH_SUBTILES = 2


def _cparams(sem, vmem=VMEM_LIMIT):
    return pltpu.CompilerParams(dimension_semantics=sem, vmem_limit_bytes=vmem)


def _rms(x, g, eps=RMS_EPS):
    return x * lax.rsqrt(jnp.mean(x * x, axis=-1, keepdims=True) + eps) * g


def _dot(a, b):
    return jnp.dot(a.astype(BF16), b.astype(BF16), preferred_element_type=F32)


def _dot_nt(a, b):
    return lax.dot_general(a.astype(BF16), b.astype(BF16), (((1,), (1,)), ((), ())),
                           preferred_element_type=F32)


def _split(a):
    hi = a.astype(BF16)
    lo = (a - hi.astype(F32)).astype(BF16)
    return hi, lo


def _dot3(a, b):
    ah, al = _split(a)
    bh, bl = _split(b)
    return (jnp.dot(ah, bh, preferred_element_type=F32)
            + (jnp.dot(ah, bl, preferred_element_type=F32)
               + jnp.dot(al, bh, preferred_element_type=F32)))


def _dot2_exact_rhs(a, b):
    ah, al = _split(a)
    bb = b.astype(BF16)
    return jnp.dot(ah, bb, preferred_element_type=F32) + jnp.dot(al, bb, preferred_element_type=F32)


def _dot2_exact_lhs(a, b):
    aa = a.astype(BF16)
    bh, bl = _split(b)
    return jnp.dot(aa, bh, preferred_element_type=F32) + jnp.dot(aa, bl, preferred_element_type=F32)


def _store_row_tiles(ref, x):
    rows, width = x.shape
    assert width == SUBLANES * LANES
    for s in range(SUBLANES):
        ref[pl.ds(s, rows, stride=SUBLANES), :] = x[:, LANES * s:LANES * (s + 1)]


def _load_row_tiles(ref, rows):
    return jnp.concatenate([ref[pl.ds(s, rows, stride=SUBLANES), :] for s in range(SUBLANES)], axis=1)


def _sigmoid(x):
    return 1.0 / (1.0 + jnp.exp(-x))


def _mla_proj_prompt_kernel(x_ref, ga_ref, wcq_ref, gq_ref, wckv_ref, gkv_ref,
                            wkp_ref, wkpr_ref, wkpp_ref, wkprp_ref,
                            wq_ref, wqr_ref, wuk_ref, wuv_ref,
                            cos32_ref, sin32_ref, cost_ref, sint_ref,
                            ckv_out, kpe_out, q_out, k_out, v_out, *, scale):
    xn = _rms(x_ref[...], ga_ref[...]).astype(BF16)
    cqn = _rms(jnp.dot(xn, wcq_ref[...], preferred_element_type=F32), gq_ref[...]).astype(BF16)
    ckvn = _rms(jnp.dot(xn, wckv_ref[...], preferred_element_type=F32), gkv_ref[...])
    ckv_out[...] = ckvn
    kpe_out[...] = (jnp.dot(xn, wkp_ref[...], preferred_element_type=F32) * cos32_ref[...]
                    + jnp.dot(xn, wkpr_ref[...], preferred_element_type=F32) * sin32_ref[...])
    cost = cost_ref[...]
    sint = sint_ref[...]
    kpp = (jnp.dot(xn, wkpp_ref[...], preferred_element_type=F32) * cost
           + jnp.dot(xn, wkprp_ref[...], preferred_element_type=F32) * sint)
    qa = jnp.dot(cqn, wq_ref[...], preferred_element_type=F32)
    qb = jnp.dot(cqn, wqr_ref[...], preferred_element_type=F32)
    ckb = ckvn.astype(BF16)
    ka = jnp.dot(ckb, wuk_ref[...], preferred_element_type=F32)
    va = jnp.dot(ckb, wuv_ref[...], preferred_element_type=F32)
    ones_lane = lax.broadcasted_iota(jnp.int32, cost.shape, 1) == V_HEAD
    for h in range(MLA_HEADS):
        sl = slice(LANES * h, LANES * (h + 1))
        q_out[h] = ((qa[:, sl] * cost + qb[:, sl] * sint) * scale).astype(BF16)
        k_out[h] = (ka[:, sl] + kpp).astype(BF16)
        v_out[h] = jnp.where(ones_lane, 1.0, va[:, sl]).astype(BF16)


def _mla_proj_sample_kernel(x_ref, ga_ref, wcq_ref, gq_ref, wckv_ref, gkv_ref,
                            wkp_ref, wkpr_ref, wqn_ref, wukt_ref, wqp_ref, wqpr_ref,
                            cos32_ref, sin32_ref,
                            ckv_out, kpe_out, qlat_out, qpe_out, *, scale):
    xn = _rms(x_ref[...], ga_ref[...]).astype(BF16)
    cqn = _rms(jnp.dot(xn, wcq_ref[...], preferred_element_type=F32), gq_ref[...]).astype(BF16)
    ckv_out[...] = _rms(jnp.dot(xn, wckv_ref[...], preferred_element_type=F32), gkv_ref[...])
    cos32 = cos32_ref[...]
    sin32 = sin32_ref[...]
    kpe_out[...] = (jnp.dot(xn, wkp_ref[...], preferred_element_type=F32) * cos32
                    + jnp.dot(xn, wkpr_ref[...], preferred_element_type=F32) * sin32)
    qn = jnp.dot(cqn, wqn_ref[...], preferred_element_type=F32)
    for h in range(MLA_HEADS):
        sl = slice(LANES * h, LANES * (h + 1))
        qlat_out[h] = (_dot(qn[:, sl], wukt_ref[h]) * scale).astype(BF16)
        qpe_out[h] = ((jnp.dot(cqn, wqp_ref[h], preferred_element_type=F32) * cos32
                       + jnp.dot(cqn, wqpr_ref[h], preferred_element_type=F32) * sin32) * scale)


def _full(shape):
    nd = len(shape)
    return pl.BlockSpec(shape, lambda *_: (0,) * nd)


def _rope_tables(pos):
    half = QK_ROPE // 2
    inv = jnp.exp(-math.log(ROPE_THETA) * jnp.arange(half, dtype=F32) / half)
    ang = pos.astype(F32)[:, None] * inv[None, :]
    cos = jnp.cos(ang)
    sin = jnp.sin(ang)
    return jnp.concatenate([cos, cos], axis=1), jnp.concatenate([sin, sin], axis=1)


def _rot_cols(w):
    half = QK_ROPE // 2
    return jnp.concatenate([-w[..., half:], w[..., :half]], axis=-1)


def _pad_cols(w, lo, total=LANES):
    pad = [(0, 0)] * (w.ndim - 1) + [(lo, total - lo - w.shape[-1])]
    return jnp.pad(w, pad)


def _mla_weights(W):
    d = W['w_in'].shape[0]
    q_lora = W['w_uq'].shape[0]
    kv_lora = W['w_uk'].shape[0]
    c0, c1, c2 = q_lora, q_lora + kv_lora, q_lora + kv_lora + QK_ROPE
    w_in = W['w_in']
    wcq = w_in[:, :c0].astype(BF16)
    wckv = w_in[:, c0:c1].astype(BF16)
    wkp = w_in[:, c1:c2]
    wuq = W['w_uq'].reshape(q_lora, MLA_HEADS, QK_NOPE + QK_ROPE)
    return dict(d=d, q_lora=q_lora, kv_lora=kv_lora, wcq=wcq, wckv=wckv, wkp=wkp, wuq=wuq,
                ga=W['g_attn_norm'].reshape(1, d), gq=W['g_q_norm'].reshape(1, q_lora),
                gkv=W['g_kv_norm'].reshape(1, kv_lora))


def mla_proj_prompt(x, pos_tables, W, tm):
    n, d = x.shape
    m = _mla_weights(W)
    q_lora, kv_lora = m['q_lora'], m['kv_lora']
    cos32, sin32 = pos_tables
    t = cos32.shape[0]
    ones = jnp.ones((t, QK_NOPE), F32)
    zeros_hi = jnp.zeros((t, LANES - QK_NOPE - QK_ROPE), F32)
    cost = jnp.concatenate([ones, cos32, zeros_hi], axis=1)
    sint = jnp.concatenate([jnp.zeros((t, QK_NOPE), F32), sin32, zeros_hi], axis=1)
    wkp = m['wkp']
    wkpr = _rot_cols(wkp)
    wuq = m['wuq']
    wq = _pad_cols(wuq, 0).reshape(q_lora, MLA_HEADS * LANES).astype(BF16)
    wqr = _pad_cols(_rot_cols(wuq[..., QK_NOPE:]), QK_NOPE).reshape(q_lora, MLA_HEADS * LANES).astype(BF16)
    wuk = _pad_cols(W['w_uk'], 0).reshape(kv_lora, MLA_HEADS * LANES).astype(BF16)
    wuv = _pad_cols(W['w_uv'], 0).reshape(kv_lora, MLA_HEADS * LANES).astype(BF16)
    args = (x, m['ga'], m['wcq'], m['gq'], m['wckv'], m['gkv'],
            wkp.astype(BF16), wkpr.astype(BF16),
            _pad_cols(wkp, QK_NOPE).astype(BF16), _pad_cols(wkpr, QK_NOPE).astype(BF16),
            wq, wqr, wuk, wuv, cos32, sin32, cost, sint)
    nt = t // tm
    row = lambda i: (i, 0)
    tab = lambda i: (i % nt, 0)
    in_specs = [pl.BlockSpec((tm, d), row)] + [_full(a.shape) for a in args[1:14]] + [
        pl.BlockSpec((tm, QK_ROPE), tab), pl.BlockSpec((tm, QK_ROPE), tab),
        pl.BlockSpec((tm, LANES), tab), pl.BlockSpec((tm, LANES), tab)]
    head = pl.BlockSpec((MLA_HEADS, tm, LANES), lambda i: (0, i, 0))
    out_shape = (jax.ShapeDtypeStruct((n, kv_lora), F32), jax.ShapeDtypeStruct((n, QK_ROPE), F32),
                 jax.ShapeDtypeStruct((MLA_HEADS, n, LANES), BF16),
                 jax.ShapeDtypeStruct((MLA_HEADS, n, LANES), BF16),
                 jax.ShapeDtypeStruct((MLA_HEADS, n, LANES), BF16))
    out_specs = (pl.BlockSpec((tm, kv_lora), row), pl.BlockSpec((tm, QK_ROPE), row), head, head, head)
    scale = (QK_NOPE + QK_ROPE) ** -0.5 * math.log2(math.e)
    return pl.pallas_call(
        functools.partial(_mla_proj_prompt_kernel, scale=scale),
        grid=(n // tm,), in_specs=in_specs, out_specs=out_specs, out_shape=out_shape,
        compiler_params=_cparams(("parallel",)), name="mla_proj_prompt")(*args)


def mla_proj_sample(x, pos_tables, W):
    n, d = x.shape
    m = _mla_weights(W)
    q_lora, kv_lora = m['q_lora'], m['kv_lora']
    cos32, sin32 = pos_tables
    wkp = m['wkp']
    wuq = m['wuq']
    wqn = _pad_cols(wuq[..., :QK_NOPE], 0).reshape(q_lora, MLA_HEADS * LANES).astype(BF16)
    wukt = jnp.pad(jnp.transpose(W['w_uk'], (1, 2, 0)), ((0, 0), (0, LANES - QK_NOPE), (0, 0))).astype(BF16)
    wqp = jnp.transpose(wuq[..., QK_NOPE:], (1, 0, 2))
    args = (x, m['ga'], m['wcq'], m['gq'], m['wckv'], m['gkv'],
            wkp.astype(BF16), _rot_cols(wkp).astype(BF16), wqn, wukt,
            wqp.astype(BF16), _rot_cols(wqp).astype(BF16), cos32, sin32)
    out_shape = (jax.ShapeDtypeStruct((n, kv_lora), F32), jax.ShapeDtypeStruct((n, QK_ROPE), F32),
                 jax.ShapeDtypeStruct((MLA_HEADS, n, kv_lora), BF16),
                 jax.ShapeDtypeStruct((MLA_HEADS, n, QK_ROPE), F32))
    scale = (QK_NOPE + QK_ROPE) ** -0.5
    return pl.pallas_call(
        functools.partial(_mla_proj_sample_kernel, scale=scale),
        grid=(1,), in_specs=[_full(a.shape) for a in args],
        out_specs=tuple(_full(s.shape) for s in out_shape), out_shape=out_shape,
        compiler_params=_cparams(("arbitrary",)), name="mla_proj_sample")(*args)


def _in_proj_rest_kernel(x_ref, ga_ref, wrw_ref, wqm_ref, wg_ref, bg_ref,
                         rw_out, qm_out, g_out, *, mem_scale):
    xn = _rms(x_ref[...], ga_ref[...]).astype(BF16)
    rw_out[...] = jnp.dot(xn, wrw_ref[...], preferred_element_type=F32)
    qm_out[...] = (jnp.dot(xn, wqm_ref[...], preferred_element_type=F32) * mem_scale).astype(BF16)
    g_out[...] = _sigmoid(jnp.dot(xn, wg_ref[...], preferred_element_type=F32) + bg_ref[...]).astype(BF16)


def in_proj_rest(x, W, tm):
    n, d = x.shape
    q_lora = W['w_uq'].shape[0]
    kv_lora = W['w_uk'].shape[0]
    rw_cols = W['mu_shift'].shape[0]
    mem_w = MEM_HEADS * MEM_HEAD
    c2 = q_lora + kv_lora + QK_ROPE
    c3 = c2 + rw_cols
    c4 = c3 + mem_w
    w_in = W['w_in']
    n_gate = w_in.shape[1] - c4
    args = (x, W['g_attn_norm'].reshape(1, d), w_in[:, c2:c3].astype(BF16), w_in[:, c3:c4].astype(BF16),
            w_in[:, c4:].astype(BF16), W['b_gate'].reshape(1, n_gate))
    row = lambda i: (i, 0)
    out_shape = (jax.ShapeDtypeStruct((n, rw_cols), F32), jax.ShapeDtypeStruct((n, mem_w), BF16),
                 jax.ShapeDtypeStruct((n, n_gate), BF16))
    return pl.pallas_call(
        functools.partial(_in_proj_rest_kernel, mem_scale=MEM_HEAD ** -0.5),
        grid=(n // tm,),
        in_specs=[pl.BlockSpec((tm, d), row)] + [_full(a.shape) for a in args[1:]],
        out_specs=(pl.BlockSpec((tm, rw_cols), row), pl.BlockSpec((tm, mem_w), row),
                   pl.BlockSpec((tm, n_gate), row)),
        out_shape=out_shape, compiler_params=_cparams(("parallel",)), name="in_proj_rest")(*args)


def _mem_kv_kernel(m_ref, g_ref, wk_ref, wv_ref, k_out, v_out):
    mn = _rms(m_ref[...], g_ref[...]).astype(BF16)
    k_out[...] = jnp.dot(mn, wk_ref[...], preferred_element_type=F32)
    v_out[...] = jnp.dot(mn, wv_ref[...], preferred_element_type=F32)


def mem_kv(mem, W):
    n, d = mem.shape
    mem_w = MEM_HEADS * MEM_HEAD
    args = (mem, W['g_mem_norm'].reshape(1, d), W['w_mem_k'].astype(BF16), W['w_mem_v'].astype(BF16))
    out_shape = (jax.ShapeDtypeStruct((n, mem_w), F32), jax.ShapeDtypeStruct((n, mem_w), F32))
    return pl.pallas_call(
        _mem_kv_kernel, grid=(1,), in_specs=[_full(a.shape) for a in args],
        out_specs=(_full((n, mem_w)), _full((n, mem_w))), out_shape=out_shape,
        compiler_params=_cparams(("arbitrary",)), name="mem_kv")(*args)


def _flash_kernel(q_ref, k_ref, v_ref, o_ref, *scratch, tq, n_sub):
    qi = pl.program_id(2)
    sub = tq // n_sub
    m_scs = scratch[:n_sub]
    acc_scs = scratch[n_sub:2 * n_sub]
    s_scs = scratch[2 * n_sub:]
    for u in range(n_sub):
        m_scs[u][...] = jnp.full(m_scs[u].shape, -jnp.inf, F32)
        acc_scs[u][...] = jnp.zeros(acc_scs[u].shape, F32)

    def scores(j, slot):
        kb = k_ref[pl.ds(pl.multiple_of(j * tq, tq), tq), :]
        for u in range(n_sub):
            s_scs[slot][u] = lax.dot_general(q_ref[u * sub:(u + 1) * sub, :], kb, (((1,), (1,)), ((), ())),
                                            preferred_element_type=F32)

    def consume(j, slot, masked):
        vb = v_ref[pl.ds(pl.multiple_of(j * tq, tq), tq), :]
        for u in range(n_sub):
            s = s_scs[slot][u]
            if masked:
                rows = u * sub + lax.broadcasted_iota(jnp.int32, (sub, tq), 0)
                cols = lax.broadcasted_iota(jnp.int32, (sub, tq), 1)
                s = jnp.where(cols <= rows, s, -jnp.inf)
            m_prev = m_scs[u][...]
            m_new = jnp.maximum(m_prev, jnp.max(s, axis=-1, keepdims=True))
            alpha = jnp.exp2(m_prev - m_new)
            p = jnp.exp2(s - jnp.tile(m_new, (1, tq // LANES)))
            acc_scs[u][...] = alpha * acc_scs[u][...] + jnp.dot(p.astype(BF16), vb, preferred_element_type=F32)
            m_scs[u][...] = m_new

    scores(0, 0)

    def body(i, carry):
        scores(2 * i + 1, 1)
        consume(2 * i, 0, False)
        scores(2 * i + 2, 0)
        consume(2 * i + 1, 1, False)
        return carry

    lax.fori_loop(0, qi // 2, body, 0)

    @pl.when(qi % 2 == 1)
    def _():
        scores(qi, 1)
        consume(qi - 1, 0, False)
        consume(qi, 1, True)

    @pl.when(qi % 2 == 0)
    def _():
        consume(qi, 0, True)

    for u in range(n_sub):
        acc = acc_scs[u][...]
        o_ref[u * sub:(u + 1) * sub, :] = (acc / acc[:, V_HEAD:V_HEAD + 1]).astype(o_ref.dtype)


def flash_prompt(q, k, v, batch, tq):
    h, n, dh = q.shape
    t = n // batch
    nq = t // tq
    kv_spec = pl.BlockSpec((None, t, dh), lambda b, hh, i: (hh, b, 0))
    n_sub = FLASH_SUBTILES
    sub = tq // n_sub
    return pl.pallas_call(
        functools.partial(_flash_kernel, tq=tq, n_sub=n_sub), grid=(batch, h, nq),
        in_specs=[pl.BlockSpec((None, tq, dh), lambda b, hh, i: (hh, b * nq + i, 0)), kv_spec, kv_spec],
        out_specs=pl.BlockSpec((tq, dh), lambda b, hh, i: (b * nq + i, hh)),
        out_shape=jax.ShapeDtypeStruct((n, h * dh), BF16),
        scratch_shapes=([pltpu.VMEM((sub, LANES), F32)] * n_sub + [pltpu.VMEM((sub, dh), F32)] * n_sub
                        + [pltpu.VMEM((n_sub, sub, tq), F32)] * 2),
        compiler_params=_cparams(("parallel", "parallel", "arbitrary")), name="flash_prompt")(q, k, v)


def _decode_kernel(pt_ref, qlat_ref, qpe_ref, ckv_ref, kpe_ref, wuv_ref, lat_hbm, rope_hbm, o_ref,
                   lat_buf, rope_buf, sem, *, n_pages, page):
    b = pl.program_id(0)

    def page_copies(seq, slot, i):
        pg = pt_ref[seq, i]
        rows = pl.ds(i * page, page)
        return (pltpu.make_async_copy(lat_hbm.at[0, pg], lat_buf.at[slot, rows], sem.at[slot]),
                pltpu.make_async_copy(rope_hbm.at[0, pg], rope_buf.at[slot, :, rows], sem.at[slot]))

    def start_all(seq, slot):
        for i in range(n_pages):
            for c in page_copies(seq, slot, i):
                c.start()

    @pl.when(b == 0)
    def _():
        start_all(0, 0)

    @pl.when(b + 1 < pl.num_programs(0))
    def _():
        start_all(b + 1, (b + 1) % 2)

    slot = b % 2

    def drain(i, carry):
        for c in page_copies(b, slot, 0):
            c.wait()
        return carry

    lax.fori_loop(0, n_pages, drain, 0)

    qlat = qlat_ref[0]
    qpe = qpe_ref[0]
    kall = lat_buf[slot].astype(BF16)
    s = _dot_nt(qlat, kall) + _dot(qpe, rope_buf[slot])
    ckv = ckv_ref[0]
    kpe = kpe_ref[0]
    s_new = (jnp.sum(qlat.astype(F32) * ckv, axis=-1, keepdims=True)
             + jnp.sum(qpe * kpe, axis=-1, keepdims=True))
    m = jnp.maximum(jnp.max(s, axis=-1, keepdims=True), s_new)
    p = jnp.exp(s - m)
    p_new = jnp.exp(s_new - m)
    l = jnp.sum(p, axis=-1, keepdims=True) + p_new
    o_lat = (jnp.dot(p.astype(BF16), kall, preferred_element_type=F32) + p_new * ckv) / l
    ov = _dot(o_lat, wuv_ref[...])
    hh = lax.broadcasted_iota(jnp.int32, ov.shape, 0)
    cc = lax.broadcasted_iota(jnp.int32, ov.shape, 1) // V_HEAD
    o_ref[0] = jnp.sum(jnp.where(hh == cc, ov, 0.0), axis=0, keepdims=True)


def decode_attn(qlat, qpe, ckv, kpe, cache_lat, cache_rope, page_table, w_uv):
    db, h, kv = qlat.shape
    _, n_pool, page, _ = cache_lat.shape
    n_pages = page_table.shape[1]
    past = n_pages * page
    wuv = w_uv.reshape(kv, h * V_HEAD).astype(BF16)
    per_b = lambda shape: pl.BlockSpec((1,) + shape, lambda b, pt: (b, 0, 0))
    in_specs = [per_b((h, kv)), per_b((h, QK_ROPE)), per_b((1, kv)), per_b((1, QK_ROPE)),
                pl.BlockSpec(wuv.shape, lambda b, pt: (0, 0)),
                pl.BlockSpec(memory_space=pl.ANY), pl.BlockSpec(memory_space=pl.ANY)]
    grid_spec = pltpu.PrefetchScalarGridSpec(
        num_scalar_prefetch=1, grid=(db,), in_specs=in_specs,
        out_specs=per_b((1, h * V_HEAD)),
        scratch_shapes=[pltpu.VMEM((2, past, kv), F32), pltpu.VMEM((2, QK_ROPE, past), F32),
                        pltpu.SemaphoreType.DMA((2,))])
    return pl.pallas_call(
        functools.partial(_decode_kernel, n_pages=n_pages, page=page), grid_spec=grid_spec,
        out_shape=jax.ShapeDtypeStruct((db, 1, h * V_HEAD), F32),
        compiler_params=_cparams(("arbitrary",)), name="decode_attn")(
            page_table, qlat, qpe, ckv, kpe, wuv, cache_lat, cache_rope)


def _mem_attn_prompt_kernel(q_ref, k_ref, v_ref, o_ref):
    q = q_ref[...]
    k = k_ref[0].astype(BF16)
    v = v_ref[0].astype(BF16)
    for h in range(MEM_HEADS):
        sl = slice(MEM_HEAD * h, MEM_HEAD * (h + 1))
        s = _dot_nt(q[:, sl], k[:, sl])
        p = jnp.exp(s - jnp.max(s, axis=-1, keepdims=True))
        o = jnp.dot(p.astype(BF16), v[:, sl], preferred_element_type=F32) / jnp.sum(p, axis=-1, keepdims=True)
        o_ref[:, sl] = o.astype(o_ref.dtype)


def mem_attn_prompt(qm, mk, mv, tm):
    n, w = qm.shape
    b, m, _ = mk.shape
    nt = n // b // tm
    kv_spec = pl.BlockSpec((1, m, w), lambda i: (i // nt, 0, 0))
    return pl.pallas_call(
        _mem_attn_prompt_kernel, grid=(n // tm,),
        in_specs=[pl.BlockSpec((tm, w), lambda i: (i, 0)), kv_spec, kv_spec],
        out_specs=pl.BlockSpec((tm, w), lambda i: (i, 0)),
        out_shape=jax.ShapeDtypeStruct((n, w), BF16),
        compiler_params=_cparams(("parallel",)), name="mem_attn_prompt")(qm, mk, mv)


def _mem_attn_sample_kernel(q_ref, k_ref, v_ref, o_ref, *, tb):
    for bl in range(tb):
        q = q_ref[bl].astype(F32)
        s = jnp.sum(k_ref[bl] * q, axis=-1, keepdims=True)
        p = jnp.exp(s - jnp.max(s, axis=0, keepdims=True))
        o = jnp.sum(p * v_ref[bl], axis=0) / jnp.sum(p, axis=0)
        o_ref[bl] = o.astype(o_ref.dtype)


def mem_attn_sample(qm, mk, mv, tb=8):
    db, nh, hd = qm.shape
    m = mk.shape[1]
    kv_spec = pl.BlockSpec((tb, m, nh, hd), lambda i: (i, 0, 0, 0))
    q_spec = pl.BlockSpec((tb, nh, hd), lambda i: (i, 0, 0))
    return pl.pallas_call(
        functools.partial(_mem_attn_sample_kernel, tb=tb), grid=(db // tb,),
        in_specs=[q_spec, kv_spec, kv_spec], out_specs=q_spec,
        out_shape=jax.ShapeDtypeStruct((db, nh, hd), BF16),
        compiler_params=_cparams(("parallel",)), name="mem_attn_sample")(qm, mk, mv)


def _rwkv_prep(xr, lora_w, w0, a0, kk_w, ka_w, gmat):
    w = RWKV_HEADS * RWKV_HEAD
    r = xr[:, 0:w]
    k = xr[:, w:2 * w]
    v = xr[:, 2 * w:3 * w]
    tail = xr[:, 3 * w:3 * w + DECAY_LORA + AAA_LORA]
    lane = lax.broadcasted_iota(jnp.int32, tail.shape, 1)
    z = jnp.where(lane < DECAY_LORA, jnp.tanh(tail), tail)
    lo = _dot3(z, lora_w)
    dec = lo[:, :w] + w0
    aaa = lo[:, w:] + a0
    u = -dec
    softplus = jnp.maximum(u, 0.0) + jnp.log(1.0 + jnp.exp(-jnp.abs(u)))
    logw = -jnp.exp(-softplus - 0.5)
    a_sig = _sigmoid(aaa)
    kk = k * kk_w
    norm = jnp.sqrt(_dot2_exact_rhs(kk * kk, gmat))
    kkn = kk / jnp.maximum(norm, L2_EPS)
    k2 = k * (1.0 + (a_sig - 1.0) * ka_w)
    return r, k2, v, logw, -kkn, kkn * a_sig


def _rwkv_post(y, r, k2, v, rk, lnw, lnb, gmat):
    inv_n = 1.0 / RWKV_HEAD
    mu = _dot2_exact_rhs(y, gmat) * inv_n
    d = y - mu
    var = _dot2_exact_rhs(d * d, gmat) * inv_n
    yn = d * lax.rsqrt(var + GN_EPS) * lnw + lnb
    return yn + _dot2_exact_rhs(r * k2 * rk, gmat) * v


def _rwkv_prompt_kernel(rw_ref, mu_ref, lora_ref, w0_ref, a0_ref, kk_ref, ka_ref, rk_ref,
                        lnw_ref, lnb_ref, g_ref, out_ref, zout_ref, prev_sc, z_sc, *, chunk):
    c = chunk
    nb = rw_ref.shape[0]

    @pl.when(pl.program_id(0) == 0)
    def _():
        prev_sc[...] = jnp.zeros(prev_sc.shape, F32)
        z_sc[...] = jnp.zeros(z_sc.shape, F32)

    rw = jnp.concatenate([rw_ref[b] for b in range(nb)], axis=0)
    row = lax.broadcasted_iota(jnp.int32, rw.shape, 0)
    prev = pltpu.roll(rw, 1, 0)
    for b in range(nb):
        prev = jnp.where(row == b * c, prev_sc[b:b + 1, :], prev)
        prev_sc[b:b + 1, :] = rw[(b + 1) * c - 1:(b + 1) * c, :]
    xr = rw + (prev - rw) * mu_ref[...]
    gmat = g_ref[...]
    r, k2, v, logw, av, bv = _rwkv_prep(xr, lora_ref[...], w0_ref[...], a0_ref[...],
                                        kk_ref[...], ka_ref[...], gmat)

    ti = lax.broadcasted_iota(jnp.int32, (nb * c, nb * c), 0)
    si = lax.broadcasted_iota(jnp.int32, (nb * c, nb * c), 1)
    causal = jnp.logical_and(si <= ti, (si // c) == (ti // c))
    cum = _dot2_exact_lhs(causal.astype(F32), logw)
    clast = jnp.concatenate([jnp.broadcast_to(cum[(b + 1) * c - 1:(b + 1) * c, :], (c, cum.shape[1]))
                             for b in range(nb)], axis=0)
    e_neg = jnp.exp(-cum)
    e_rest = jnp.exp(clast - cum)
    at = av * jnp.exp(cum - logw)
    bt = bv * e_neg
    kt = k2 * e_neg
    rt = r * jnp.exp(cum)
    bg = bv * e_rest
    kg = k2 * e_rest
    glast = jnp.exp(clast)

    g = RWKV_GROUP
    gl = g * RWKV_HEAD
    gc = g * c
    n_groups = RWKV_HEADS // g
    emask = (lax.broadcasted_iota(jnp.int32, (gc, gl), 0) // c
             == lax.broadcasted_iota(jnp.int32, (gc, gl), 1) // RWKV_HEAD)
    br = lax.broadcasted_iota(jnp.int32, (gc, gc), 0)
    bc = lax.broadcasted_iota(jnp.int32, (gc, gc), 1)
    same = (br // c) == (bc // c)
    strict = jnp.logical_and(same, (br % c) > (bc % c))
    incl = jnp.logical_and(same, (br % c) >= (bc % c))
    eye_gc = (br == bc).astype(F32)
    lane_head = lax.broadcasted_iota(jnp.int32, (c, gl), 1) // RWKV_HEAD
    eye_gl = (lax.broadcasted_iota(jnp.int32, (gl, gl), 0) == lax.broadcasted_iota(jnp.int32, (gl, gl), 1))

    def dup(x):
        return jnp.concatenate([x] * g, axis=0)

    def expand(x):
        return jnp.where(emask, dup(x), 0.0)

    def per_head(x):
        return jnp.concatenate([jnp.where(lane_head == q, x, 0.0) for q in range(g)], axis=0)

    items = [(b, p) for b in range(nb) for p in range(n_groups)]

    def part(x, item):
        b, p = item
        return x[b * c:(b + 1) * c, gl * p:gl * (p + 1)]

    aps = []
    for it in items:
        lhs = jnp.concatenate([part(at, it), part(rt, it)], axis=0)
        rhs = jnp.concatenate([per_head(part(bt, it)), per_head(part(kt, it))], axis=0)
        aps.append(_dot_nt(lhs, rhs))
    nmats = [jnp.where(strict, dup(ap[0:c, 0:gc]), 0.0) for ap in aps]
    akms = [jnp.where(strict, dup(ap[0:c, gc:2 * gc]), 0.0) for ap in aps]
    rbms = [jnp.where(incl, dup(ap[c:2 * c, 0:gc]), 0.0) for ap in aps]
    rkms = [jnp.where(incl, dup(ap[c:2 * c, gc:2 * gc]), 0.0) for ap in aps]
    tmats = [eye_gc + nm for nm in nmats]
    pws = nmats
    for _ in range(int(math.log2(c)) - 1):
        pws = [_dot(pw, pw) for pw in pws]
        tmats = [tm + _dot(tm, pw) for tm, pw in zip(tmats, pws)]
    ves = [expand(part(v, it)) for it in items]
    akvs = [_dot(akm, ve) for akm, ve in zip(akms, ves)]
    wus = [_dot(tm, jnp.concatenate([expand(part(at, it)), akv], axis=1))
           for tm, it, akv in zip(tmats, items, akvs)]
    bkts = [jnp.concatenate([expand(part(bg, it)), expand(part(kg, it))], axis=0).T
            for it in items]
    gcols = [jnp.sum(jnp.where(eye_gl, part(glast, it)[0:1, :], 0.0), axis=1, keepdims=True) for it in items]
    zs = [z_sc[b, p] for b, p in items]
    ues = [_dot(wu[:, :gl], z) + wu[:, gl:] for wu, z in zip(wus, zs)]
    for n, (b, p) in enumerate(items):
        z_new = gcols[n] * zs[n] + _dot3(bkts[n], jnp.concatenate([ues[n], ves[n]], axis=0))
        z_sc[b, p] = z_new
        zout_ref[b, p] = z_new
    ys = [[None] * n_groups for _ in range(nb)]
    for n, (b, p) in enumerate(items):
        ye = _dot(jnp.concatenate([expand(part(rt, items[n])), rbms[n], rkms[n]], axis=1),
                  jnp.concatenate([zs[n], ues[n], ves[n]], axis=0))
        yp = ye[0:c, :]
        for q in range(1, g):
            yp = yp + ye[q * c:(q + 1) * c, :]
        ys[b][p] = yp
    y = jnp.concatenate([jnp.concatenate(yb, axis=1) for yb in ys], axis=0)
    out = _rwkv_post(y, r, k2, v, rk_ref[...], lnw_ref[...], lnb_ref[...], gmat)
    for b in range(nb):
        out_ref[b] = out[b * c:(b + 1) * c, :].astype(out_ref.dtype)


def _rwkv_params(W):
    w = RWKV_HEADS * RWKV_HEAD
    lora = jnp.zeros((DECAY_LORA + AAA_LORA, 2 * w), F32)
    lora = lora.at[:DECAY_LORA, :w].set(W['w_decay_up']).at[DECAY_LORA:, w:].set(W['w_aaa_up'])
    head = np.arange(w) // RWKV_HEAD
    gmat = jnp.asarray((head[:, None] == head[None, :]).astype(np.float32))
    row = lambda a: a.reshape(1, -1)
    return dict(mu=row(W['mu_shift']), lora=lora, w0=row(W['w0_decay']), a0=row(W['a0']),
                kk=row(W['k_k']), ka=row(W['k_a']), rk=row(W['r_k']), lnw=row(W['ln_x_w']),
                lnb=row(W['ln_x_b']), gmat=gmat)


def rwkv_prompt(rw, batch, W):
    n, cols = rw.shape
    t = n // batch
    c = RWKV_CHUNK
    nc = t // c
    w = RWKV_HEADS * RWKV_HEAD
    pr = _rwkv_params(W)
    args = (rw.reshape(batch, t, cols), pr['mu'], pr['lora'], pr['w0'], pr['a0'], pr['kk'], pr['ka'], pr['rk'],
            pr['lnw'], pr['lnb'], pr['gmat'])
    in_specs = [pl.BlockSpec((batch, c, cols), lambda i: (0, i, 0))] + [
        pl.BlockSpec(a.shape, lambda i: (0, 0)) for a in args[1:]]
    n_groups = RWKV_HEADS // RWKV_GROUP
    gl = RWKV_GROUP * RWKV_HEAD
    out_shape = (jax.ShapeDtypeStruct((batch, t, w), BF16),
                 jax.ShapeDtypeStruct((batch, n_groups, gl, gl), F32))
    ob, z = pl.pallas_call(
        functools.partial(_rwkv_prompt_kernel, chunk=c), grid=(nc,), in_specs=in_specs,
        out_specs=(pl.BlockSpec((batch, c, w), lambda i: (0, i, 0)),
                   pl.BlockSpec((batch, n_groups, gl, gl), lambda i: (0, 0, 0, 0))),
        out_shape=out_shape,
        scratch_shapes=[pltpu.VMEM((batch, cols), F32), pltpu.VMEM((batch, n_groups, gl, gl), F32)],
        compiler_params=_cparams(("arbitrary",)), name="rwkv_prompt")(*args)
    return ob.reshape(n, w), z


def _rwkv_sample_prep_kernel(rw_ref, sh_ref, mu_ref, lora_ref, w0_ref, a0_ref, kk_ref, ka_ref, g_ref,
                             r_out, k_out, v_out, w_out, a_out, b_out):
    rw = rw_ref[...]
    xr = rw + (sh_ref[...] - rw) * mu_ref[...]
    r, k2, v, logw, av, bv = _rwkv_prep(xr, lora_ref[...], w0_ref[...], a0_ref[...],
                                        kk_ref[...], ka_ref[...], g_ref[...])
    r_out[...] = r
    k_out[...] = k2
    v_out[...] = v
    w_out[...] = jnp.exp(logw)
    a_out[...] = av
    b_out[...] = bv


def _rwkv_sample_state_kernel(s_ref, r_ref, k_ref, v_ref, w_ref, a_ref, b_ref, s_out, y_out, *, tb):
    n = RWKV_HEAD
    eye = lax.broadcasted_iota(jnp.int32, (n, n), 0) == lax.broadcasted_iota(jnp.int32, (n, n), 1)
    for bl in range(tb):
        for h in range(RWKV_HEADS):
            s = s_ref[bl, h]
            row = lambda ref: ref[bl, h:h + 1, :]
            sa = jnp.sum(s * row(a_ref), axis=1, keepdims=True)
            vcol = jnp.sum(jnp.where(eye, row(v_ref), 0.0), axis=1, keepdims=True)
            s_new = s * row(w_ref) + sa * row(b_ref) + vcol * row(k_ref)
            s_out[bl, h] = s_new
            ycol = jnp.sum(s_new * row(r_ref), axis=1, keepdims=True)
            y_out[bl, h:h + 1, :] = jnp.sum(jnp.where(eye, ycol, 0.0), axis=0, keepdims=True)


def _rwkv_sample_post_kernel(y_ref, r_ref, k_ref, v_ref, rk_ref, lnw_ref, lnb_ref, g_ref, o_ref):
    o_ref[...] = _rwkv_post(y_ref[...], r_ref[...], k_ref[...], v_ref[...], rk_ref[...],
                            lnw_ref[...], lnb_ref[...], g_ref[...]).astype(o_ref.dtype)


def rwkv_sample(rw, shift0, state, W, tb=8):
    db, cols = rw.shape
    w = RWKV_HEADS * RWKV_HEAD
    pr = _rwkv_params(W)
    args = (rw, shift0, pr['mu'], pr['lora'], pr['w0'], pr['a0'], pr['kk'], pr['ka'], pr['gmat'])
    vec = jax.ShapeDtypeStruct((db, w), F32)
    r, k2, v, wd, av, bv = pl.pallas_call(
        _rwkv_sample_prep_kernel, grid=(1,), in_specs=[_full(a.shape) for a in args],
        out_specs=tuple(_full((db, w)) for _ in range(6)), out_shape=(vec,) * 6,
        compiler_params=_cparams(("arbitrary",)), name="rwkv_sample_prep")(*args)
    heads = lambda a: a.reshape(db, RWKV_HEADS, RWKV_HEAD)
    st_spec = pl.BlockSpec((tb, RWKV_HEADS, RWKV_HEAD, RWKV_HEAD), lambda i: (i, 0, 0, 0))
    hv_spec = pl.BlockSpec((tb, RWKV_HEADS, RWKV_HEAD), lambda i: (i, 0, 0))
    s_new, y = pl.pallas_call(
        functools.partial(_rwkv_sample_state_kernel, tb=tb), grid=(db // tb,),
        in_specs=[st_spec] + [hv_spec] * 6, out_specs=(st_spec, hv_spec),
        out_shape=(jax.ShapeDtypeStruct(state.shape, F32),
                   jax.ShapeDtypeStruct((db, RWKV_HEADS, RWKV_HEAD), F32)),
        compiler_params=_cparams(("parallel",)), name="rwkv_sample_state")(
            state, heads(r), heads(k2), heads(v), heads(wd), heads(av), heads(bv))
    pargs = (y.reshape(db, w), r, k2, v, pr['rk'], pr['lnw'], pr['lnb'], pr['gmat'])
    ob = pl.pallas_call(
        _rwkv_sample_post_kernel, grid=(1,), in_specs=[_full(a.shape) for a in pargs],
        out_specs=_full((db, w)), out_shape=jax.ShapeDtypeStruct((db, w), BF16),
        compiler_params=_cparams(("arbitrary",)), name="rwkv_sample_post")(*pargs)
    return ob, s_new


def _merge_kernel(oa_ref, ob_ref, oc_ref, g_ref, x_ref, woa_ref, wob_ref, woc_ref, wout_ref,
                  gffn_ref, wr_ref, br_ref, h_out, hn_out, ti_out, tg_out, *, d):
    o_a = jnp.dot(oa_ref[...], woa_ref[...], preferred_element_type=F32)
    o_b = jnp.dot(ob_ref[...], wob_ref[...], preferred_element_type=F32)
    o_c = jnp.dot(oc_ref[...], woc_ref[...], preferred_element_type=F32)
    g = g_ref[...].astype(F32)
    merged = g[:, 0:d] * o_a + g[:, d:2 * d] * o_b + g[:, 2 * d:3 * d] * o_c
    h = x_ref[...] + _dot(merged, wout_ref[...])
    h_out[...] = h
    hn = _rms(h, gffn_ref[...])
    _store_row_tiles(hn_out, hn)
    logits = _dot3(hn, wr_ref[...]) + br_ref[...]
    n_e = logits.shape[1]
    lane = lax.broadcasted_iota(jnp.int32, logits.shape, 1)
    cur = logits
    vals, idxs = [], []
    for _ in range(TOP_K):
        mx = jnp.max(cur, axis=-1, keepdims=True)
        ix = jnp.min(jnp.where(cur == mx, lane, n_e), axis=-1, keepdims=True)
        vals.append(mx)
        idxs.append(ix)
        cur = jnp.where(lane == ix, -jnp.inf, cur)
    es = [jnp.exp(vk - vals[0]) for vk in vals]
    den = es[0]
    for e in es[1:]:
        den = den + e
    lane_k = lax.broadcasted_iota(jnp.int32, ti_out.shape, 1)
    ti = jnp.zeros(ti_out.shape, jnp.int32)
    tg = jnp.zeros(tg_out.shape, F32)
    for kk in range(TOP_K):
        ti = jnp.where(lane_k == kk, idxs[kk], ti)
        tg = jnp.where(lane_k == kk, es[kk] / den, tg)
    ti_out[...] = ti
    tg_out[...] = tg


def merge_out(oa, w_oa, ob, oc, gates, x, W, tm):
    n, d = x.shape
    n_e = W['w_router'].shape[1]
    args = (oa, ob, oc, gates, x, w_oa.astype(BF16), W['w_o_rwkv'].astype(BF16), W['w_o_mem'].astype(BF16),
            W['w_out'].astype(BF16), W['g_ffn_norm'].reshape(1, d), W['w_router'], W['b_router'].reshape(1, n_e))
    row = lambda i: (i, 0)
    in_specs = [pl.BlockSpec((tm, a.shape[1]), row) for a in args[:5]] + [_full(a.shape) for a in args[5:]]
    out_shape = (jax.ShapeDtypeStruct((n, d), F32), jax.ShapeDtypeStruct((n * SUBLANES, LANES), F32),
                 jax.ShapeDtypeStruct((n, TOP_K), jnp.int32), jax.ShapeDtypeStruct((n, TOP_K), F32))
    out_specs = (pl.BlockSpec((tm, d), row), pl.BlockSpec((tm * SUBLANES, LANES), row),
                 pl.BlockSpec((tm, TOP_K), row), pl.BlockSpec((tm, TOP_K), row))
    return pl.pallas_call(
        functools.partial(_merge_kernel, d=d), grid=(n // tm,), in_specs=in_specs, out_specs=out_specs,
        out_shape=out_shape, compiler_params=_cparams(("parallel",)), name="merge_out")(*args)


def _pair_split_kernel(w_ref, p_ref, o_ref):
    p = p_ref[...]
    for t in range(w_ref.shape[2] // (2 * LANES)):
        sl = slice(2 * LANES * t, 2 * LANES * (t + 1))
        o_ref[0, :, sl] = jnp.dot(w_ref[0, :, sl].astype(BF16), p, preferred_element_type=F32).astype(BF16)


def pair_split_cast(w, tr=512):
    n_e, d, cols = w.shape
    j = np.arange(2 * LANES)
    src = np.where(j < LANES, 2 * j, 2 * (j - LANES) + 1)
    perm = jnp.asarray((np.arange(2 * LANES)[:, None] == src[None, :]).astype(np.float32), dtype=BF16)
    blk = pl.BlockSpec((1, tr, cols), lambda e, i: (e, i, 0))
    return pl.pallas_call(
        _pair_split_kernel, grid=(n_e, d // tr),
        in_specs=[blk, pl.BlockSpec(perm.shape, lambda e, i: (0, 0))], out_specs=blk,
        out_shape=jax.ShapeDtypeStruct(w.shape, BF16),
        compiler_params=_cparams(("parallel", "parallel")), name="moe_w1_prep")(w, perm)


def _gmm_kernel(be_ref, nu_ref, tok_ref, src_ref, w1_ref, b1_ref, w2_ref, b2_ref,
                y_ref, xbuf, sem, *, tm):
    i = pl.program_id(0)
    n_used = nu_ref[0]

    def row_copy(blk, r, slot):
        return pltpu.make_async_copy(src_ref.at[pl.ds(tok_ref[blk * tm + r] * SUBLANES, SUBLANES)],
                                     xbuf.at[slot, pl.ds(r * SUBLANES, SUBLANES)], sem.at[slot])

    def drain(slot):
        def step(r, carry):
            pltpu.make_async_copy(src_ref.at[pl.ds(0, SUBLANES)], xbuf.at[slot, pl.ds(0, SUBLANES)],
                                  sem.at[slot]).wait()
            return carry
        lax.fori_loop(0, tm, step, 0, unroll=8)

    @pl.when(jnp.logical_and(i == 0, n_used > 0))
    def _():
        def issue(r, carry):
            row_copy(0, r, 0).start()
            return carry
        lax.fori_loop(0, tm, issue, 0, unroll=8)

    @pl.when(i < n_used)
    def _():
        slot = i % 2
        drain(slot)
        x = _load_row_tiles(xbuf.at[slot], tm).astype(BF16)
        nxt = jnp.minimum(i + 1, n_used - 1)
        for r in range(tm):
            row_copy(nxt, r, 1 - slot).start()
        h = jnp.dot(x, w1_ref[0], preferred_element_type=F32) + b1_ref[0]
        acts = []
        for t in range(w2_ref.shape[1] // LANES):
            x_glu = jnp.minimum(h[:, 2 * LANES * t:2 * LANES * t + LANES], SWIGLU_LIMIT)
            x_lin = jnp.clip(h[:, 2 * LANES * t + LANES:2 * LANES * (t + 1)], -SWIGLU_LIMIT, SWIGLU_LIMIT)
            acts.append((x_glu * _sigmoid(SWIGLU_ALPHA * x_glu) * (x_lin + 1.0)).astype(BF16))
        y = jnp.dot(jnp.concatenate(acts, axis=1), w2_ref[0], preferred_element_type=F32) + b2_ref[0]
        _store_row_tiles(y_ref, y)

    @pl.when(i == n_used - 1)
    def _():
        drain((i + 1) % 2)

    @pl.when(i >= n_used)
    def _():
        y_ref[...] = jnp.zeros(y_ref.shape, y_ref.dtype)


def grouped_mlp(src, row_tok, block_e, n_used, W, tm):
    n_rows = row_tok.shape[0]
    d = W['w_mlp1'].shape[1]
    n_e = W['w_mlp1'].shape[0]
    d_ff = W['w_mlp2'].shape[1]
    w1 = pair_split_cast(W['w_mlp1'])
    b1 = W['b_mlp1'].reshape(n_e, d_ff // LANES, LANES, 2)
    b1 = jnp.swapaxes(b1, 2, 3).reshape(n_e, 1, 2 * d_ff)
    w2 = W['w_mlp2'].astype(BF16)
    b2 = W['b_mlp2'].reshape(n_e, 1, d)
    ex = lambda i, be, nu, tok: (be[i], 0, 0)
    grid_spec = pltpu.PrefetchScalarGridSpec(
        num_scalar_prefetch=3, grid=(n_rows // tm,),
        in_specs=[pl.BlockSpec(memory_space=pl.ANY),
                  pl.BlockSpec((1, d, 2 * d_ff), ex), pl.BlockSpec((1, 1, 2 * d_ff), ex),
                  pl.BlockSpec((1, d_ff, d), ex), pl.BlockSpec((1, 1, d), ex)],
        out_specs=pl.BlockSpec((tm * SUBLANES, LANES), lambda i, be, nu, tok: (i, 0)),
        scratch_shapes=[pltpu.VMEM((2, tm * SUBLANES, LANES), F32), pltpu.SemaphoreType.DMA((2,))])
    return pl.pallas_call(
        functools.partial(_gmm_kernel, tm=tm), grid_spec=grid_spec,
        out_shape=jax.ShapeDtypeStruct((n_rows * SUBLANES, LANES), F32),
        compiler_params=_cparams(("arbitrary",)), name="moe_gmm")(
            block_e, n_used, row_tok, src, w1, b1, w2, b2)


def _combine_kernel(dest_ref, gate_ref, h_ref, gf_ref, yb_ref, o_ref, buf, sem, *, tc):
    i = pl.program_id(0)

    def start(step, slot):
        def issue(t, carry):
            for k in range(TOP_K):
                pltpu.make_async_copy(
                    yb_ref.at[pl.ds(dest_ref[(step * tc + t) * TOP_K + k] * SUBLANES, SUBLANES)],
                    buf.at[slot, k, pl.ds(t * SUBLANES, SUBLANES)], sem.at[slot]).start()
            return carry
        lax.fori_loop(0, tc, issue, 0, unroll=4)

    @pl.when(i == 0)
    def _():
        start(0, 0)

    @pl.when(i + 1 < pl.num_programs(0))
    def _():
        start(i + 1, (i + 1) % 2)

    slot = i % 2

    def drain(t, carry):
        for k in range(TOP_K):
            pltpu.make_async_copy(yb_ref.at[pl.ds(0, SUBLANES)], buf.at[slot, k, pl.ds(0, SUBLANES)],
                                  sem.at[slot]).wait()
        return carry

    lax.fori_loop(0, tc, drain, 0, unroll=4)
    gate = gate_ref[...]
    y = h_ref[...]
    for k in range(TOP_K):
        y = y + gate[:, k:k + 1] * _load_row_tiles(buf.at[slot, k], tc)
    o_ref[...] = _rms(y, gf_ref[...])


def moe_combine(yb, dest, gate, h, g_final, tc):
    n, d = h.shape
    grid_spec = pltpu.PrefetchScalarGridSpec(
        num_scalar_prefetch=1, grid=(n // tc,),
        in_specs=[pl.BlockSpec((tc, TOP_K), lambda i, ds: (i, 0)),
                  pl.BlockSpec((tc, d), lambda i, ds: (i, 0)),
                  pl.BlockSpec((1, d), lambda i, ds: (0, 0)),
                  pl.BlockSpec(memory_space=pl.ANY)],
        out_specs=pl.BlockSpec((tc, d), lambda i, ds: (i, 0)),
        scratch_shapes=[pltpu.VMEM((2, TOP_K, tc * SUBLANES, LANES), F32), pltpu.SemaphoreType.DMA((2,))])
    return pl.pallas_call(
        functools.partial(_combine_kernel, tc=tc), grid_spec=grid_spec,
        out_shape=jax.ShapeDtypeStruct((n, d), F32),
        compiler_params=_cparams(("arbitrary",)), name="moe_combine")(dest, gate, h, g_final.reshape(1, d), yb)


def _routing(top_i, n_e, tm):
    n_tok = top_i.shape[0]
    n_assign = n_tok * TOP_K
    flat_e = top_i.reshape(-1)
    onehot = (flat_e[:, None] == jnp.arange(n_e, dtype=jnp.int32)[None, :]).astype(jnp.int32)
    csum = jnp.cumsum(onehot, axis=0)
    counts = csum[-1]
    padded = ((counts + tm - 1) // tm) * tm
    pend = jnp.cumsum(padded)
    pstart = pend - padded
    dest = jnp.sum(onehot * (csum - onehot + pstart[None, :]), axis=1).astype(jnp.int32)
    n_blocks = -(-n_assign // tm) + n_e
    n_rows = n_blocks * tm
    row_tok = jnp.zeros((n_rows,), jnp.int32).at[dest].set(jnp.arange(n_assign, dtype=jnp.int32) // TOP_K)
    block_row0 = jnp.arange(n_blocks, dtype=jnp.int32) * tm
    block_e = jnp.minimum(jnp.sum((pend[None, :] <= block_row0[:, None]).astype(jnp.int32), axis=1), n_e - 1)
    n_used = (pend[-1:] // tm).astype(jnp.int32)
    return dest, row_tok, block_e, n_used


def kernel(x_prompt, x_sample, mem_prompt, cache_kv_latent, cache_k_rope, page_table, state_rwkv, state_rwkv_shift, cache_mem_k, cache_mem_v, g_attn_norm, w_in, b_gate, g_q_norm, w_uq, g_kv_norm, w_uk, w_uv, w_o_mla, mu_shift, w0_decay, w_decay_up, a0, w_aaa_up, k_k, k_a, r_k, ln_x_w, ln_x_b, w_o_rwkv, g_mem_norm, w_mem_k, w_mem_v, w_o_mem, w_out, g_ffn_norm, w_router, b_router, w_mlp1, b_mlp1, w_mlp2, b_mlp2, g_final):
    layer_params = dict(g_attn_norm=g_attn_norm, w_in=w_in, b_gate=b_gate, g_q_norm=g_q_norm, w_uq=w_uq,
                        g_kv_norm=g_kv_norm, w_uk=w_uk, w_uv=w_uv, w_o_mla=w_o_mla, mu_shift=mu_shift,
                        w0_decay=w0_decay, w_decay_up=w_decay_up, a0=a0, w_aaa_up=w_aaa_up, k_k=k_k, k_a=k_a,
                        r_k=r_k, ln_x_w=ln_x_w, ln_x_b=ln_x_b, w_o_rwkv=w_o_rwkv, g_mem_norm=g_mem_norm,
                        w_mem_k=w_mem_k, w_mem_v=w_mem_v, w_o_mem=w_o_mem, w_out=w_out, g_ffn_norm=g_ffn_norm,
                        w_router=w_router, b_router=b_router, w_mlp1=w_mlp1, b_mlp1=b_mlp1,
                        w_mlp2=w_mlp2, b_mlp2=b_mlp2)
    depth = w_in.shape[0]
    assert depth == 1, "single-layer trunk"
    W = {name: arr.reshape(arr.shape[1:]) for name, arr in layer_params.items()}
    B, T, D = x_prompt.shape
    DB, TS, _ = x_sample.shape
    assert TS == 1, "sample group decodes one token per sequence"
    n_pages = page_table.shape[1]
    page = cache_kv_latent.shape[2]
    past_len = n_pages * page
    kv_lora = w_uk.shape[1]
    mem_tokens = mem_prompt.shape[1]
    mem_w = MEM_HEADS * MEM_HEAD
    n_e = w_router.shape[2]
    rw_w = RWKV_HEADS * RWKV_HEAD

    tm = min(256, T)
    xp = x_prompt.reshape(B * T, D)
    xs = x_sample.reshape(DB, D)

    tab_p = _rope_tables(jnp.arange(T, dtype=jnp.int32))
    ckv_p, kpe_p, q_h, k_h, v_h = mla_proj_prompt(xp, tab_p, W, tm)
    rw_p, qm_p, gates_p = in_proj_rest(xp, W, tm)
    mk_p, mv_p = mem_kv(mem_prompt.reshape(B * mem_tokens, D), W)
    tq = min(FLASH_TQ, T)
    oa_p = flash_prompt(q_h, k_h, v_h, B, tq)
    oc_p = mem_attn_prompt(qm_p, mk_p.reshape(B, mem_tokens, mem_w), mv_p.reshape(B, mem_tokens, mem_w), tm)
    ob_p, z_p = rwkv_prompt(rw_p, B, W)
    woa_p = jnp.pad(W['w_o_mla'].reshape(MLA_HEADS, V_HEAD, D),
                    ((0, 0), (0, LANES - V_HEAD), (0, 0))).reshape(MLA_HEADS * LANES, D)
    h_p, hn_p, ti_p, tg_p = merge_out(oa_p, woa_p, ob_p, oc_p, gates_p, xp, W, tm)

    tab_s = _rope_tables(jnp.full((DB,), past_len, jnp.int32))
    ckv_s, kpe_s, qlat_s, qpe_s = mla_proj_sample(xs, tab_s, W)
    rw_s, qm_s, gates_s = in_proj_rest(xs, W, DB)
    oa_s = decode_attn(jnp.transpose(qlat_s, (1, 0, 2)), jnp.transpose(qpe_s, (1, 0, 2)),
                       ckv_s.reshape(DB, 1, kv_lora), kpe_s.reshape(DB, 1, QK_ROPE),
                       cache_kv_latent, jnp.swapaxes(cache_k_rope, 2, 3), page_table,
                       W['w_uv']).reshape(DB, MLA_HEADS * V_HEAD)
    oc_s = mem_attn_sample(qm_s.reshape(DB, MEM_HEADS, MEM_HEAD),
                           cache_mem_k.reshape(DB, mem_tokens, MEM_HEADS, MEM_HEAD),
                           cache_mem_v.reshape(DB, mem_tokens, MEM_HEADS, MEM_HEAD)).reshape(DB, mem_w)
    ob_s, st_s = rwkv_sample(rw_s, state_rwkv_shift.reshape(DB, -1),
                             state_rwkv.reshape(DB, RWKV_HEADS, RWKV_HEAD, RWKV_HEAD), W)
    h_s, hn_s, ti_s, tg_s = merge_out(oa_s.astype(BF16), W['w_o_mla'], ob_s, oc_s, gates_s, xs, W, DB)

    hn_all = jnp.concatenate([hn_p, hn_s], axis=0)
    top_i = jnp.concatenate([ti_p, ti_s], axis=0)
    dest, row_tok, block_e, n_used = _routing(top_i, n_e, MOE_TM)
    yb = grouped_mlp(hn_all, row_tok, block_e, n_used, W, MOE_TM)
    n_p = B * T
    y_p = moe_combine(yb, dest[:n_p * TOP_K], tg_p, h_p, g_final, min(128, T))
    y_s = moe_combine(yb, dest[n_p * TOP_K:], tg_s, h_s, g_final, DB)

    zq = z_p.reshape(B, RWKV_HEADS // RWKV_GROUP, RWKV_GROUP, RWKV_HEAD, RWKV_GROUP, RWKV_HEAD)
    st_p = jnp.stack([zq[:, :, q, :, q, :] for q in range(RWKV_GROUP)], axis=2)
    st_p = jnp.swapaxes(st_p.reshape(B, RWKV_HEADS, RWKV_HEAD, RWKV_HEAD), -1, -2)
    lead = lambda a: a[None]
    return (y_p.reshape(B, T, D), y_s.reshape(DB, TS, D),
            lead(ckv_p.reshape(B, T, kv_lora)), lead(kpe_p.reshape(B, T, QK_ROPE)),
            lead(st_p), lead(rw_p.reshape(B, T, -1)[:, -1]),
            lead(mk_p.reshape(B, mem_tokens, MEM_HEADS, MEM_HEAD)),
            lead(mv_p.reshape(B, mem_tokens, MEM_HEADS, MEM_HEAD)),
            lead(ckv_s.reshape(DB, TS, kv_lora)), lead(kpe_s.reshape(DB, TS, QK_ROPE)),
            lead(st_s), lead(rw_s))
```

```python
import functools
import math

import numpy as np
import jax
import jax.numpy as jnp
from jax import lax
from jax.experimental import pallas as pl
from jax.experimental.pallas import tpu as pltpu

F32 = jnp.float32
BF16 = jnp.bfloat16

MLA_HEADS = 8
QK_NOPE = 64
QK_ROPE = 32
V_HEAD = 64
ROPE_THETA = 10000.0
RWKV_HEADS = 8
RWKV_HEAD = 64
DECAY_LORA = 64
AAA_LORA = 64
MEM_HEADS = 4
MEM_HEAD = 128
TOP_K = 4
SWIGLU_ALPHA = 1.702
SWIGLU_LIMIT = 7.0
RMS_EPS = 1e-6
GN_EPS = 64e-5
L2_EPS = 1e-12

LANES = 128
SUBLANES = 8
VMEM_LIMIT = 48 * 1024 * 1024

RWKV_CHUNK = 64
RWKV_GROUP = 2
PROJ_TM = 512
MOE_TM = 512
GMM_BUFS = 3
FLASH_TQ = 1024
FLASH_SUBTILES = 2


def _cparams(sem, vmem=VMEM_LIMIT):
    return pltpu.CompilerParams(dimension_semantics=sem, vmem_limit_bytes=vmem)


def _rms(x, g, eps=RMS_EPS):
    return x * lax.rsqrt(jnp.mean(x * x, axis=-1, keepdims=True) + eps) * g


def _dot(a, b):
    return jnp.dot(a.astype(BF16), b.astype(BF16), preferred_element_type=F32)


def _dot_nt(a, b):
    return lax.dot_general(a.astype(BF16), b.astype(BF16), (((1,), (1,)), ((), ())),
                           preferred_element_type=F32)


def _split(a):
    hi = a.astype(BF16)
    lo = (a - hi.astype(F32)).astype(BF16)
    return hi, lo


def _dot3(a, b):
    ah, al = _split(a)
    bh, bl = _split(b)
    return (jnp.dot(ah, bh, preferred_element_type=F32)
            + (jnp.dot(ah, bl, preferred_element_type=F32)
               + jnp.dot(al, bh, preferred_element_type=F32)))


def _dot2_exact_rhs(a, b):
    ah, al = _split(a)
    bb = b.astype(BF16)
    return jnp.dot(ah, bb, preferred_element_type=F32) + jnp.dot(al, bb, preferred_element_type=F32)


def _dot2_exact_lhs(a, b):
    aa = a.astype(BF16)
    bh, bl = _split(b)
    return jnp.dot(aa, bh, preferred_element_type=F32) + jnp.dot(aa, bl, preferred_element_type=F32)


def _store_row_tiles(ref, x):
    rows, width = x.shape
    assert width == SUBLANES * LANES
    for s in range(SUBLANES):
        ref[pl.ds(s, rows, stride=SUBLANES), :] = x[:, LANES * s:LANES * (s + 1)]


def _load_row_tiles(ref, rows):
    return jnp.concatenate([ref[pl.ds(s, rows, stride=SUBLANES), :] for s in range(SUBLANES)], axis=1)


def _sigmoid(x):
    return 1.0 / (1.0 + jnp.exp(-x))


def _mla_proj_prompt_kernel(x_ref, ga_ref, wcq_ref, gq_ref, wckv_ref, gkv_ref,
                            wkp_ref, wkpr_ref, wkpp_ref, wkprp_ref,
                            wq_ref, wqr_ref, wuk_ref, wuv_ref,
                            cos32_ref, sin32_ref, cost_ref, sint_ref,
                            ckv_out, kpe_out, q_out, k_out, v_out, *, scale):
    xn = _rms(x_ref[...], ga_ref[...]).astype(BF16)
    cqn = _rms(jnp.dot(xn, wcq_ref[...], preferred_element_type=F32), gq_ref[...]).astype(BF16)
    ckvn = _rms(jnp.dot(xn, wckv_ref[...], preferred_element_type=F32), gkv_ref[...])
    ckv_out[...] = ckvn
    kpe_out[...] = (jnp.dot(xn, wkp_ref[...], preferred_element_type=F32) * cos32_ref[...]
                    + jnp.dot(xn, wkpr_ref[...], preferred_element_type=F32) * sin32_ref[...])
    cost = cost_ref[...]
    sint = sint_ref[...]
    kpp = (jnp.dot(xn, wkpp_ref[...], preferred_element_type=F32) * cost
           + jnp.dot(xn, wkprp_ref[...], preferred_element_type=F32) * sint)
    qa = jnp.dot(cqn, wq_ref[...], preferred_element_type=F32)
    qb = jnp.dot(cqn, wqr_ref[...], preferred_element_type=F32)
    ckb = ckvn.astype(BF16)
    ka = jnp.dot(ckb, wuk_ref[...], preferred_element_type=F32)
    va = jnp.dot(ckb, wuv_ref[...], preferred_element_type=F32)
    ones_lane = lax.broadcasted_iota(jnp.int32, cost.shape, 1) == V_HEAD
    for h in range(MLA_HEADS):
        sl = slice(LANES * h, LANES * (h + 1))
        q_out[h] = ((qa[:, sl] * cost + qb[:, sl] * sint) * scale).astype(BF16)
        k_out[h] = (ka[:, sl] + kpp).astype(BF16)
        v_out[h] = jnp.where(ones_lane, 1.0, va[:, sl]).astype(BF16)


def _mla_proj_sample_kernel(x_ref, ga_ref, wcq_ref, gq_ref, wckv_ref, gkv_ref,
                            wkp_ref, wkpr_ref, wqn_ref, wukt_ref, wqp_ref, wqpr_ref,
                            cos32_ref, sin32_ref,
                            ckv_out, kpe_out, qlat_out, qpe_out, *, scale):
    xn = _rms(x_ref[...], ga_ref[...]).astype(BF16)
    cqn = _rms(jnp.dot(xn, wcq_ref[...], preferred_element_type=F32), gq_ref[...]).astype(BF16)
    ckv_out[...] = _rms(jnp.dot(xn, wckv_ref[...], preferred_element_type=F32), gkv_ref[...])
    cos32 = cos32_ref[...]
    sin32 = sin32_ref[...]
    kpe_out[...] = (jnp.dot(xn, wkp_ref[...], preferred_element_type=F32) * cos32
                    + jnp.dot(xn, wkpr_ref[...], preferred_element_type=F32) * sin32)
    qn = jnp.dot(cqn, wqn_ref[...], preferred_element_type=F32)
    for h in range(MLA_HEADS):
        sl = slice(LANES * h, LANES * (h + 1))
        qlat_out[h] = (_dot(qn[:, sl], wukt_ref[h]) * scale).astype(BF16)
        qpe_out[h] = ((jnp.dot(cqn, wqp_ref[h], preferred_element_type=F32) * cos32
                       + jnp.dot(cqn, wqpr_ref[h], preferred_element_type=F32) * sin32) * scale)


def _full(shape):
    nd = len(shape)
    return pl.BlockSpec(shape, lambda *_: (0,) * nd, pipeline_mode=pl.Buffered(1))


def _rope_tables(pos):
    half = QK_ROPE // 2
    inv = jnp.exp(-math.log(ROPE_THETA) * jnp.arange(half, dtype=F32) / half)
    ang = pos.astype(F32)[:, None] * inv[None, :]
    cos = jnp.cos(ang)
    sin = jnp.sin(ang)
    return jnp.concatenate([cos, cos], axis=1), jnp.concatenate([sin, sin], axis=1)


def _rot_cols(w):
    half = QK_ROPE // 2
    return jnp.concatenate([-w[..., half:], w[..., :half]], axis=-1)


def _pad_cols(w, lo, total=LANES):
    pad = [(0, 0)] * (w.ndim - 1) + [(lo, total - lo - w.shape[-1])]
    return jnp.pad(w, pad)


def _mla_weights(W):
    d = W['w_in'].shape[0]
    q_lora = W['w_uq'].shape[0]
    kv_lora = W['w_uk'].shape[0]
    c0, c1, c2 = q_lora, q_lora + kv_lora, q_lora + kv_lora + QK_ROPE
    w_in = W['w_in']
    wcq = w_in[:, :c0].astype(BF16)
    wckv = w_in[:, c0:c1].astype(BF16)
    wkp = w_in[:, c1:c2]
    wuq = W['w_uq'].reshape(q_lora, MLA_HEADS, QK_NOPE + QK_ROPE)
    return dict(d=d, q_lora=q_lora, kv_lora=kv_lora, wcq=wcq, wckv=wckv, wkp=wkp, wuq=wuq,
                ga=W['g_attn_norm'].reshape(1, d), gq=W['g_q_norm'].reshape(1, q_lora),
                gkv=W['g_kv_norm'].reshape(1, kv_lora))


def mla_proj_prompt(x, pos_tables, W, tm):
    n, d = x.shape
    m = _mla_weights(W)
    q_lora, kv_lora = m['q_lora'], m['kv_lora']
    cos32, sin32 = pos_tables
    t = cos32.shape[0]
    ones = jnp.ones((t, QK_NOPE), F32)
    zeros_hi = jnp.zeros((t, LANES - QK_NOPE - QK_ROPE), F32)
    cost = jnp.concatenate([ones, cos32, zeros_hi], axis=1)
    sint = jnp.concatenate([jnp.zeros((t, QK_NOPE), F32), sin32, zeros_hi], axis=1)
    wkp = m['wkp']
    wkpr = _rot_cols(wkp)
    wuq = m['wuq']
    wq = _pad_cols(wuq, 0).reshape(q_lora, MLA_HEADS * LANES).astype(BF16)
    wqr = _pad_cols(_rot_cols(wuq[..., QK_NOPE:]), QK_NOPE).reshape(q_lora, MLA_HEADS * LANES).astype(BF16)
    wuk = _pad_cols(W['w_uk'], 0).reshape(kv_lora, MLA_HEADS * LANES).astype(BF16)
    wuv = _pad_cols(W['w_uv'], 0).reshape(kv_lora, MLA_HEADS * LANES).astype(BF16)
    args = (x, m['ga'], m['wcq'], m['gq'], m['wckv'], m['gkv'],
            wkp.astype(BF16), wkpr.astype(BF16),
            _pad_cols(wkp, QK_NOPE).astype(BF16), _pad_cols(wkpr, QK_NOPE).astype(BF16),
            wq, wqr, wuk, wuv, cos32, sin32, cost, sint)
    nt = t // tm
    row = lambda i: (i, 0)
    tab = lambda i: (i % nt, 0)
    in_specs = [pl.BlockSpec((tm, d), row)] + [_full(a.shape) for a in args[1:14]] + [
        pl.BlockSpec((tm, QK_ROPE), tab), pl.BlockSpec((tm, QK_ROPE), tab),
        pl.BlockSpec((tm, LANES), tab), pl.BlockSpec((tm, LANES), tab)]
    head = pl.BlockSpec((MLA_HEADS, tm, LANES), lambda i: (0, i, 0))
    out_shape = (jax.ShapeDtypeStruct((n, kv_lora), F32), jax.ShapeDtypeStruct((n, QK_ROPE), F32),
                 jax.ShapeDtypeStruct((MLA_HEADS, n, LANES), BF16),
                 jax.ShapeDtypeStruct((MLA_HEADS, n, LANES), BF16),
                 jax.ShapeDtypeStruct((MLA_HEADS, n, LANES), BF16))
    out_specs = (pl.BlockSpec((tm, kv_lora), row), pl.BlockSpec((tm, QK_ROPE), row), head, head, head)
    scale = (QK_NOPE + QK_ROPE) ** -0.5 * math.log2(math.e)
    return pl.pallas_call(
        functools.partial(_mla_proj_prompt_kernel, scale=scale),
        grid=(n // tm,), in_specs=in_specs, out_specs=out_specs, out_shape=out_shape,
        compiler_params=_cparams(("parallel",)), name="mla_proj_prompt")(*args)


def mla_proj_sample(x, pos_tables, W):
    n, d = x.shape
    m = _mla_weights(W)
    q_lora, kv_lora = m['q_lora'], m['kv_lora']
    cos32, sin32 = pos_tables
    wkp = m['wkp']
    wuq = m['wuq']
    wqn = _pad_cols(wuq[..., :QK_NOPE], 0).reshape(q_lora, MLA_HEADS * LANES).astype(BF16)
    wukt = jnp.pad(jnp.transpose(W['w_uk'], (1, 2, 0)), ((0, 0), (0, LANES - QK_NOPE), (0, 0))).astype(BF16)
    wqp = jnp.transpose(wuq[..., QK_NOPE:], (1, 0, 2))
    args = (x, m['ga'], m['wcq'], m['gq'], m['wckv'], m['gkv'],
            wkp.astype(BF16), _rot_cols(wkp).astype(BF16), wqn, wukt,
            wqp.astype(BF16), _rot_cols(wqp).astype(BF16), cos32, sin32)
    out_shape = (jax.ShapeDtypeStruct((n, kv_lora), F32), jax.ShapeDtypeStruct((n, QK_ROPE), F32),
                 jax.ShapeDtypeStruct((MLA_HEADS, n, kv_lora), BF16),
                 jax.ShapeDtypeStruct((MLA_HEADS, n, QK_ROPE), F32))
    scale = (QK_NOPE + QK_ROPE) ** -0.5
    return pl.pallas_call(
        functools.partial(_mla_proj_sample_kernel, scale=scale),
        grid=(1,), in_specs=[_full(a.shape) for a in args],
        out_specs=tuple(_full(s.shape) for s in out_shape), out_shape=out_shape,
        compiler_params=_cparams(("arbitrary",)), name="mla_proj_sample")(*args)


def _in_proj_rest_kernel(x_ref, ga_ref, wrw_ref, wqm_ref, wg_ref, bg_ref,
                         rw_out, qm_out, g_out, *, mem_scale):
    xn = _rms(x_ref[...], ga_ref[...]).astype(BF16)
    rw_out[...] = jnp.dot(xn, wrw_ref[...], preferred_element_type=F32)
    qm_out[...] = (jnp.dot(xn, wqm_ref[...], preferred_element_type=F32) * mem_scale).astype(BF16)
    g_out[...] = _sigmoid(jnp.dot(xn, wg_ref[...], preferred_element_type=F32) + bg_ref[...]).astype(BF16)


def in_proj_rest(x, W, tm):
    n, d = x.shape
    q_lora = W['w_uq'].shape[0]
    kv_lora = W['w_uk'].shape[0]
    rw_cols = W['mu_shift'].shape[0]
    mem_w = MEM_HEADS * MEM_HEAD
    c2 = q_lora + kv_lora + QK_ROPE
    c3 = c2 + rw_cols
    c4 = c3 + mem_w
    w_in = W['w_in']
    n_gate = w_in.shape[1] - c4
    args = (x, W['g_attn_norm'].reshape(1, d), w_in[:, c2:c3].astype(BF16), w_in[:, c3:c4].astype(BF16),
            w_in[:, c4:].astype(BF16), W['b_gate'].reshape(1, n_gate))
    row = lambda i: (i, 0)
    out_shape = (jax.ShapeDtypeStruct((n, rw_cols), F32), jax.ShapeDtypeStruct((n, mem_w), BF16),
                 jax.ShapeDtypeStruct((n, n_gate), BF16))
    return pl.pallas_call(
        functools.partial(_in_proj_rest_kernel, mem_scale=MEM_HEAD ** -0.5),
        grid=(n // tm,),
        in_specs=[pl.BlockSpec((tm, d), row)] + [_full(a.shape) for a in args[1:]],
        out_specs=(pl.BlockSpec((tm, rw_cols), row), pl.BlockSpec((tm, mem_w), row),
                   pl.BlockSpec((tm, n_gate), row)),
        out_shape=out_shape, compiler_params=_cparams(("parallel",)), name="in_proj_rest")(*args)


def _mem_kv_kernel(m_ref, g_ref, wk_ref, wv_ref, k_out, v_out):
    mn = _rms(m_ref[...], g_ref[...]).astype(BF16)
    k_out[...] = jnp.dot(mn, wk_ref[...], preferred_element_type=F32)
    v_out[...] = jnp.dot(mn, wv_ref[...], preferred_element_type=F32)


def mem_kv(mem, W):
    n, d = mem.shape
    mem_w = MEM_HEADS * MEM_HEAD
    args = (mem, W['g_mem_norm'].reshape(1, d), W['w_mem_k'].astype(BF16), W['w_mem_v'].astype(BF16))
    out_shape = (jax.ShapeDtypeStruct((n, mem_w), F32), jax.ShapeDtypeStruct((n, mem_w), F32))
    return pl.pallas_call(
        _mem_kv_kernel, grid=(1,), in_specs=[_full(a.shape) for a in args],
        out_specs=(_full((n, mem_w)), _full((n, mem_w))), out_shape=out_shape,
        compiler_params=_cparams(("arbitrary",)), name="mem_kv")(*args)


def _flash_kernel(q_ref, k_ref, v_ref, o_ref, *scratch, tq, n_sub):
    qi = pl.program_id(2)
    sub = tq // n_sub
    m_scs = scratch[:n_sub]
    acc_scs = scratch[n_sub:2 * n_sub]
    s_scs = scratch[2 * n_sub:]
    for u in range(n_sub):
        m_scs[u][...] = jnp.full(m_scs[u].shape, -jnp.inf, F32)
        acc_scs[u][...] = jnp.zeros(acc_scs[u].shape, F32)

    def scores(j, slot):
        kb = k_ref[pl.ds(pl.multiple_of(j * tq, tq), tq), :]
        for u in range(n_sub):
            s_scs[slot][u] = lax.dot_general(q_ref[u * sub:(u + 1) * sub, :], kb, (((1,), (1,)), ((), ())),
                                            preferred_element_type=F32)

    def consume(j, slot, masked):
        vb = v_ref[pl.ds(pl.multiple_of(j * tq, tq), tq), :]
        for u in range(n_sub):
            s = s_scs[slot][u]
            if masked:
                rows = u * sub + lax.broadcasted_iota(jnp.int32, (sub, tq), 0)
                cols = lax.broadcasted_iota(jnp.int32, (sub, tq), 1)
                s = jnp.where(cols <= rows, s, -jnp.inf)
            m_prev = m_scs[u][...]
            m_new = jnp.maximum(m_prev, jnp.max(s, axis=-1, keepdims=True))
            alpha = jnp.exp2(m_prev - m_new)
            p = jnp.exp2(s - jnp.tile(m_new, (1, tq // LANES)))
            acc_scs[u][...] = alpha * acc_scs[u][...] + jnp.dot(p.astype(BF16), vb, preferred_element_type=F32)
            m_scs[u][...] = m_new

    scores(0, 0)

    def body(i, carry):
        scores(2 * i + 1, 1)
        consume(2 * i, 0, False)
        scores(2 * i + 2, 0)
        consume(2 * i + 1, 1, False)
        return carry

    lax.fori_loop(0, qi // 2, body, 0)

    @pl.when(qi % 2 == 1)
    def _():
        scores(qi, 1)
        consume(qi - 1, 0, False)
        consume(qi, 1, True)

    @pl.when(qi % 2 == 0)
    def _():
        consume(qi, 0, True)

    for u in range(n_sub):
        acc = acc_scs[u][...]
        o_ref[u * sub:(u + 1) * sub, :] = (acc / acc[:, V_HEAD:V_HEAD + 1]).astype(o_ref.dtype)


def flash_prompt(q, k, v, batch, tq):
    h, n, dh = q.shape
    t = n // batch
    nq = t // tq
    kv_spec = pl.BlockSpec((None, t, dh), lambda b, hh, i: (hh, b, 0))
    n_sub = FLASH_SUBTILES
    sub = tq // n_sub
    return pl.pallas_call(
        functools.partial(_flash_kernel, tq=tq, n_sub=n_sub), grid=(batch, h, nq),
        in_specs=[pl.BlockSpec((None, tq, dh), lambda b, hh, i: (hh, b * nq + i, 0)), kv_spec, kv_spec],
        out_specs=pl.BlockSpec((tq, dh), lambda b, hh, i: (b * nq + i, hh)),
        out_shape=jax.ShapeDtypeStruct((n, h * dh), BF16),
        scratch_shapes=([pltpu.VMEM((sub, LANES), F32)] * n_sub + [pltpu.VMEM((sub, dh), F32)] * n_sub
                        + [pltpu.VMEM((n_sub, sub, tq), F32)] * 2),
        compiler_params=_cparams(("parallel", "parallel", "arbitrary")), name="flash_prompt")(q, k, v)


def _decode_kernel(pt_ref, qlat_ref, qpe_ref, ckv_ref, kpe_ref, wuv_ref, lat_hbm, rope_hbm, o_ref,
                   lat_buf, rope_buf, sem, *, n_pages, page):
    b = pl.program_id(0)

    def page_copies(seq, slot, i):
        pg = pt_ref[seq, i]
        rows = pl.ds(i * page, page)
        return (pltpu.make_async_copy(lat_hbm.at[0, pg], lat_buf.at[slot, rows], sem.at[slot]),
                pltpu.make_async_copy(rope_hbm.at[0, pg], rope_buf.at[slot, :, rows], sem.at[slot]))

    def start_all(seq, slot):
        for i in range(n_pages):
            for c in page_copies(seq, slot, i):
                c.start()

    @pl.when(b == 0)
    def _():
        start_all(0, 0)

    @pl.when(b + 1 < pl.num_programs(0))
    def _():
        start_all(b + 1, (b + 1) % 2)

    slot = b % 2

    def drain(i, carry):
        for c in page_copies(b, slot, 0):
            c.wait()
        return carry

    lax.fori_loop(0, n_pages, drain, 0)

    qlat = qlat_ref[0]
    qpe = qpe_ref[0]
    kall = lat_buf[slot].astype(BF16)
    s = _dot_nt(qlat, kall) + _dot(qpe, rope_buf[slot])
    ckv = ckv_ref[0]
    kpe = kpe_ref[0]
    s_new = (jnp.sum(qlat.astype(F32) * ckv, axis=-1, keepdims=True)
             + jnp.sum(qpe * kpe, axis=-1, keepdims=True))
    m = jnp.maximum(jnp.max(s, axis=-1, keepdims=True), s_new)
    p = jnp.exp(s - m)
    p_new = jnp.exp(s_new - m)
    l = jnp.sum(p, axis=-1, keepdims=True) + p_new
    o_lat = (jnp.dot(p.astype(BF16), kall, preferred_element_type=F32) + p_new * ckv) / l
    ov = _dot(o_lat, wuv_ref[...])
    hh = lax.broadcasted_iota(jnp.int32, ov.shape, 0)
    cc = lax.broadcasted_iota(jnp.int32, ov.shape, 1) // V_HEAD
    o_ref[0] = jnp.sum(jnp.where(hh == cc, ov, 0.0), axis=0, keepdims=True)


def decode_attn(qlat, qpe, ckv, kpe, cache_lat, cache_rope, page_table, w_uv):
    db, h, kv = qlat.shape
    _, n_pool, page, _ = cache_lat.shape
    n_pages = page_table.shape[1]
    past = n_pages * page
    wuv = w_uv.reshape(kv, h * V_HEAD).astype(BF16)
    per_b = lambda shape: pl.BlockSpec((1,) + shape, lambda b, pt: (b, 0, 0))
    in_specs = [per_b((h, kv)), per_b((h, QK_ROPE)), per_b((1, kv)), per_b((1, QK_ROPE)),
                pl.BlockSpec(wuv.shape, lambda b, pt: (0, 0)),
                pl.BlockSpec(memory_space=pl.ANY), pl.BlockSpec(memory_space=pl.ANY)]
    grid_spec = pltpu.PrefetchScalarGridSpec(
        num_scalar_prefetch=1, grid=(db,), in_specs=in_specs,
        out_specs=per_b((1, h * V_HEAD)),
        scratch_shapes=[pltpu.VMEM((2, past, kv), F32), pltpu.VMEM((2, QK_ROPE, past), F32),
                        pltpu.SemaphoreType.DMA((2,))])
    return pl.pallas_call(
        functools.partial(_decode_kernel, n_pages=n_pages, page=page), grid_spec=grid_spec,
        out_shape=jax.ShapeDtypeStruct((db, 1, h * V_HEAD), F32),
        compiler_params=_cparams(("arbitrary",)), name="decode_attn")(
            page_table, qlat, qpe, ckv, kpe, wuv, cache_lat, cache_rope)


def _mem_attn_prompt_kernel(q_ref, k_ref, v_ref, o_ref):
    q = q_ref[...]
    k = k_ref[0].astype(BF16)
    v = v_ref[0].astype(BF16)
    for h in range(MEM_HEADS):
        sl = slice(MEM_HEAD * h, MEM_HEAD * (h + 1))
        s = _dot_nt(q[:, sl], k[:, sl])
        p = jnp.exp(s - jnp.max(s, axis=-1, keepdims=True))
        o = jnp.dot(p.astype(BF16), v[:, sl], preferred_element_type=F32) / jnp.sum(p, axis=-1, keepdims=True)
        o_ref[:, sl] = o.astype(o_ref.dtype)


def mem_attn_prompt(qm, mk, mv, tm):
    n, w = qm.shape
    b, m, _ = mk.shape
    nt = n // b // tm
    kv_spec = pl.BlockSpec((1, m, w), lambda i: (i // nt, 0, 0))
    return pl.pallas_call(
        _mem_attn_prompt_kernel, grid=(n // tm,),
        in_specs=[pl.BlockSpec((tm, w), lambda i: (i, 0)), kv_spec, kv_spec],
        out_specs=pl.BlockSpec((tm, w), lambda i: (i, 0)),
        out_shape=jax.ShapeDtypeStruct((n, w), BF16),
        compiler_params=_cparams(("parallel",)), name="mem_attn_prompt")(qm, mk, mv)


def _mem_attn_sample_kernel(q_ref, k_ref, v_ref, o_ref, *, tb):
    for bl in range(tb):
        q = q_ref[bl].astype(F32)
        s = jnp.sum(k_ref[bl] * q, axis=-1, keepdims=True)
        p = jnp.exp(s - jnp.max(s, axis=0, keepdims=True))
        o = jnp.sum(p * v_ref[bl], axis=0) / jnp.sum(p, axis=0)
        o_ref[bl] = o.astype(o_ref.dtype)


def mem_attn_sample(qm, mk, mv, tb=8):
    db, nh, hd = qm.shape
    m = mk.shape[1]
    kv_spec = pl.BlockSpec((tb, m, nh, hd), lambda i: (i, 0, 0, 0))
    q_spec = pl.BlockSpec((tb, nh, hd), lambda i: (i, 0, 0))
    return pl.pallas_call(
        functools.partial(_mem_attn_sample_kernel, tb=tb), grid=(db // tb,),
        in_specs=[q_spec, kv_spec, kv_spec], out_specs=q_spec,
        out_shape=jax.ShapeDtypeStruct((db, nh, hd), BF16),
        compiler_params=_cparams(("parallel",)), name="mem_attn_sample")(qm, mk, mv)


def _rwkv_prep(xr, lora_w, w0, a0, kk_w, ka_w, gmat):
    w = RWKV_HEADS * RWKV_HEAD
    r = xr[:, 0:w]
    k = xr[:, w:2 * w]
    v = xr[:, 2 * w:3 * w]
    tail = xr[:, 3 * w:3 * w + DECAY_LORA + AAA_LORA]
    lane = lax.broadcasted_iota(jnp.int32, tail.shape, 1)
    z = jnp.where(lane < DECAY_LORA, jnp.tanh(tail), tail)
    lo = _dot3(z, lora_w)
    dec = lo[:, :w] + w0
    aaa = lo[:, w:] + a0
    u = -dec
    softplus = jnp.maximum(u, 0.0) + jnp.log(1.0 + jnp.exp(-jnp.abs(u)))
    logw = -jnp.exp(-softplus - 0.5)
    a_sig = _sigmoid(aaa)
    kk = k * kk_w
    norm = jnp.sqrt(_dot2_exact_rhs(kk * kk, gmat))
    kkn = kk / jnp.maximum(norm, L2_EPS)
    k2 = k * (1.0 + (a_sig - 1.0) * ka_w)
    return r, k2, v, logw, -kkn, kkn * a_sig


def _rwkv_post(y, r, k2, v, rk, lnw, lnb, gmat):
    inv_n = 1.0 / RWKV_HEAD
    mu = _dot2_exact_rhs(y, gmat) * inv_n
    d = y - mu
    var = _dot2_exact_rhs(d * d, gmat) * inv_n
    yn = d * lax.rsqrt(var + GN_EPS) * lnw + lnb
    return yn + _dot2_exact_rhs(r * k2 * rk, gmat) * v


def _rwkv_prompt_kernel(rw_ref, mu_ref, lora_ref, w0_ref, a0_ref, kk_ref, ka_ref, rk_ref,
                        lnw_ref, lnb_ref, g_ref, out_ref, zout_ref, prev_sc, z_sc, *, chunk):
    c = chunk
    nb = rw_ref.shape[0]

    @pl.when(pl.program_id(0) == 0)
    def _():
        prev_sc[...] = jnp.zeros(prev_sc.shape, F32)
        z_sc[...] = jnp.zeros(z_sc.shape, F32)

    rw = jnp.concatenate([rw_ref[b] for b in range(nb)], axis=0)
    row = lax.broadcasted_iota(jnp.int32, rw.shape, 0)
    prev = pltpu.roll(rw, 1, 0)
    for b in range(nb):
        prev = jnp.where(row == b * c, prev_sc[b:b + 1, :], prev)
        prev_sc[b:b + 1, :] = rw[(b + 1) * c - 1:(b + 1) * c, :]
    xr = rw + (prev - rw) * mu_ref[...]
    gmat = g_ref[...]
    r, k2, v, logw, av, bv = _rwkv_prep(xr, lora_ref[...], w0_ref[...], a0_ref[...],
                                        kk_ref[...], ka_ref[...], gmat)

    ti = lax.broadcasted_iota(jnp.int32, (nb * c, nb * c), 0)
    si = lax.broadcasted_iota(jnp.int32, (nb * c, nb * c), 1)
    causal = jnp.logical_and(si <= ti, (si // c) == (ti // c))
    cum = _dot2_exact_lhs(causal.astype(F32), logw)
    clast = jnp.concatenate([jnp.broadcast_to(cum[(b + 1) * c - 1:(b + 1) * c, :], (c, cum.shape[1]))
                             for b in range(nb)], axis=0)
    e_neg = jnp.exp(-cum)
    e_rest = jnp.exp(clast - cum)
    at = av * jnp.exp(cum - logw)
    bt = bv * e_neg
    kt = k2 * e_neg
    rt = r * jnp.exp(cum)
    bg = bv * e_rest
    kg = k2 * e_rest
    glast = jnp.exp(clast)

    g = RWKV_GROUP
    gl = g * RWKV_HEAD
    gc = g * c
    n_groups = RWKV_HEADS // g
    emask = (lax.broadcasted_iota(jnp.int32, (gc, gl), 0) // c
             == lax.broadcasted_iota(jnp.int32, (gc, gl), 1) // RWKV_HEAD)
    br = lax.broadcasted_iota(jnp.int32, (gc, gc), 0)
    bc = lax.broadcasted_iota(jnp.int32, (gc, gc), 1)
    same = (br // c) == (bc // c)
    strict = jnp.logical_and(same, (br % c) > (bc % c))
    incl = jnp.logical_and(same, (br % c) >= (bc % c))
    eye_gc = (br == bc).astype(F32)
    lane_head = lax.broadcasted_iota(jnp.int32, (c, gl), 1) // RWKV_HEAD
    eye_gl = (lax.broadcasted_iota(jnp.int32, (gl, gl), 0) == lax.broadcasted_iota(jnp.int32, (gl, gl), 1))

    def dup(x):
        return jnp.concatenate([x] * g, axis=0)

    def expand(x):
        return jnp.where(emask, dup(x), 0.0)

    def per_head(x):
        return jnp.concatenate([jnp.where(lane_head == q, x, 0.0) for q in range(g)], axis=0)

    items = [(b, p) for b in range(nb) for p in range(n_groups)]

    def part(x, item):
        b, p = item
        return x[b * c:(b + 1) * c, gl * p:gl * (p + 1)]

    aps = []
    for it in items:
        lhs = jnp.concatenate([part(at, it), part(rt, it)], axis=0)
        rhs = jnp.concatenate([per_head(part(bt, it)), per_head(part(kt, it))], axis=0)
        aps.append(_dot_nt(lhs, rhs))
    nmats = [jnp.where(strict, dup(ap[0:c, 0:gc]), 0.0) for ap in aps]
    akms = [jnp.where(strict, dup(ap[0:c, gc:2 * gc]), 0.0) for ap in aps]
    rbms = [jnp.where(incl, dup(ap[c:2 * c, 0:gc]), 0.0) for ap in aps]
    rkms = [jnp.where(incl, dup(ap[c:2 * c, gc:2 * gc]), 0.0) for ap in aps]
    tmats = [eye_gc + nm for nm in nmats]
    pws = nmats
    for _ in range(int(math.log2(c)) - 1):
        pws = [_dot(pw, pw) for pw in pws]
        tmats = [tm + _dot(tm, pw) for tm, pw in zip(tmats, pws)]
    ves = [expand(part(v, it)) for it in items]
    akvs = [_dot(akm, ve) for akm, ve in zip(akms, ves)]
    wus = [_dot(tm, jnp.concatenate([expand(part(at, it)), akv], axis=1))
           for tm, it, akv in zip(tmats, items, akvs)]
    bkts = [jnp.concatenate([expand(part(bg, it)), expand(part(kg, it))], axis=0).T
            for it in items]
    gcols = [jnp.sum(jnp.where(eye_gl, part(glast, it)[0:1, :], 0.0), axis=1, keepdims=True) for it in items]
    zs = [z_sc[b, p] for b, p in items]
    ues = [_dot(wu[:, :gl], z) + wu[:, gl:] for wu, z in zip(wus, zs)]
    for n, (b, p) in enumerate(items):
        z_new = gcols[n] * zs[n] + _dot3(bkts[n], jnp.concatenate([ues[n], ves[n]], axis=0))
        z_sc[b, p] = z_new
        zout_ref[b, p] = z_new
    ys = [[None] * n_groups for _ in range(nb)]
    for n, (b, p) in enumerate(items):
        ye = _dot(jnp.concatenate([expand(part(rt, items[n])), rbms[n], rkms[n]], axis=1),
                  jnp.concatenate([zs[n], ues[n], ves[n]], axis=0))
        yp = ye[0:c, :]
        for q in range(1, g):
            yp = yp + ye[q * c:(q + 1) * c, :]
        ys[b][p] = yp
    y = jnp.concatenate([jnp.concatenate(yb, axis=1) for yb in ys], axis=0)
    out = _rwkv_post(y, r, k2, v, rk_ref[...], lnw_ref[...], lnb_ref[...], gmat)
    for b in range(nb):
        out_ref[b] = out[b * c:(b + 1) * c, :].astype(out_ref.dtype)


def _rwkv_params(W):
    w = RWKV_HEADS * RWKV_HEAD
    lora = jnp.zeros((DECAY_LORA + AAA_LORA, 2 * w), F32)
    lora = lora.at[:DECAY_LORA, :w].set(W['w_decay_up']).at[DECAY_LORA:, w:].set(W['w_aaa_up'])
    head = np.arange(w) // RWKV_HEAD
    gmat = jnp.asarray((head[:, None] == head[None, :]).astype(np.float32))
    row = lambda a: a.reshape(1, -1)
    return dict(mu=row(W['mu_shift']), lora=lora, w0=row(W['w0_decay']), a0=row(W['a0']),
                kk=row(W['k_k']), ka=row(W['k_a']), rk=row(W['r_k']), lnw=row(W['ln_x_w']),
                lnb=row(W['ln_x_b']), gmat=gmat)


def rwkv_prompt(rw, batch, W):
    n, cols = rw.shape
    t = n // batch
    c = RWKV_CHUNK
    nc = t // c
    w = RWKV_HEADS * RWKV_HEAD
    pr = _rwkv_params(W)
    args = (rw.reshape(batch, t, cols), pr['mu'], pr['lora'], pr['w0'], pr['a0'], pr['kk'], pr['ka'], pr['rk'],
            pr['lnw'], pr['lnb'], pr['gmat'])
    in_specs = [pl.BlockSpec((batch, c, cols), lambda i: (0, i, 0))] + [
        pl.BlockSpec(a.shape, lambda i: (0, 0)) for a in args[1:]]
    n_groups = RWKV_HEADS // RWKV_GROUP
    gl = RWKV_GROUP * RWKV_HEAD
    out_shape = (jax.ShapeDtypeStruct((batch, t, w), BF16),
                 jax.ShapeDtypeStruct((batch, n_groups, gl, gl), F32))
    ob, z = pl.pallas_call(
        functools.partial(_rwkv_prompt_kernel, chunk=c), grid=(nc,), in_specs=in_specs,
        out_specs=(pl.BlockSpec((batch, c, w), lambda i: (0, i, 0)),
                   pl.BlockSpec((batch, n_groups, gl, gl), lambda i: (0, 0, 0, 0))),
        out_shape=out_shape,
        scratch_shapes=[pltpu.VMEM((batch, cols), F32), pltpu.VMEM((batch, n_groups, gl, gl), F32)],
        compiler_params=_cparams(("arbitrary",)), name="rwkv_prompt")(*args)
    return ob.reshape(n, w), z


def _rwkv_sample_prep_kernel(rw_ref, sh_ref, mu_ref, lora_ref, w0_ref, a0_ref, kk_ref, ka_ref, g_ref,
                             r_out, k_out, v_out, w_out, a_out, b_out):
    rw = rw_ref[...]
    xr = rw + (sh_ref[...] - rw) * mu_ref[...]
    r, k2, v, logw, av, bv = _rwkv_prep(xr, lora_ref[...], w0_ref[...], a0_ref[...],
                                        kk_ref[...], ka_ref[...], g_ref[...])
    r_out[...] = r
    k_out[...] = k2
    v_out[...] = v
    w_out[...] = jnp.exp(logw)
    a_out[...] = av
    b_out[...] = bv


def _rwkv_sample_state_kernel(s_ref, r_ref, k_ref, v_ref, w_ref, a_ref, b_ref, s_out, y_out, *, tb):
    n = RWKV_HEAD
    eye = lax.broadcasted_iota(jnp.int32, (n, n), 0) == lax.broadcasted_iota(jnp.int32, (n, n), 1)
    for bl in range(tb):
        for h in range(RWKV_HEADS):
            s = s_ref[bl, h]
            row = lambda ref: ref[bl, h:h + 1, :]
            sa = jnp.sum(s * row(a_ref), axis=1, keepdims=True)
            vcol = jnp.sum(jnp.where(eye, row(v_ref), 0.0), axis=1, keepdims=True)
            s_new = s * row(w_ref) + sa * row(b_ref) + vcol * row(k_ref)
            s_out[bl, h] = s_new
            ycol = jnp.sum(s_new * row(r_ref), axis=1, keepdims=True)
            y_out[bl, h:h + 1, :] = jnp.sum(jnp.where(eye, ycol, 0.0), axis=0, keepdims=True)


def _rwkv_sample_post_kernel(y_ref, r_ref, k_ref, v_ref, rk_ref, lnw_ref, lnb_ref, g_ref, o_ref):
    o_ref[...] = _rwkv_post(y_ref[...], r_ref[...], k_ref[...], v_ref[...], rk_ref[...],
                            lnw_ref[...], lnb_ref[...], g_ref[...]).astype(o_ref.dtype)


def rwkv_sample(rw, shift0, state, W, tb=8):
    db, cols = rw.shape
    w = RWKV_HEADS * RWKV_HEAD
    pr = _rwkv_params(W)
    args = (rw, shift0, pr['mu'], pr['lora'], pr['w0'], pr['a0'], pr['kk'], pr['ka'], pr['gmat'])
    vec = jax.ShapeDtypeStruct((db, w), F32)
    r, k2, v, wd, av, bv = pl.pallas_call(
        _rwkv_sample_prep_kernel, grid=(1,), in_specs=[_full(a.shape) for a in args],
        out_specs=tuple(_full((db, w)) for _ in range(6)), out_shape=(vec,) * 6,
        compiler_params=_cparams(("arbitrary",)), name="rwkv_sample_prep")(*args)
    heads = lambda a: a.reshape(db, RWKV_HEADS, RWKV_HEAD)
    st_spec = pl.BlockSpec((tb, RWKV_HEADS, RWKV_HEAD, RWKV_HEAD), lambda i: (i, 0, 0, 0))
    hv_spec = pl.BlockSpec((tb, RWKV_HEADS, RWKV_HEAD), lambda i: (i, 0, 0))
    s_new, y = pl.pallas_call(
        functools.partial(_rwkv_sample_state_kernel, tb=tb), grid=(db // tb,),
        in_specs=[st_spec] + [hv_spec] * 6, out_specs=(st_spec, hv_spec),
        out_shape=(jax.ShapeDtypeStruct(state.shape, F32),
                   jax.ShapeDtypeStruct((db, RWKV_HEADS, RWKV_HEAD), F32)),
        compiler_params=_cparams(("parallel",)), name="rwkv_sample_state")(
            state, heads(r), heads(k2), heads(v), heads(wd), heads(av), heads(bv))
    pargs = (y.reshape(db, w), r, k2, v, pr['rk'], pr['lnw'], pr['lnb'], pr['gmat'])
    ob = pl.pallas_call(
        _rwkv_sample_post_kernel, grid=(1,), in_specs=[_full(a.shape) for a in pargs],
        out_specs=_full((db, w)), out_shape=jax.ShapeDtypeStruct((db, w), BF16),
        compiler_params=_cparams(("arbitrary",)), name="rwkv_sample_post")(*pargs)
    return ob, s_new


def _merge_kernel(oa_ref, ob_ref, oc_ref, g_ref, x_ref, woa_ref, wob_ref, woc_ref, wout_ref,
                  gffn_ref, wr_ref, br_ref, h_out, hn_out, ti_out, tg_out, *, d):
    o_a = jnp.dot(oa_ref[...], woa_ref[...], preferred_element_type=F32)
    o_b = jnp.dot(ob_ref[...], wob_ref[...], preferred_element_type=F32)
    o_c = jnp.dot(oc_ref[...], woc_ref[...], preferred_element_type=F32)
    g = g_ref[...].astype(F32)
    merged = g[:, 0:d] * o_a + g[:, d:2 * d] * o_b + g[:, 2 * d:3 * d] * o_c
    h = x_ref[...] + _dot(merged, wout_ref[...])
    h_out[...] = h
    hn = _rms(h, gffn_ref[...])
    _store_row_tiles(hn_out, hn)
    logits = _dot3(hn, wr_ref[...]) + br_ref[...]
    n_e = logits.shape[1]
    lane = lax.broadcasted_iota(jnp.int32, logits.shape, 1)
    cur = logits
    vals, idxs = [], []
    for _ in range(TOP_K):
        mx = jnp.max(cur, axis=-1, keepdims=True)
        ix = jnp.min(jnp.where(cur == mx, lane, n_e), axis=-1, keepdims=True)
        vals.append(mx)
        idxs.append(ix)
        cur = jnp.where(lane == ix, -jnp.inf, cur)
    es = [jnp.exp(vk - vals[0]) for vk in vals]
    den = es[0]
    for e in es[1:]:
        den = den + e
    lane_k = lax.broadcasted_iota(jnp.int32, ti_out.shape, 1)
    ti = jnp.zeros(ti_out.shape, jnp.int32)
    tg = jnp.zeros(tg_out.shape, F32)
    for kk in range(TOP_K):
        ti = jnp.where(lane_k == kk, idxs[kk], ti)
        tg = jnp.where(lane_k == kk, es[kk] / den, tg)
    ti_out[...] = ti
    tg_out[...] = tg


def merge_out(oa, w_oa, ob, oc, gates, x, W, tm):
    n, d = x.shape
    n_e = W['w_router'].shape[1]
    args = (oa, ob, oc, gates, x, w_oa.astype(BF16), W['w_o_rwkv'].astype(BF16), W['w_o_mem'].astype(BF16),
            W['w_out'].astype(BF16), W['g_ffn_norm'].reshape(1, d), W['w_router'], W['b_router'].reshape(1, n_e))
    row = lambda i: (i, 0)
    in_specs = [pl.BlockSpec((tm, a.shape[1]), row) for a in args[:5]] + [_full(a.shape) for a in args[5:]]
    out_shape = (jax.ShapeDtypeStruct((n, d), F32), jax.ShapeDtypeStruct((n * SUBLANES, LANES), F32),
                 jax.ShapeDtypeStruct((n, TOP_K), jnp.int32), jax.ShapeDtypeStruct((n, TOP_K), F32))
    out_specs = (pl.BlockSpec((tm, d), row), pl.BlockSpec((tm * SUBLANES, LANES), row),
                 pl.BlockSpec((tm, TOP_K), row), pl.BlockSpec((tm, TOP_K), row))
    return pl.pallas_call(
        functools.partial(_merge_kernel, d=d), grid=(n // tm,), in_specs=in_specs, out_specs=out_specs,
        out_shape=out_shape, compiler_params=_cparams(("parallel",)), name="merge_out")(*args)


def _pair_split_kernel(w_ref, p_ref, o_ref):
    p = p_ref[...]
    for t in range(w_ref.shape[2] // (2 * LANES)):
        sl = slice(2 * LANES * t, 2 * LANES * (t + 1))
        o_ref[0, :, sl] = jnp.dot(w_ref[0, :, sl].astype(BF16), p, preferred_element_type=F32).astype(BF16)


def pair_split_cast(w, tr=512):
    n_e, d, cols = w.shape
    j = np.arange(2 * LANES)
    src = np.where(j < LANES, 2 * j, 2 * (j - LANES) + 1)
    perm = jnp.asarray((np.arange(2 * LANES)[:, None] == src[None, :]).astype(np.float32), dtype=BF16)
    blk = pl.BlockSpec((1, tr, cols), lambda e, i: (e, i, 0))
    return pl.pallas_call(
        _pair_split_kernel, grid=(n_e, d // tr),
        in_specs=[blk, pl.BlockSpec(perm.shape, lambda e, i: (0, 0))], out_specs=blk,
        out_shape=jax.ShapeDtypeStruct(w.shape, BF16),
        compiler_params=_cparams(("parallel", "parallel")), name="moe_w1_prep")(w, perm)


def _gmm_kernel(be_ref, nu_ref, tok_ref, src_ref, w1_ref, b1_ref, w2_ref, b2_ref,
                y_ref, xbuf, sem, *, tm):
    i = pl.program_id(0)
    n_used = nu_ref[0]

    def row_copy(blk, r, slot):
        return pltpu.make_async_copy(src_ref.at[pl.ds(tok_ref[blk * tm + r] * SUBLANES, SUBLANES)],
                                     xbuf.at[slot, pl.ds(r * SUBLANES, SUBLANES)], sem.at[slot])

    def drain(slot):
        def step(r, carry):
            pltpu.make_async_copy(src_ref.at[pl.ds(0, SUBLANES)], xbuf.at[slot, pl.ds(0, SUBLANES)],
                                  sem.at[slot]).wait()
            return carry
        lax.fori_loop(0, tm, step, 0, unroll=8)

    last = n_used - 1

    @pl.when(jnp.logical_and(i == 0, n_used > 0))
    def _():
        def issue(r, carry):
            row_copy(0, r, 0).start()
            row_copy(jnp.minimum(1, last), r, 1).start()
            return carry
        lax.fori_loop(0, tm, issue, 0, unroll=8)

    @pl.when(i < n_used)
    def _():
        slot = i % GMM_BUFS
        drain(slot)
        x = _load_row_tiles(xbuf.at[slot], tm).astype(BF16)
        nxt = jnp.minimum(i + 2, last)
        nslot = (i + 2) % GMM_BUFS
        for r in range(tm):
            row_copy(nxt, r, nslot).start()
        h = jnp.dot(x, w1_ref[0], preferred_element_type=F32) + b1_ref[0]
        acts = []
        for t in range(w2_ref.shape[1] // LANES):
            x_glu = jnp.minimum(h[:, 2 * LANES * t:2 * LANES * t + LANES], SWIGLU_LIMIT)
            x_lin = jnp.clip(h[:, 2 * LANES * t + LANES:2 * LANES * (t + 1)], -SWIGLU_LIMIT, SWIGLU_LIMIT)
            acts.append((x_glu * _sigmoid(SWIGLU_ALPHA * x_glu) * (x_lin + 1.0)).astype(BF16))
        y = jnp.dot(jnp.concatenate(acts, axis=1), w2_ref[0], preferred_element_type=F32) + b2_ref[0]
        _store_row_tiles(y_ref, y)

    @pl.when(i == last)
    def _():
        drain((i + 1) % GMM_BUFS)
        drain((i + 2) % GMM_BUFS)

    @pl.when(i >= n_used)
    def _():
        y_ref[...] = jnp.zeros(y_ref.shape, y_ref.dtype)


def grouped_mlp(src, row_tok, block_e, n_used, W, tm):
    n_rows = row_tok.shape[0]
    d = W['w_mlp1'].shape[1]
    n_e = W['w_mlp1'].shape[0]
    d_ff = W['w_mlp2'].shape[1]
    w1 = pair_split_cast(W['w_mlp1'])
    b1 = W['b_mlp1'].reshape(n_e, d_ff // LANES, LANES, 2)
    b1 = jnp.swapaxes(b1, 2, 3).reshape(n_e, 1, 2 * d_ff)
    w2 = W['w_mlp2'].astype(BF16)
    b2 = W['b_mlp2'].reshape(n_e, 1, d)
    ex = lambda i, be, nu, tok: (be[i], 0, 0)
    grid_spec = pltpu.PrefetchScalarGridSpec(
        num_scalar_prefetch=3, grid=(n_rows // tm,),
        in_specs=[pl.BlockSpec(memory_space=pl.ANY),
                  pl.BlockSpec((1, d, 2 * d_ff), ex), pl.BlockSpec((1, 1, 2 * d_ff), ex),
                  pl.BlockSpec((1, d_ff, d), ex), pl.BlockSpec((1, 1, d), ex)],
        out_specs=pl.BlockSpec((tm * SUBLANES, LANES), lambda i, be, nu, tok: (i, 0)),
        scratch_shapes=[pltpu.VMEM((GMM_BUFS, tm * SUBLANES, LANES), F32), pltpu.SemaphoreType.DMA((GMM_BUFS,))])
    return pl.pallas_call(
        functools.partial(_gmm_kernel, tm=tm), grid_spec=grid_spec,
        out_shape=jax.ShapeDtypeStruct((n_rows * SUBLANES, LANES), F32),
        compiler_params=_cparams(("arbitrary",)), name="moe_gmm")(
            block_e, n_used, row_tok, src, w1, b1, w2, b2)


def _combine_kernel(dest_ref, gate_ref, h_ref, gf_ref, yb_ref, o_ref, buf, sem, *, tc):
    i = pl.program_id(0)

    def start(step, slot):
        def issue(t, carry):
            for k in range(TOP_K):
                pltpu.make_async_copy(
                    yb_ref.at[pl.ds(dest_ref[(step * tc + t) * TOP_K + k] * SUBLANES, SUBLANES)],
                    buf.at[slot, k, pl.ds(t * SUBLANES, SUBLANES)], sem.at[slot]).start()
            return carry
        lax.fori_loop(0, tc, issue, 0, unroll=4)

    @pl.when(i == 0)
    def _():
        start(0, 0)

    @pl.when(i + 1 < pl.num_programs(0))
    def _():
        start(i + 1, (i + 1) % 2)

    slot = i % 2

    def drain(t, carry):
        for k in range(TOP_K):
            pltpu.make_async_copy(yb_ref.at[pl.ds(0, SUBLANES)], buf.at[slot, k, pl.ds(0, SUBLANES)],
                                  sem.at[slot]).wait()
        return carry

    lax.fori_loop(0, tc, drain, 0, unroll=4)
    gate = gate_ref[...]
    y = h_ref[...]
    for k in range(TOP_K):
        y = y + gate[:, k:k + 1] * _load_row_tiles(buf.at[slot, k], tc)
    o_ref[...] = _rms(y, gf_ref[...])


def moe_combine(yb, dest, gate, h, g_final, tc):
    n, d = h.shape
    grid_spec = pltpu.PrefetchScalarGridSpec(
        num_scalar_prefetch=1, grid=(n // tc,),
        in_specs=[pl.BlockSpec((tc, TOP_K), lambda i, ds: (i, 0)),
                  pl.BlockSpec((tc, d), lambda i, ds: (i, 0)),
                  pl.BlockSpec((1, d), lambda i, ds: (0, 0)),
                  pl.BlockSpec(memory_space=pl.ANY)],
        out_specs=pl.BlockSpec((tc, d), lambda i, ds: (i, 0)),
        scratch_shapes=[pltpu.VMEM((2, TOP_K, tc * SUBLANES, LANES), F32), pltpu.SemaphoreType.DMA((2,))])
    return pl.pallas_call(
        functools.partial(_combine_kernel, tc=tc), grid_spec=grid_spec,
        out_shape=jax.ShapeDtypeStruct((n, d), F32),
        compiler_params=_cparams(("arbitrary",)), name="moe_combine")(dest, gate, h, g_final.reshape(1, d), yb)


def _routing(top_i, n_e, tm):
    n_tok = top_i.shape[0]
    n_assign = n_tok * TOP_K
    flat_e = top_i.reshape(-1)
    onehot = (flat_e[:, None] == jnp.arange(n_e, dtype=jnp.int32)[None, :]).astype(jnp.int32)
    csum = jnp.cumsum(onehot, axis=0)
    counts = csum[-1]
    padded = ((counts + tm - 1) // tm) * tm
    pend = jnp.cumsum(padded)
    pstart = pend - padded
    dest = jnp.sum(onehot * (csum - onehot + pstart[None, :]), axis=1).astype(jnp.int32)
    n_blocks = -(-n_assign // tm) + n_e
    n_rows = n_blocks * tm
    row_tok = jnp.zeros((n_rows,), jnp.int32).at[dest].set(jnp.arange(n_assign, dtype=jnp.int32) // TOP_K)
    block_row0 = jnp.arange(n_blocks, dtype=jnp.int32) * tm
    block_e = jnp.minimum(jnp.sum((pend[None, :] <= block_row0[:, None]).astype(jnp.int32), axis=1), n_e - 1)
    n_used = (pend[-1:] // tm).astype(jnp.int32)
    return dest, row_tok, block_e, n_used


def kernel(x_prompt, x_sample, mem_prompt, cache_kv_latent, cache_k_rope, page_table, state_rwkv, state_rwkv_shift, cache_mem_k, cache_mem_v, g_attn_norm, w_in, b_gate, g_q_norm, w_uq, g_kv_norm, w_uk, w_uv, w_o_mla, mu_shift, w0_decay, w_decay_up, a0, w_aaa_up, k_k, k_a, r_k, ln_x_w, ln_x_b, w_o_rwkv, g_mem_norm, w_mem_k, w_mem_v, w_o_mem, w_out, g_ffn_norm, w_router, b_router, w_mlp1, b_mlp1, w_mlp2, b_mlp2, g_final):
    layer_params = dict(g_attn_norm=g_attn_norm, w_in=w_in, b_gate=b_gate, g_q_norm=g_q_norm, w_uq=w_uq,
                        g_kv_norm=g_kv_norm, w_uk=w_uk, w_uv=w_uv, w_o_mla=w_o_mla, mu_shift=mu_shift,
                        w0_decay=w0_decay, w_decay_up=w_decay_up, a0=a0, w_aaa_up=w_aaa_up, k_k=k_k, k_a=k_a,
                        r_k=r_k, ln_x_w=ln_x_w, ln_x_b=ln_x_b, w_o_rwkv=w_o_rwkv, g_mem_norm=g_mem_norm,
                        w_mem_k=w_mem_k, w_mem_v=w_mem_v, w_o_mem=w_o_mem, w_out=w_out, g_ffn_norm=g_ffn_norm,
                        w_router=w_router, b_router=b_router, w_mlp1=w_mlp1, b_mlp1=b_mlp1,
                        w_mlp2=w_mlp2, b_mlp2=b_mlp2)
    depth = w_in.shape[0]
    assert depth == 1, "single-layer trunk"
    W = {name: arr.reshape(arr.shape[1:]) for name, arr in layer_params.items()}
    B, T, D = x_prompt.shape
    DB, TS, _ = x_sample.shape
    assert TS == 1, "sample group decodes one token per sequence"
    n_pages = page_table.shape[1]
    page = cache_kv_latent.shape[2]
    past_len = n_pages * page
    kv_lora = w_uk.shape[1]
    mem_tokens = mem_prompt.shape[1]
    mem_w = MEM_HEADS * MEM_HEAD
    n_e = w_router.shape[2]
    rw_w = RWKV_HEADS * RWKV_HEAD

    tm = min(PROJ_TM, T)
    xp = x_prompt.reshape(B * T, D)
    xs = x_sample.reshape(DB, D)

    tab_p = _rope_tables(jnp.arange(T, dtype=jnp.int32))
    ckv_p, kpe_p, q_h, k_h, v_h = mla_proj_prompt(xp, tab_p, W, tm)
    rw_p, qm_p, gates_p = in_proj_rest(xp, W, tm)
    mk_p, mv_p = mem_kv(mem_prompt.reshape(B * mem_tokens, D), W)
    tq = min(FLASH_TQ, T)
    oa_p = flash_prompt(q_h, k_h, v_h, B, tq)
    oc_p = mem_attn_prompt(qm_p, mk_p.reshape(B, mem_tokens, mem_w), mv_p.reshape(B, mem_tokens, mem_w), tm)
    ob_p, z_p = rwkv_prompt(rw_p, B, W)
    woa_p = jnp.pad(W['w_o_mla'].reshape(MLA_HEADS, V_HEAD, D),
                    ((0, 0), (0, LANES - V_HEAD), (0, 0))).reshape(MLA_HEADS * LANES, D)
    h_p, hn_p, ti_p, tg_p = merge_out(oa_p, woa_p, ob_p, oc_p, gates_p, xp, W, tm)

    tab_s = _rope_tables(jnp.full((DB,), past_len, jnp.int32))
    ckv_s, kpe_s, qlat_s, qpe_s = mla_proj_sample(xs, tab_s, W)
    rw_s, qm_s, gates_s = in_proj_rest(xs, W, DB)
    oa_s = decode_attn(jnp.transpose(qlat_s, (1, 0, 2)), jnp.transpose(qpe_s, (1, 0, 2)),
                       ckv_s.reshape(DB, 1, kv_lora), kpe_s.reshape(DB, 1, QK_ROPE),
                       cache_kv_latent, jnp.swapaxes(cache_k_rope, 2, 3), page_table,
                       W['w_uv']).reshape(DB, MLA_HEADS * V_HEAD)
    oc_s = mem_attn_sample(qm_s.reshape(DB, MEM_HEADS, MEM_HEAD),
                           cache_mem_k.reshape(DB, mem_tokens, MEM_HEADS, MEM_HEAD),
                           cache_mem_v.reshape(DB, mem_tokens, MEM_HEADS, MEM_HEAD)).reshape(DB, mem_w)
    ob_s, st_s = rwkv_sample(rw_s, state_rwkv_shift.reshape(DB, -1),
                             state_rwkv.reshape(DB, RWKV_HEADS, RWKV_HEAD, RWKV_HEAD), W)
    h_s, hn_s, ti_s, tg_s = merge_out(oa_s.astype(BF16), W['w_o_mla'], ob_s, oc_s, gates_s, xs, W, DB)

    hn_all = jnp.concatenate([hn_p, hn_s], axis=0)
    top_i = jnp.concatenate([ti_p, ti_s], axis=0)
    dest, row_tok, block_e, n_used = _routing(top_i, n_e, MOE_TM)
    yb = grouped_mlp(hn_all, row_tok, block_e, n_used, W, MOE_TM)
    n_p = B * T
    y_p = moe_combine(yb, dest[:n_p * TOP_K], tg_p, h_p, g_final, min(128, T))
    y_s = moe_combine(yb, dest[n_p * TOP_K:], tg_s, h_s, g_final, DB)

    zq = z_p.reshape(B, RWKV_HEADS // RWKV_GROUP, RWKV_GROUP, RWKV_HEAD, RWKV_GROUP, RWKV_HEAD)
    st_p = jnp.stack([zq[:, :, q, :, q, :] for q in range(RWKV_GROUP)], axis=2)
    st_p = jnp.swapaxes(st_p.reshape(B, RWKV_HEADS, RWKV_HEAD, RWKV_HEAD), -1, -2)
    lead = lambda a: a[None]
    return (y_p.reshape(B, T, D), y_s.reshape(DB, TS, D),
            lead(ckv_p.reshape(B, T, kv_lora)), lead(kpe_p.reshape(B, T, QK_ROPE)),
            lead(st_p), lead(rw_p.reshape(B, T, -1)[:, -1]),
            lead(mk_p.reshape(B, mem_tokens, MEM_HEADS, MEM_HEAD)),
            lead(mv_p.reshape(B, mem_tokens, MEM_HEADS, MEM_HEAD)),
            lead(ckv_s.reshape(DB, TS, kv_lora)), lead(kpe_s.reshape(DB, TS, QK_ROPE)),
            lead(st_s), lead(rw_s))
```

```python
import functools
import math

import numpy as np
import jax
import jax.numpy as jnp
from jax import lax
from jax.experimental import pallas as pl
from jax.experimental.pallas import tpu as pltpu

F32 = jnp.float32
BF16 = jnp.bfloat16

MLA_HEADS = 8
QK_NOPE = 64
QK_ROPE = 32
V_HEAD = 64
ROPE_THETA = 10000.0
RWKV_HEADS = 8
RWKV_HEAD = 64
DECAY_LORA = 64
AAA_LORA = 64
MEM_HEADS = 4
MEM_HEAD = 128
TOP_K = 4
SWIGLU_ALPHA = 1.702
SWIGLU_LIMIT = 7.0
RMS_EPS = 1e-6
GN_EPS = 64e-5
L2_EPS = 1e-12

LANES = 128
SUBLANES = 8
VMEM_LIMIT = 48 * 1024 * 1024

RWKV_CHUNK = 64
RWKV_GROUP = 2
PROJ_TM = 512
MOE_TM = 512
GMM_BUFS = 3
GMM_CHUNK_COLS = 512
FLASH_TQ = 1024
FLASH_SUBTILES = 2


def _cparams(sem, vmem=VMEM_LIMIT):
    return pltpu.CompilerParams(dimension_semantics=sem, vmem_limit_bytes=vmem)


def _rms(x, g, eps=RMS_EPS):
    return x * lax.rsqrt(jnp.mean(x * x, axis=-1, keepdims=True) + eps) * g


def _dot(a, b):
    return jnp.dot(a.astype(BF16), b.astype(BF16), preferred_element_type=F32)


def _dot_nt(a, b):
    return lax.dot_general(a.astype(BF16), b.astype(BF16), (((1,), (1,)), ((), ())),
                           preferred_element_type=F32)


def _split(a):
    hi = a.astype(BF16)
    lo = (a - hi.astype(F32)).astype(BF16)
    return hi, lo


def _dot3(a, b):
    ah, al = _split(a)
    bh, bl = _split(b)
    return (jnp.dot(ah, bh, preferred_element_type=F32)
            + (jnp.dot(ah, bl, preferred_element_type=F32)
               + jnp.dot(al, bh, preferred_element_type=F32)))


def _dot2_exact_rhs(a, b):
    ah, al = _split(a)
    bb = b.astype(BF16)
    return jnp.dot(ah, bb, preferred_element_type=F32) + jnp.dot(al, bb, preferred_element_type=F32)


def _dot2_exact_lhs(a, b):
    aa = a.astype(BF16)
    bh, bl = _split(b)
    return jnp.dot(aa, bh, preferred_element_type=F32) + jnp.dot(aa, bl, preferred_element_type=F32)


def _store_row_tiles(ref, x):
    rows, width = x.shape
    assert width == SUBLANES * LANES
    for s in range(SUBLANES):
        ref[pl.ds(s, rows, stride=SUBLANES), :] = x[:, LANES * s:LANES * (s + 1)]


def _load_row_tiles(ref, rows):
    return jnp.concatenate([ref[pl.ds(s, rows, stride=SUBLANES), :] for s in range(SUBLANES)], axis=1)


def _sigmoid(x):
    return 1.0 / (1.0 + jnp.exp(-x))


def _mla_proj_prompt_kernel(x_ref, ga_ref, wcq_ref, gq_ref, wckv_ref, gkv_ref,
                            wkp_ref, wkpr_ref, wkpp_ref, wkprp_ref,
                            wq_ref, wqr_ref, wuk_ref, wuv_ref,
                            cos32_ref, sin32_ref, cost_ref, sint_ref,
                            ckv_out, kpe_out, q_out, k_out, v_out, *, scale):
    xn = _rms(x_ref[...], ga_ref[...]).astype(BF16)
    cqn = _rms(jnp.dot(xn, wcq_ref[...], preferred_element_type=F32), gq_ref[...]).astype(BF16)
    ckvn = _rms(jnp.dot(xn, wckv_ref[...], preferred_element_type=F32), gkv_ref[...])
    ckv_out[...] = ckvn
    kpe_out[...] = (jnp.dot(xn, wkp_ref[...], preferred_element_type=F32) * cos32_ref[...]
                    + jnp.dot(xn, wkpr_ref[...], preferred_element_type=F32) * sin32_ref[...])
    cost = cost_ref[...]
    sint = sint_ref[...]
    kpp = (jnp.dot(xn, wkpp_ref[...], preferred_element_type=F32) * cost
           + jnp.dot(xn, wkprp_ref[...], preferred_element_type=F32) * sint)
    qa = jnp.dot(cqn, wq_ref[...], preferred_element_type=F32)
    qb = jnp.dot(cqn, wqr_ref[...], preferred_element_type=F32)
    ckb = ckvn.astype(BF16)
    ka = jnp.dot(ckb, wuk_ref[...], preferred_element_type=F32)
    va = jnp.dot(ckb, wuv_ref[...], preferred_element_type=F32)
    ones_lane = lax.broadcasted_iota(jnp.int32, cost.shape, 1) == V_HEAD
    for h in range(MLA_HEADS):
        sl = slice(LANES * h, LANES * (h + 1))
        q_out[h] = ((qa[:, sl] * cost + qb[:, sl] * sint) * scale).astype(BF16)
        k_out[h] = (ka[:, sl] + kpp).astype(BF16)
        v_out[h] = jnp.where(ones_lane, 1.0, va[:, sl]).astype(BF16)


def _mla_proj_sample_kernel(x_ref, ga_ref, wcq_ref, gq_ref, wckv_ref, gkv_ref,
                            wkp_ref, wkpr_ref, wqn_ref, wukt_ref, wqp_ref, wqpr_ref,
                            cos32_ref, sin32_ref,
                            ckv_out, kpe_out, qlat_out, qpe_out, *, scale):
    xn = _rms(x_ref[...], ga_ref[...]).astype(BF16)
    cqn = _rms(jnp.dot(xn, wcq_ref[...], preferred_element_type=F32), gq_ref[...]).astype(BF16)
    ckv_out[...] = _rms(jnp.dot(xn, wckv_ref[...], preferred_element_type=F32), gkv_ref[...])
    cos32 = cos32_ref[...]
    sin32 = sin32_ref[...]
    kpe_out[...] = (jnp.dot(xn, wkp_ref[...], preferred_element_type=F32) * cos32
                    + jnp.dot(xn, wkpr_ref[...], preferred_element_type=F32) * sin32)
    qn = jnp.dot(cqn, wqn_ref[...], preferred_element_type=F32)
    for h in range(MLA_HEADS):
        sl = slice(LANES * h, LANES * (h + 1))
        qlat_out[h] = (_dot(qn[:, sl], wukt_ref[h]) * scale).astype(BF16)
        qpe_out[h] = ((jnp.dot(cqn, wqp_ref[h], preferred_element_type=F32) * cos32
                       + jnp.dot(cqn, wqpr_ref[h], preferred_element_type=F32) * sin32) * scale)


def _full(shape):
    nd = len(shape)
    return pl.BlockSpec(shape, lambda *_: (0,) * nd, pipeline_mode=pl.Buffered(1))


def _rope_tables(pos):
    half = QK_ROPE // 2
    inv = jnp.exp(-math.log(ROPE_THETA) * jnp.arange(half, dtype=F32) / half)
    ang = pos.astype(F32)[:, None] * inv[None, :]
    cos = jnp.cos(ang)
    sin = jnp.sin(ang)
    return jnp.concatenate([cos, cos], axis=1), jnp.concatenate([sin, sin], axis=1)


def _rot_cols(w):
    half = QK_ROPE // 2
    return jnp.concatenate([-w[..., half:], w[..., :half]], axis=-1)


def _pad_cols(w, lo, total=LANES):
    pad = [(0, 0)] * (w.ndim - 1) + [(lo, total - lo - w.shape[-1])]
    return jnp.pad(w, pad)


def _mla_weights(W):
    d = W['w_in'].shape[0]
    q_lora = W['w_uq'].shape[0]
    kv_lora = W['w_uk'].shape[0]
    c0, c1, c2 = q_lora, q_lora + kv_lora, q_lora + kv_lora + QK_ROPE
    w_in = W['w_in']
    wcq = w_in[:, :c0].astype(BF16)
    wckv = w_in[:, c0:c1].astype(BF16)
    wkp = w_in[:, c1:c2]
    wuq = W['w_uq'].reshape(q_lora, MLA_HEADS, QK_NOPE + QK_ROPE)
    return dict(d=d, q_lora=q_lora, kv_lora=kv_lora, wcq=wcq, wckv=wckv, wkp=wkp, wuq=wuq,
                ga=W['g_attn_norm'].reshape(1, d), gq=W['g_q_norm'].reshape(1, q_lora),
                gkv=W['g_kv_norm'].reshape(1, kv_lora))


def mla_proj_prompt(x, pos_tables, W, tm):
    n, d = x.shape
    m = _mla_weights(W)
    q_lora, kv_lora = m['q_lora'], m['kv_lora']
    cos32, sin32 = pos_tables
    t = cos32.shape[0]
    ones = jnp.ones((t, QK_NOPE), F32)
    zeros_hi = jnp.zeros((t, LANES - QK_NOPE - QK_ROPE), F32)
    cost = jnp.concatenate([ones, cos32, zeros_hi], axis=1)
    sint = jnp.concatenate([jnp.zeros((t, QK_NOPE), F32), sin32, zeros_hi], axis=1)
    wkp = m['wkp']
    wkpr = _rot_cols(wkp)
    wuq = m['wuq']
    wq = _pad_cols(wuq, 0).reshape(q_lora, MLA_HEADS * LANES).astype(BF16)
    wqr = _pad_cols(_rot_cols(wuq[..., QK_NOPE:]), QK_NOPE).reshape(q_lora, MLA_HEADS * LANES).astype(BF16)
    wuk = _pad_cols(W['w_uk'], 0).reshape(kv_lora, MLA_HEADS * LANES).astype(BF16)
    wuv = _pad_cols(W['w_uv'], 0).reshape(kv_lora, MLA_HEADS * LANES).astype(BF16)
    args = (x, m['ga'], m['wcq'], m['gq'], m['wckv'], m['gkv'],
            wkp.astype(BF16), wkpr.astype(BF16),
            _pad_cols(wkp, QK_NOPE).astype(BF16), _pad_cols(wkpr, QK_NOPE).astype(BF16),
            wq, wqr, wuk, wuv, cos32, sin32, cost, sint)
    nt = t // tm
    row = lambda i: (i, 0)
    tab = lambda i: (i % nt, 0)
    in_specs = [pl.BlockSpec((tm, d), row)] + [_full(a.shape) for a in args[1:14]] + [
        pl.BlockSpec((tm, QK_ROPE), tab), pl.BlockSpec((tm, QK_ROPE), tab),
        pl.BlockSpec((tm, LANES), tab), pl.BlockSpec((tm, LANES), tab)]
    head = pl.BlockSpec((MLA_HEADS, tm, LANES), lambda i: (0, i, 0))
    out_shape = (jax.ShapeDtypeStruct((n, kv_lora), F32), jax.ShapeDtypeStruct((n, QK_ROPE), F32),
                 jax.ShapeDtypeStruct((MLA_HEADS, n, LANES), BF16),
                 jax.ShapeDtypeStruct((MLA_HEADS, n, LANES), BF16),
                 jax.ShapeDtypeStruct((MLA_HEADS, n, LANES), BF16))
    out_specs = (pl.BlockSpec((tm, kv_lora), row), pl.BlockSpec((tm, QK_ROPE), row), head, head, head)
    scale = (QK_NOPE + QK_ROPE) ** -0.5 * math.log2(math.e)
    return pl.pallas_call(
        functools.partial(_mla_proj_prompt_kernel, scale=scale),
        grid=(n // tm,), in_specs=in_specs, out_specs=out_specs, out_shape=out_shape,
        compiler_params=_cparams(("parallel",)), name="mla_proj_prompt")(*args)


def mla_proj_sample(x, pos_tables, W):
    n, d = x.shape
    m = _mla_weights(W)
    q_lora, kv_lora = m['q_lora'], m['kv_lora']
    cos32, sin32 = pos_tables
    wkp = m['wkp']
    wuq = m['wuq']
    wqn = _pad_cols(wuq[..., :QK_NOPE], 0).reshape(q_lora, MLA_HEADS * LANES).astype(BF16)
    wukt = jnp.pad(jnp.transpose(W['w_uk'], (1, 2, 0)), ((0, 0), (0, LANES - QK_NOPE), (0, 0))).astype(BF16)
    wqp = jnp.transpose(wuq[..., QK_NOPE:], (1, 0, 2))
    args = (x, m['ga'], m['wcq'], m['gq'], m['wckv'], m['gkv'],
            wkp.astype(BF16), _rot_cols(wkp).astype(BF16), wqn, wukt,
            wqp.astype(BF16), _rot_cols(wqp).astype(BF16), cos32, sin32)
    out_shape = (jax.ShapeDtypeStruct((n, kv_lora), F32), jax.ShapeDtypeStruct((n, QK_ROPE), F32),
                 jax.ShapeDtypeStruct((MLA_HEADS, n, kv_lora), BF16),
                 jax.ShapeDtypeStruct((MLA_HEADS, n, QK_ROPE), F32))
    scale = (QK_NOPE + QK_ROPE) ** -0.5
    return pl.pallas_call(
        functools.partial(_mla_proj_sample_kernel, scale=scale),
        grid=(1,), in_specs=[_full(a.shape) for a in args],
        out_specs=tuple(_full(s.shape) for s in out_shape), out_shape=out_shape,
        compiler_params=_cparams(("arbitrary",)), name="mla_proj_sample")(*args)


def _in_proj_rest_kernel(x_ref, ga_ref, wrw_ref, wqm_ref, wg_ref, bg_ref,
                         rw_out, qm_out, g_out, *, mem_scale):
    xn = _rms(x_ref[...], ga_ref[...]).astype(BF16)
    rw_out[...] = jnp.dot(xn, wrw_ref[...], preferred_element_type=F32)
    qm_out[...] = (jnp.dot(xn, wqm_ref[...], preferred_element_type=F32) * mem_scale).astype(BF16)
    g_out[...] = _sigmoid(jnp.dot(xn, wg_ref[...], preferred_element_type=F32) + bg_ref[...]).astype(BF16)


def in_proj_rest(x, W, tm):
    n, d = x.shape
    q_lora = W['w_uq'].shape[0]
    kv_lora = W['w_uk'].shape[0]
    rw_cols = W['mu_shift'].shape[0]
    mem_w = MEM_HEADS * MEM_HEAD
    c2 = q_lora + kv_lora + QK_ROPE
    c3 = c2 + rw_cols
    c4 = c3 + mem_w
    w_in = W['w_in']
    n_gate = w_in.shape[1] - c4
    args = (x, W['g_attn_norm'].reshape(1, d), w_in[:, c2:c3].astype(BF16), w_in[:, c3:c4].astype(BF16),
            w_in[:, c4:].astype(BF16), W['b_gate'].reshape(1, n_gate))
    row = lambda i: (i, 0)
    out_shape = (jax.ShapeDtypeStruct((n, rw_cols), F32), jax.ShapeDtypeStruct((n, mem_w), BF16),
                 jax.ShapeDtypeStruct((n, n_gate), BF16))
    return pl.pallas_call(
        functools.partial(_in_proj_rest_kernel, mem_scale=MEM_HEAD ** -0.5),
        grid=(n // tm,),
        in_specs=[pl.BlockSpec((tm, d), row)] + [_full(a.shape) for a in args[1:]],
        out_specs=(pl.BlockSpec((tm, rw_cols), row), pl.BlockSpec((tm, mem_w), row),
                   pl.BlockSpec((tm, n_gate), row)),
        out_shape=out_shape, compiler_params=_cparams(("parallel",)), name="in_proj_rest")(*args)


def _mem_kv_kernel(m_ref, g_ref, wk_ref, wv_ref, k_out, v_out):
    mn = _rms(m_ref[...], g_ref[...]).astype(BF16)
    k_out[...] = jnp.dot(mn, wk_ref[...], preferred_element_type=F32)
    v_out[...] = jnp.dot(mn, wv_ref[...], preferred_element_type=F32)


def mem_kv(mem, W):
    n, d = mem.shape
    mem_w = MEM_HEADS * MEM_HEAD
    args = (mem, W['g_mem_norm'].reshape(1, d), W['w_mem_k'].astype(BF16), W['w_mem_v'].astype(BF16))
    out_shape = (jax.ShapeDtypeStruct((n, mem_w), F32), jax.ShapeDtypeStruct((n, mem_w), F32))
    return pl.pallas_call(
        _mem_kv_kernel, grid=(1,), in_specs=[_full(a.shape) for a in args],
        out_specs=(_full((n, mem_w)), _full((n, mem_w))), out_shape=out_shape,
        compiler_params=_cparams(("arbitrary",)), name="mem_kv")(*args)


def _flash_kernel(q_ref, k_ref, v_ref, o_ref, *scratch, tq, n_sub):
    qi = pl.program_id(2)
    sub = tq // n_sub
    m_scs = scratch[:n_sub]
    acc_scs = scratch[n_sub:2 * n_sub]
    s_scs = scratch[2 * n_sub:]
    for u in range(n_sub):
        m_scs[u][...] = jnp.full(m_scs[u].shape, -jnp.inf, F32)
        acc_scs[u][...] = jnp.zeros(acc_scs[u].shape, F32)

    def scores(j, slot):
        kb = k_ref[pl.ds(pl.multiple_of(j * tq, tq), tq), :]
        for u in range(n_sub):
            s_scs[slot][u] = lax.dot_general(q_ref[u * sub:(u + 1) * sub, :], kb, (((1,), (1,)), ((), ())),
                                            preferred_element_type=F32)

    def consume(j, slot, masked):
        vb = v_ref[pl.ds(pl.multiple_of(j * tq, tq), tq), :]
        for u in range(n_sub):
            s = s_scs[slot][u]
            if masked:
                rows = u * sub + lax.broadcasted_iota(jnp.int32, (sub, tq), 0)
                cols = lax.broadcasted_iota(jnp.int32, (sub, tq), 1)
                s = jnp.where(cols <= rows, s, -jnp.inf)
            m_prev = m_scs[u][...]
            m_new = jnp.maximum(m_prev, jnp.max(s, axis=-1, keepdims=True))
            alpha = jnp.exp2(m_prev - m_new)
            p = jnp.exp2(s - jnp.tile(m_new, (1, tq // LANES)))
            acc_scs[u][...] = alpha * acc_scs[u][...] + jnp.dot(p.astype(BF16), vb, preferred_element_type=F32)
            m_scs[u][...] = m_new

    scores(0, 0)

    def body(i, carry):
        scores(2 * i + 1, 1)
        consume(2 * i, 0, False)
        scores(2 * i + 2, 0)
        consume(2 * i + 1, 1, False)
        return carry

    lax.fori_loop(0, qi // 2, body, 0)

    @pl.when(qi % 2 == 1)
    def _():
        scores(qi, 1)
        consume(qi - 1, 0, False)
        consume(qi, 1, True)

    @pl.when(qi % 2 == 0)
    def _():
        consume(qi, 0, True)

    for u in range(n_sub):
        acc = acc_scs[u][...]
        o_ref[u * sub:(u + 1) * sub, :] = (acc / acc[:, V_HEAD:V_HEAD + 1]).astype(o_ref.dtype)


def flash_prompt(q, k, v, batch, tq):
    h, n, dh = q.shape
    t = n // batch
    nq = t // tq
    kv_spec = pl.BlockSpec((None, t, dh), lambda b, hh, i: (hh, b, 0))
    n_sub = FLASH_SUBTILES
    sub = tq // n_sub
    return pl.pallas_call(
        functools.partial(_flash_kernel, tq=tq, n_sub=n_sub), grid=(batch, h, nq),
        in_specs=[pl.BlockSpec((None, tq, dh), lambda b, hh, i: (hh, b * nq + i, 0)), kv_spec, kv_spec],
        out_specs=pl.BlockSpec((tq, dh), lambda b, hh, i: (b * nq + i, hh)),
        out_shape=jax.ShapeDtypeStruct((n, h * dh), BF16),
        scratch_shapes=([pltpu.VMEM((sub, LANES), F32)] * n_sub + [pltpu.VMEM((sub, dh), F32)] * n_sub
                        + [pltpu.VMEM((n_sub, sub, tq), F32)] * 2),
        compiler_params=_cparams(("parallel", "parallel", "arbitrary")), name="flash_prompt")(q, k, v)


def _decode_kernel(pt_ref, qlat_ref, qpe_ref, ckv_ref, kpe_ref, wuv_ref, lat_hbm, rope_hbm, o_ref,
                   lat_buf, rope_buf, sem, *, n_pages, page):
    b = pl.program_id(0)

    def page_copies(seq, slot, i):
        pg = pt_ref[seq, i]
        rows = pl.ds(i * page, page)
        return (pltpu.make_async_copy(lat_hbm.at[0, pg], lat_buf.at[slot, rows], sem.at[slot]),
                pltpu.make_async_copy(rope_hbm.at[0, pg], rope_buf.at[slot, :, rows], sem.at[slot]))

    def start_all(seq, slot):
        for i in range(n_pages):
            for c in page_copies(seq, slot, i):
                c.start()

    @pl.when(b == 0)
    def _():
        start_all(0, 0)

    @pl.when(b + 1 < pl.num_programs(0))
    def _():
        start_all(b + 1, (b + 1) % 2)

    slot = b % 2

    def drain(i, carry):
        for c in page_copies(b, slot, 0):
            c.wait()
        return carry

    lax.fori_loop(0, n_pages, drain, 0)

    qlat = qlat_ref[0]
    qpe = qpe_ref[0]
    kall = lat_buf[slot].astype(BF16)
    s = _dot_nt(qlat, kall) + _dot(qpe, rope_buf[slot])
    ckv = ckv_ref[0]
    kpe = kpe_ref[0]
    s_new = (jnp.sum(qlat.astype(F32) * ckv, axis=-1, keepdims=True)
             + jnp.sum(qpe * kpe, axis=-1, keepdims=True))
    m = jnp.maximum(jnp.max(s, axis=-1, keepdims=True), s_new)
    p = jnp.exp(s - m)
    p_new = jnp.exp(s_new - m)
    l = jnp.sum(p, axis=-1, keepdims=True) + p_new
    o_lat = (jnp.dot(p.astype(BF16), kall, preferred_element_type=F32) + p_new * ckv) / l
    ov = _dot(o_lat, wuv_ref[...])
    hh = lax.broadcasted_iota(jnp.int32, ov.shape, 0)
    cc = lax.broadcasted_iota(jnp.int32, ov.shape, 1) // V_HEAD
    o_ref[0] = jnp.sum(jnp.where(hh == cc, ov, 0.0), axis=0, keepdims=True)


def decode_attn(qlat, qpe, ckv, kpe, cache_lat, cache_rope, page_table, w_uv):
    db, h, kv = qlat.shape
    _, n_pool, page, _ = cache_lat.shape
    n_pages = page_table.shape[1]
    past = n_pages * page
    wuv = w_uv.reshape(kv, h * V_HEAD).astype(BF16)
    per_b = lambda shape: pl.BlockSpec((1,) + shape, lambda b, pt: (b, 0, 0))
    in_specs = [per_b((h, kv)), per_b((h, QK_ROPE)), per_b((1, kv)), per_b((1, QK_ROPE)),
                pl.BlockSpec(wuv.shape, lambda b, pt: (0, 0)),
                pl.BlockSpec(memory_space=pl.ANY), pl.BlockSpec(memory_space=pl.ANY)]
    grid_spec = pltpu.PrefetchScalarGridSpec(
        num_scalar_prefetch=1, grid=(db,), in_specs=in_specs,
        out_specs=per_b((1, h * V_HEAD)),
        scratch_shapes=[pltpu.VMEM((2, past, kv), F32), pltpu.VMEM((2, QK_ROPE, past), F32),
                        pltpu.SemaphoreType.DMA((2,))])
    return pl.pallas_call(
        functools.partial(_decode_kernel, n_pages=n_pages, page=page), grid_spec=grid_spec,
        out_shape=jax.ShapeDtypeStruct((db, 1, h * V_HEAD), F32),
        compiler_params=_cparams(("arbitrary",)), name="decode_attn")(
            page_table, qlat, qpe, ckv, kpe, wuv, cache_lat, cache_rope)


def _mem_attn_prompt_kernel(q_ref, k_ref, v_ref, o_ref):
    q = q_ref[...]
    k = k_ref[0].astype(BF16)
    v = v_ref[0].astype(BF16)
    for h in range(MEM_HEADS):
        sl = slice(MEM_HEAD * h, MEM_HEAD * (h + 1))
        s = _dot_nt(q[:, sl], k[:, sl])
        p = jnp.exp(s - jnp.max(s, axis=-1, keepdims=True))
        o = jnp.dot(p.astype(BF16), v[:, sl], preferred_element_type=F32) / jnp.sum(p, axis=-1, keepdims=True)
        o_ref[:, sl] = o.astype(o_ref.dtype)


def mem_attn_prompt(qm, mk, mv, tm):
    n, w = qm.shape
    b, m, _ = mk.shape
    nt = n // b // tm
    kv_spec = pl.BlockSpec((1, m, w), lambda i: (i // nt, 0, 0))
    return pl.pallas_call(
        _mem_attn_prompt_kernel, grid=(n // tm,),
        in_specs=[pl.BlockSpec((tm, w), lambda i: (i, 0)), kv_spec, kv_spec],
        out_specs=pl.BlockSpec((tm, w), lambda i: (i, 0)),
        out_shape=jax.ShapeDtypeStruct((n, w), BF16),
        compiler_params=_cparams(("parallel",)), name="mem_attn_prompt")(qm, mk, mv)


def _mem_attn_sample_kernel(q_ref, k_ref, v_ref, o_ref, *, tb):
    for bl in range(tb):
        q = q_ref[bl].astype(F32)
        s = jnp.sum(k_ref[bl] * q, axis=-1, keepdims=True)
        p = jnp.exp(s - jnp.max(s, axis=0, keepdims=True))
        o = jnp.sum(p * v_ref[bl], axis=0) / jnp.sum(p, axis=0)
        o_ref[bl] = o.astype(o_ref.dtype)


def mem_attn_sample(qm, mk, mv, tb=8):
    db, nh, hd = qm.shape
    m = mk.shape[1]
    kv_spec = pl.BlockSpec((tb, m, nh, hd), lambda i: (i, 0, 0, 0))
    q_spec = pl.BlockSpec((tb, nh, hd), lambda i: (i, 0, 0))
    return pl.pallas_call(
        functools.partial(_mem_attn_sample_kernel, tb=tb), grid=(db // tb,),
        in_specs=[q_spec, kv_spec, kv_spec], out_specs=q_spec,
        out_shape=jax.ShapeDtypeStruct((db, nh, hd), BF16),
        compiler_params=_cparams(("parallel",)), name="mem_attn_sample")(qm, mk, mv)


def _rwkv_prep(xr, lora_w, w0, a0, kk_w, ka_w, gmat):
    w = RWKV_HEADS * RWKV_HEAD
    r = xr[:, 0:w]
    k = xr[:, w:2 * w]
    v = xr[:, 2 * w:3 * w]
    tail = xr[:, 3 * w:3 * w + DECAY_LORA + AAA_LORA]
    lane = lax.broadcasted_iota(jnp.int32, tail.shape, 1)
    z = jnp.where(lane < DECAY_LORA, jnp.tanh(tail), tail)
    lo = _dot3(z, lora_w)
    dec = lo[:, :w] + w0
    aaa = lo[:, w:] + a0
    u = -dec
    softplus = jnp.maximum(u, 0.0) + jnp.log(1.0 + jnp.exp(-jnp.abs(u)))
    logw = -jnp.exp(-softplus - 0.5)
    a_sig = _sigmoid(aaa)
    kk = k * kk_w
    norm = jnp.sqrt(_dot2_exact_rhs(kk * kk, gmat))
    kkn = kk / jnp.maximum(norm, L2_EPS)
    k2 = k * (1.0 + (a_sig - 1.0) * ka_w)
    return r, k2, v, logw, -kkn, kkn * a_sig


def _rwkv_post(y, r, k2, v, rk, lnw, lnb, gmat):
    inv_n = 1.0 / RWKV_HEAD
    mu = _dot2_exact_rhs(y, gmat) * inv_n
    d = y - mu
    var = _dot2_exact_rhs(d * d, gmat) * inv_n
    yn = d * lax.rsqrt(var + GN_EPS) * lnw + lnb
    return yn + _dot2_exact_rhs(r * k2 * rk, gmat) * v


def _rwkv_prompt_kernel(rw_ref, mu_ref, lora_ref, w0_ref, a0_ref, kk_ref, ka_ref, rk_ref,
                        lnw_ref, lnb_ref, g_ref, out_ref, zout_ref, prev_sc, z_sc, *, chunk):
    c = chunk
    nb = rw_ref.shape[0]

    @pl.when(pl.program_id(0) == 0)
    def _():
        prev_sc[...] = jnp.zeros(prev_sc.shape, F32)
        z_sc[...] = jnp.zeros(z_sc.shape, F32)

    rw = jnp.concatenate([rw_ref[b] for b in range(nb)], axis=0)
    row = lax.broadcasted_iota(jnp.int32, rw.shape, 0)
    prev = pltpu.roll(rw, 1, 0)
    for b in range(nb):
        prev = jnp.where(row == b * c, prev_sc[b:b + 1, :], prev)
        prev_sc[b:b + 1, :] = rw[(b + 1) * c - 1:(b + 1) * c, :]
    xr = rw + (prev - rw) * mu_ref[...]
    gmat = g_ref[...]
    r, k2, v, logw, av, bv = _rwkv_prep(xr, lora_ref[...], w0_ref[...], a0_ref[...],
                                        kk_ref[...], ka_ref[...], gmat)

    ti = lax.broadcasted_iota(jnp.int32, (nb * c, nb * c), 0)
    si = lax.broadcasted_iota(jnp.int32, (nb * c, nb * c), 1)
    causal = jnp.logical_and(si <= ti, (si // c) == (ti // c))
    cum = _dot2_exact_lhs(causal.astype(F32), logw)
    clast = jnp.concatenate([jnp.broadcast_to(cum[(b + 1) * c - 1:(b + 1) * c, :], (c, cum.shape[1]))
                             for b in range(nb)], axis=0)
    e_neg = jnp.exp(-cum)
    e_rest = jnp.exp(clast - cum)
    at = av * jnp.exp(cum - logw)
    bt = bv * e_neg
    kt = k2 * e_neg
    rt = r * jnp.exp(cum)
    bg = bv * e_rest
    kg = k2 * e_rest
    glast = jnp.exp(clast)

    g = RWKV_GROUP
    gl = g * RWKV_HEAD
    gc = g * c
    n_groups = RWKV_HEADS // g
    emask = (lax.broadcasted_iota(jnp.int32, (gc, gl), 0) // c
             == lax.broadcasted_iota(jnp.int32, (gc, gl), 1) // RWKV_HEAD)
    br = lax.broadcasted_iota(jnp.int32, (gc, gc), 0)
    bc = lax.broadcasted_iota(jnp.int32, (gc, gc), 1)
    same = (br // c) == (bc // c)
    strict = jnp.logical_and(same, (br % c) > (bc % c))
    incl = jnp.logical_and(same, (br % c) >= (bc % c))
    eye_gc = (br == bc).astype(F32)
    lane_head = lax.broadcasted_iota(jnp.int32, (c, gl), 1) // RWKV_HEAD
    eye_gl = (lax.broadcasted_iota(jnp.int32, (gl, gl), 0) == lax.broadcasted_iota(jnp.int32, (gl, gl), 1))

    def dup(x):
        return jnp.concatenate([x] * g, axis=0)

    def expand(x):
        return jnp.where(emask, dup(x), 0.0)

    def per_head(x):
        return jnp.concatenate([jnp.where(lane_head == q, x, 0.0) for q in range(g)], axis=0)

    items = [(b, p) for b in range(nb) for p in range(n_groups)]

    def part(x, item):
        b, p = item
        return x[b * c:(b + 1) * c, gl * p:gl * (p + 1)]

    aps = []
    for it in items:
        lhs = jnp.concatenate([part(at, it), part(rt, it)], axis=0)
        rhs = jnp.concatenate([per_head(part(bt, it)), per_head(part(kt, it))], axis=0)
        aps.append(_dot_nt(lhs, rhs))
    nmats = [jnp.where(strict, dup(ap[0:c, 0:gc]), 0.0) for ap in aps]
    akms = [jnp.where(strict, dup(ap[0:c, gc:2 * gc]), 0.0) for ap in aps]
    rbms = [jnp.where(incl, dup(ap[c:2 * c, 0:gc]), 0.0) for ap in aps]
    rkms = [jnp.where(incl, dup(ap[c:2 * c, gc:2 * gc]), 0.0) for ap in aps]
    tmats = [eye_gc + nm for nm in nmats]
    pws = nmats
    for _ in range(int(math.log2(c)) - 1):
        pws = [_dot(pw, pw) for pw in pws]
        tmats = [tm + _dot(tm, pw) for tm, pw in zip(tmats, pws)]
    ves = [expand(part(v, it)) for it in items]
    akvs = [_dot(akm, ve) for akm, ve in zip(akms, ves)]
    wus = [_dot(tm, jnp.concatenate([expand(part(at, it)), akv], axis=1))
           for tm, it, akv in zip(tmats, items, akvs)]
    bkts = [jnp.concatenate([expand(part(bg, it)), expand(part(kg, it))], axis=0).T
            for it in items]
    gcols = [jnp.sum(jnp.where(eye_gl, part(glast, it)[0:1, :], 0.0), axis=1, keepdims=True) for it in items]
    zs = [z_sc[b, p] for b, p in items]
    ues = [_dot(wu[:, :gl], z) + wu[:, gl:] for wu, z in zip(wus, zs)]
    for n, (b, p) in enumerate(items):
        z_new = gcols[n] * zs[n] + _dot3(bkts[n], jnp.concatenate([ues[n], ves[n]], axis=0))
        z_sc[b, p] = z_new
        zout_ref[b, p] = z_new
    ys = [[None] * n_groups for _ in range(nb)]
    for n, (b, p) in enumerate(items):
        ye = _dot(jnp.concatenate([expand(part(rt, items[n])), rbms[n], rkms[n]], axis=1),
                  jnp.concatenate([zs[n], ues[n], ves[n]], axis=0))
        yp = ye[0:c, :]
        for q in range(1, g):
            yp = yp + ye[q * c:(q + 1) * c, :]
        ys[b][p] = yp
    y = jnp.concatenate([jnp.concatenate(yb, axis=1) for yb in ys], axis=0)
    out = _rwkv_post(y, r, k2, v, rk_ref[...], lnw_ref[...], lnb_ref[...], gmat)
    for b in range(nb):
        out_ref[b] = out[b * c:(b + 1) * c, :].astype(out_ref.dtype)


def _rwkv_params(W):
    w = RWKV_HEADS * RWKV_HEAD
    lora = jnp.zeros((DECAY_LORA + AAA_LORA, 2 * w), F32)
    lora = lora.at[:DECAY_LORA, :w].set(W['w_decay_up']).at[DECAY_LORA:, w:].set(W['w_aaa_up'])
    head = np.arange(w) // RWKV_HEAD
    gmat = jnp.asarray((head[:, None] == head[None, :]).astype(np.float32))
    row = lambda a: a.reshape(1, -1)
    return dict(mu=row(W['mu_shift']), lora=lora, w0=row(W['w0_decay']), a0=row(W['a0']),
                kk=row(W['k_k']), ka=row(W['k_a']), rk=row(W['r_k']), lnw=row(W['ln_x_w']),
                lnb=row(W['ln_x_b']), gmat=gmat)


def rwkv_prompt(rw, batch, W):
    n, cols = rw.shape
    t = n // batch
    c = RWKV_CHUNK
    nc = t // c
    w = RWKV_HEADS * RWKV_HEAD
    pr = _rwkv_params(W)
    args = (rw.reshape(batch, t, cols), pr['mu'], pr['lora'], pr['w0'], pr['a0'], pr['kk'], pr['ka'], pr['rk'],
            pr['lnw'], pr['lnb'], pr['gmat'])
    in_specs = [pl.BlockSpec((batch, c, cols), lambda i: (0, i, 0))] + [
        pl.BlockSpec(a.shape, lambda i: (0, 0)) for a in args[1:]]
    n_groups = RWKV_HEADS // RWKV_GROUP
    gl = RWKV_GROUP * RWKV_HEAD
    out_shape = (jax.ShapeDtypeStruct((batch, t, w), BF16),
                 jax.ShapeDtypeStruct((batch, n_groups, gl, gl), F32))
    ob, z = pl.pallas_call(
        functools.partial(_rwkv_prompt_kernel, chunk=c), grid=(nc,), in_specs=in_specs,
        out_specs=(pl.BlockSpec((batch, c, w), lambda i: (0, i, 0)),
                   pl.BlockSpec((batch, n_groups, gl, gl), lambda i: (0, 0, 0, 0))),
        out_shape=out_shape,
        scratch_shapes=[pltpu.VMEM((batch, cols), F32), pltpu.VMEM((batch, n_groups, gl, gl), F32)],
        compiler_params=_cparams(("arbitrary",)), name="rwkv_prompt")(*args)
    return ob.reshape(n, w), z


def _rwkv_sample_prep_kernel(rw_ref, sh_ref, mu_ref, lora_ref, w0_ref, a0_ref, kk_ref, ka_ref, g_ref,
                             r_out, k_out, v_out, w_out, a_out, b_out):
    rw = rw_ref[...]
    xr = rw + (sh_ref[...] - rw) * mu_ref[...]
    r, k2, v, logw, av, bv = _rwkv_prep(xr, lora_ref[...], w0_ref[...], a0_ref[...],
                                        kk_ref[...], ka_ref[...], g_ref[...])
    r_out[...] = r
    k_out[...] = k2
    v_out[...] = v
    w_out[...] = jnp.exp(logw)
    a_out[...] = av
    b_out[...] = bv


def _rwkv_sample_state_kernel(s_ref, r_ref, k_ref, v_ref, w_ref, a_ref, b_ref, s_out, y_out, *, tb):
    n = RWKV_HEAD
    eye = lax.broadcasted_iota(jnp.int32, (n, n), 0) == lax.broadcasted_iota(jnp.int32, (n, n), 1)
    for bl in range(tb):
        for h in range(RWKV_HEADS):
            s = s_ref[bl, h]
            row = lambda ref: ref[bl, h:h + 1, :]
            sa = jnp.sum(s * row(a_ref), axis=1, keepdims=True)
            vcol = jnp.sum(jnp.where(eye, row(v_ref), 0.0), axis=1, keepdims=True)
            s_new = s * row(w_ref) + sa * row(b_ref) + vcol * row(k_ref)
            s_out[bl, h] = s_new
            ycol = jnp.sum(s_new * row(r_ref), axis=1, keepdims=True)
            y_out[bl, h:h + 1, :] = jnp.sum(jnp.where(eye, ycol, 0.0), axis=0, keepdims=True)


def _rwkv_sample_post_kernel(y_ref, r_ref, k_ref, v_ref, rk_ref, lnw_ref, lnb_ref, g_ref, o_ref):
    o_ref[...] = _rwkv_post(y_ref[...], r_ref[...], k_ref[...], v_ref[...], rk_ref[...],
                            lnw_ref[...], lnb_ref[...], g_ref[...]).astype(o_ref.dtype)


def rwkv_sample(rw, shift0, state, W, tb=8):
    db, cols = rw.shape
    w = RWKV_HEADS * RWKV_HEAD
    pr = _rwkv_params(W)
    args = (rw, shift0, pr['mu'], pr['lora'], pr['w0'], pr['a0'], pr['kk'], pr['ka'], pr['gmat'])
    vec = jax.ShapeDtypeStruct((db, w), F32)
    r, k2, v, wd, av, bv = pl.pallas_call(
        _rwkv_sample_prep_kernel, grid=(1,), in_specs=[_full(a.shape) for a in args],
        out_specs=tuple(_full((db, w)) for _ in range(6)), out_shape=(vec,) * 6,
        compiler_params=_cparams(("arbitrary",)), name="rwkv_sample_prep")(*args)
    heads = lambda a: a.reshape(db, RWKV_HEADS, RWKV_HEAD)
    st_spec = pl.BlockSpec((tb, RWKV_HEADS, RWKV_HEAD, RWKV_HEAD), lambda i: (i, 0, 0, 0))
    hv_spec = pl.BlockSpec((tb, RWKV_HEADS, RWKV_HEAD), lambda i: (i, 0, 0))
    s_new, y = pl.pallas_call(
        functools.partial(_rwkv_sample_state_kernel, tb=tb), grid=(db // tb,),
        in_specs=[st_spec] + [hv_spec] * 6, out_specs=(st_spec, hv_spec),
        out_shape=(jax.ShapeDtypeStruct(state.shape, F32),
                   jax.ShapeDtypeStruct((db, RWKV_HEADS, RWKV_HEAD), F32)),
        compiler_params=_cparams(("parallel",)), name="rwkv_sample_state")(
            state, heads(r), heads(k2), heads(v), heads(wd), heads(av), heads(bv))
    pargs = (y.reshape(db, w), r, k2, v, pr['rk'], pr['lnw'], pr['lnb'], pr['gmat'])
    ob = pl.pallas_call(
        _rwkv_sample_post_kernel, grid=(1,), in_specs=[_full(a.shape) for a in pargs],
        out_specs=_full((db, w)), out_shape=jax.ShapeDtypeStruct((db, w), BF16),
        compiler_params=_cparams(("arbitrary",)), name="rwkv_sample_post")(*pargs)
    return ob, s_new


def _merge_kernel(oa_ref, ob_ref, oc_ref, g_ref, x_ref, woa_ref, wob_ref, woc_ref, wout_ref,
                  gffn_ref, wr_ref, br_ref, h_out, hn_out, ti_out, tg_out, *, d):
    o_a = jnp.dot(oa_ref[...], woa_ref[...], preferred_element_type=F32)
    o_b = jnp.dot(ob_ref[...], wob_ref[...], preferred_element_type=F32)
    o_c = jnp.dot(oc_ref[...], woc_ref[...], preferred_element_type=F32)
    g = g_ref[...].astype(F32)
    merged = g[:, 0:d] * o_a + g[:, d:2 * d] * o_b + g[:, 2 * d:3 * d] * o_c
    h = x_ref[...] + _dot(merged, wout_ref[...])
    h_out[...] = h
    hn = _rms(h, gffn_ref[...])
    _store_row_tiles(hn_out, hn)
    logits = _dot3(hn, wr_ref[...]) + br_ref[...]
    n_e = logits.shape[1]
    lane = lax.broadcasted_iota(jnp.int32, logits.shape, 1)
    cur = logits
    vals, idxs = [], []
    for _ in range(TOP_K):
        mx = jnp.max(cur, axis=-1, keepdims=True)
        ix = jnp.min(jnp.where(cur == mx, lane, n_e), axis=-1, keepdims=True)
        vals.append(mx)
        idxs.append(ix)
        cur = jnp.where(lane == ix, -jnp.inf, cur)
    es = [jnp.exp(vk - vals[0]) for vk in vals]
    den = es[0]
    for e in es[1:]:
        den = den + e
    lane_k = lax.broadcasted_iota(jnp.int32, ti_out.shape, 1)
    ti = jnp.zeros(ti_out.shape, jnp.int32)
    tg = jnp.zeros(tg_out.shape, F32)
    for kk in range(TOP_K):
        ti = jnp.where(lane_k == kk, idxs[kk], ti)
        tg = jnp.where(lane_k == kk, es[kk] / den, tg)
    ti_out[...] = ti
    tg_out[...] = tg


def merge_out(oa, w_oa, ob, oc, gates, x, W, tm):
    n, d = x.shape
    n_e = W['w_router'].shape[1]
    args = (oa, ob, oc, gates, x, w_oa.astype(BF16), W['w_o_rwkv'].astype(BF16), W['w_o_mem'].astype(BF16),
            W['w_out'].astype(BF16), W['g_ffn_norm'].reshape(1, d), W['w_router'], W['b_router'].reshape(1, n_e))
    row = lambda i: (i, 0)
    in_specs = [pl.BlockSpec((tm, a.shape[1]), row) for a in args[:5]] + [_full(a.shape) for a in args[5:]]
    out_shape = (jax.ShapeDtypeStruct((n, d), F32), jax.ShapeDtypeStruct((n * SUBLANES, LANES), F32),
                 jax.ShapeDtypeStruct((n, TOP_K), jnp.int32), jax.ShapeDtypeStruct((n, TOP_K), F32))
    out_specs = (pl.BlockSpec((tm, d), row), pl.BlockSpec((tm * SUBLANES, LANES), row),
                 pl.BlockSpec((tm, TOP_K), row), pl.BlockSpec((tm, TOP_K), row))
    return pl.pallas_call(
        functools.partial(_merge_kernel, d=d), grid=(n // tm,), in_specs=in_specs, out_specs=out_specs,
        out_shape=out_shape, compiler_params=_cparams(("parallel",)), name="merge_out")(*args)


def _pair_split_kernel(w_ref, p_ref, o_ref):
    p = p_ref[...]
    cw = o_ref.shape[3]
    for t in range(w_ref.shape[2] // (2 * LANES)):
        sl = slice(2 * LANES * t, 2 * LANES * (t + 1))
        piece = jnp.dot(w_ref[0, :, sl].astype(BF16), p, preferred_element_type=F32).astype(BF16)
        c, off = divmod(2 * LANES * t, cw)
        o_ref[0, c, :, off:off + 2 * LANES] = piece


def pair_split_cast(w, tr=512):
    n_e, d, cols = w.shape
    cw = GMM_CHUNK_COLS
    j = np.arange(2 * LANES)
    src = np.where(j < LANES, 2 * j, 2 * (j - LANES) + 1)
    perm = jnp.asarray((np.arange(2 * LANES)[:, None] == src[None, :]).astype(np.float32), dtype=BF16)
    return pl.pallas_call(
        _pair_split_kernel, grid=(n_e, d // tr),
        in_specs=[pl.BlockSpec((1, tr, cols), lambda e, i: (e, i, 0)),
                  pl.BlockSpec(perm.shape, lambda e, i: (0, 0))],
        out_specs=pl.BlockSpec((1, cols // cw, tr, cw), lambda e, i: (e, 0, i, 0)),
        out_shape=jax.ShapeDtypeStruct((n_e, cols // cw, d, cw), BF16),
        compiler_params=_cparams(("parallel", "parallel")), name="moe_w1_prep")(w, perm)


def _gmm_kernel(be_ref, nu_ref, tok_ref, src_ref, w1_ref, b1_ref, w2_ref, b2_ref,
                y_ref, xbuf, x_sc, y_sc, sem, *, tm):
    i = pl.program_id(0)
    n_used = nu_ref[0]

    def row_copy(blk, r, slot):
        return pltpu.make_async_copy(src_ref.at[pl.ds(tok_ref[blk * tm + r] * SUBLANES, SUBLANES)],
                                     xbuf.at[slot, pl.ds(r * SUBLANES, SUBLANES)], sem.at[slot])

    def drain(slot):
        def step(r, carry):
            pltpu.make_async_copy(src_ref.at[pl.ds(0, SUBLANES)], xbuf.at[slot, pl.ds(0, SUBLANES)],
                                  sem.at[slot]).wait()
            return carry
        lax.fori_loop(0, tm, step, 0, unroll=8)

    last = n_used - 1

    @pl.when(jnp.logical_and(i == 0, n_used > 0))
    def _():
        def issue(r, carry):
            row_copy(0, r, 0).start(priority=0)
            row_copy(jnp.minimum(1, last), r, 1).start(priority=1)
            return carry
        lax.fori_loop(0, tm, issue, 0, unroll=8)

    @pl.when(i < n_used)
    def _():
        slot = i % GMM_BUFS
        drain(slot)
        x_sc[...] = _load_row_tiles(xbuf.at[slot], tm).astype(BF16)
        y_sc[...] = jnp.broadcast_to(b2_ref[0], y_sc.shape)
        nxt = jnp.minimum(i + 2, last)
        nslot = (i + 2) % GMM_BUFS
        n_chunks = w1_ref.shape[1]
        rows_per_chunk = tm // n_chunks

        def chunk(c, carry):
            for j in range(rows_per_chunk):
                row_copy(nxt, c * rows_per_chunk + j, nslot).start(priority=j % 2)
            h = jnp.dot(x_sc[...], w1_ref[0, c], preferred_element_type=F32) + b1_ref[0, c]
            acts = []
            for t in range(h.shape[1] // (2 * LANES)):
                x_glu = jnp.minimum(h[:, 2 * LANES * t:2 * LANES * t + LANES], SWIGLU_LIMIT)
                x_lin = jnp.clip(h[:, 2 * LANES * t + LANES:2 * LANES * (t + 1)], -SWIGLU_LIMIT, SWIGLU_LIMIT)
                acts.append((x_glu * _sigmoid(SWIGLU_ALPHA * x_glu) * (x_lin + 1.0)).astype(BF16))
            y_sc[...] += jnp.dot(jnp.concatenate(acts, axis=1), w2_ref[0, c], preferred_element_type=F32)
            return carry

        lax.fori_loop(0, n_chunks, chunk, 0)
        _store_row_tiles(y_ref, y_sc[...])

    @pl.when(i == last)
    def _():
        drain((i + 1) % GMM_BUFS)
        drain((i + 2) % GMM_BUFS)

    @pl.when(i >= n_used)
    def _():
        y_ref[...] = jnp.zeros(y_ref.shape, y_ref.dtype)


def grouped_mlp(src, row_tok, block_e, n_used, W, tm):
    n_rows = row_tok.shape[0]
    d = W['w_mlp1'].shape[1]
    n_e = W['w_mlp1'].shape[0]
    d_ff = W['w_mlp2'].shape[1]
    cw = GMM_CHUNK_COLS
    n_chunks = 2 * d_ff // cw
    w1 = pair_split_cast(W['w_mlp1'])
    b1 = W['b_mlp1'].reshape(n_e, d_ff // LANES, LANES, 2)
    b1 = jnp.swapaxes(b1, 2, 3).reshape(n_e, n_chunks, 1, cw)
    w2 = W['w_mlp2'].astype(BF16).reshape(n_e, n_chunks, d_ff // n_chunks, d)
    b2 = W['b_mlp2'].reshape(n_e, 1, d)
    ex3 = lambda i, be, nu, tok: (be[i], 0, 0)
    ex4 = lambda i, be, nu, tok: (be[i], 0, 0, 0)
    grid_spec = pltpu.PrefetchScalarGridSpec(
        num_scalar_prefetch=3, grid=(n_rows // tm,),
        in_specs=[pl.BlockSpec(memory_space=pl.ANY),
                  pl.BlockSpec((1, n_chunks, d, cw), ex4), pl.BlockSpec((1, n_chunks, 1, cw), ex4),
                  pl.BlockSpec((1, n_chunks, d_ff // n_chunks, d), ex4), pl.BlockSpec((1, 1, d), ex3)],
        out_specs=pl.BlockSpec((tm * SUBLANES, LANES), lambda i, be, nu, tok: (i, 0)),
        scratch_shapes=[pltpu.VMEM((GMM_BUFS, tm * SUBLANES, LANES), F32), pltpu.VMEM((tm, d), BF16),
                        pltpu.VMEM((tm, d), F32), pltpu.SemaphoreType.DMA((GMM_BUFS,))])
    return pl.pallas_call(
        functools.partial(_gmm_kernel, tm=tm), grid_spec=grid_spec,
        out_shape=jax.ShapeDtypeStruct((n_rows * SUBLANES, LANES), F32),
        compiler_params=_cparams(("arbitrary",)), name="moe_gmm")(
            block_e, n_used, row_tok, src, w1, b1, w2, b2)


def _combine_kernel(dest_ref, gate_ref, h_ref, gf_ref, yb_ref, o_ref, buf, sem, *, tc):
    i = pl.program_id(0)

    def start(step, slot):
        def issue(t, carry):
            for k in range(TOP_K):
                pltpu.make_async_copy(
                    yb_ref.at[pl.ds(dest_ref[(step * tc + t) * TOP_K + k] * SUBLANES, SUBLANES)],
                    buf.at[slot, k, pl.ds(t * SUBLANES, SUBLANES)], sem.at[slot]).start(priority=k % 2)
            return carry
        lax.fori_loop(0, tc, issue, 0, unroll=4)

    @pl.when(i == 0)
    def _():
        start(0, 0)

    @pl.when(i + 1 < pl.num_programs(0))
    def _():
        start(i + 1, (i + 1) % 2)

    slot = i % 2

    def drain(t, carry):
        for k in range(TOP_K):
            pltpu.make_async_copy(yb_ref.at[pl.ds(0, SUBLANES)], buf.at[slot, k, pl.ds(0, SUBLANES)],
                                  sem.at[slot]).wait()
        return carry

    lax.fori_loop(0, tc, drain, 0, unroll=4)
    gate = gate_ref[...]
    y = h_ref[...]
    for k in range(TOP_K):
        y = y + gate[:, k:k + 1] * _load_row_tiles(buf.at[slot, k], tc)
    o_ref[...] = _rms(y, gf_ref[...])


def moe_combine(yb, dest, gate, h, g_final, tc):
    n, d = h.shape
    grid_spec = pltpu.PrefetchScalarGridSpec(
        num_scalar_prefetch=1, grid=(n // tc,),
        in_specs=[pl.BlockSpec((tc, TOP_K), lambda i, ds: (i, 0)),
                  pl.BlockSpec((tc, d), lambda i, ds: (i, 0)),
                  pl.BlockSpec((1, d), lambda i, ds: (0, 0)),
                  pl.BlockSpec(memory_space=pl.ANY)],
        out_specs=pl.BlockSpec((tc, d), lambda i, ds: (i, 0)),
        scratch_shapes=[pltpu.VMEM((2, TOP_K, tc * SUBLANES, LANES), F32), pltpu.SemaphoreType.DMA((2,))])
    return pl.pallas_call(
        functools.partial(_combine_kernel, tc=tc), grid_spec=grid_spec,
        out_shape=jax.ShapeDtypeStruct((n, d), F32),
        compiler_params=_cparams(("arbitrary",)), name="moe_combine")(dest, gate, h, g_final.reshape(1, d), yb)


def _routing(top_i, n_e, tm):
    n_tok = top_i.shape[0]
    n_assign = n_tok * TOP_K
    flat_e = top_i.reshape(-1)
    onehot = (flat_e[:, None] == jnp.arange(n_e, dtype=jnp.int32)[None, :]).astype(jnp.int32)
    csum = jnp.cumsum(onehot, axis=0)
    counts = csum[-1]
    padded = ((counts + tm - 1) // tm) * tm
    pend = jnp.cumsum(padded)
    pstart = pend - padded
    dest = jnp.sum(onehot * (csum - onehot + pstart[None, :]), axis=1).astype(jnp.int32)
    n_blocks = -(-n_assign // tm) + n_e
    n_rows = n_blocks * tm
    row_tok = jnp.zeros((n_rows,), jnp.int32).at[dest].set(jnp.arange(n_assign, dtype=jnp.int32) // TOP_K)
    block_row0 = jnp.arange(n_blocks, dtype=jnp.int32) * tm
    block_e = jnp.minimum(jnp.sum((pend[None, :] <= block_row0[:, None]).astype(jnp.int32), axis=1), n_e - 1)
    n_used = (pend[-1:] // tm).astype(jnp.int32)
    return dest, row_tok, block_e, n_used


def kernel(x_prompt, x_sample, mem_prompt, cache_kv_latent, cache_k_rope, page_table, state_rwkv, state_rwkv_shift, cache_mem_k, cache_mem_v, g_attn_norm, w_in, b_gate, g_q_norm, w_uq, g_kv_norm, w_uk, w_uv, w_o_mla, mu_shift, w0_decay, w_decay_up, a0, w_aaa_up, k_k, k_a, r_k, ln_x_w, ln_x_b, w_o_rwkv, g_mem_norm, w_mem_k, w_mem_v, w_o_mem, w_out, g_ffn_norm, w_router, b_router, w_mlp1, b_mlp1, w_mlp2, b_mlp2, g_final):
    layer_params = dict(g_attn_norm=g_attn_norm, w_in=w_in, b_gate=b_gate, g_q_norm=g_q_norm, w_uq=w_uq,
                        g_kv_norm=g_kv_norm, w_uk=w_uk, w_uv=w_uv, w_o_mla=w_o_mla, mu_shift=mu_shift,
                        w0_decay=w0_decay, w_decay_up=w_decay_up, a0=a0, w_aaa_up=w_aaa_up, k_k=k_k, k_a=k_a,
                        r_k=r_k, ln_x_w=ln_x_w, ln_x_b=ln_x_b, w_o_rwkv=w_o_rwkv, g_mem_norm=g_mem_norm,
                        w_mem_k=w_mem_k, w_mem_v=w_mem_v, w_o_mem=w_o_mem, w_out=w_out, g_ffn_norm=g_ffn_norm,
                        w_router=w_router, b_router=b_router, w_mlp1=w_mlp1, b_mlp1=b_mlp1,
                        w_mlp2=w_mlp2, b_mlp2=b_mlp2)
    depth = w_in.shape[0]
    assert depth == 1, "single-layer trunk"
    W = {name: arr.reshape(arr.shape[1:]) for name, arr in layer_params.items()}
    B, T, D = x_prompt.shape
    DB, TS, _ = x_sample.shape
    assert TS == 1, "sample group decodes one token per sequence"
    n_pages = page_table.shape[1]
    page = cache_kv_latent.shape[2]
    past_len = n_pages * page
    kv_lora = w_uk.shape[1]
    mem_tokens = mem_prompt.shape[1]
    mem_w = MEM_HEADS * MEM_HEAD
    n_e = w_router.shape[2]
    rw_w = RWKV_HEADS * RWKV_HEAD

    tm = min(PROJ_TM, T)
    xp = x_prompt.reshape(B * T, D)
    xs = x_sample.reshape(DB, D)

    tab_p = _rope_tables(jnp.arange(T, dtype=jnp.int32))
    ckv_p, kpe_p, q_h, k_h, v_h = mla_proj_prompt(xp, tab_p, W, tm)
    rw_p, qm_p, gates_p = in_proj_rest(xp, W, tm)
    mk_p, mv_p = mem_kv(mem_prompt.reshape(B * mem_tokens, D), W)
    tq = min(FLASH_TQ, T)
    oa_p = flash_prompt(q_h, k_h, v_h, B, tq)
    oc_p = mem_attn_prompt(qm_p, mk_p.reshape(B, mem_tokens, mem_w), mv_p.reshape(B, mem_tokens, mem_w), tm)
    ob_p, z_p = rwkv_prompt(rw_p, B, W)
    woa_p = jnp.pad(W['w_o_mla'].reshape(MLA_HEADS, V_HEAD, D),
                    ((0, 0), (0, LANES - V_HEAD), (0, 0))).reshape(MLA_HEADS * LANES, D)
    h_p, hn_p, ti_p, tg_p = merge_out(oa_p, woa_p, ob_p, oc_p, gates_p, xp, W, tm)

    tab_s = _rope_tables(jnp.full((DB,), past_len, jnp.int32))
    ckv_s, kpe_s, qlat_s, qpe_s = mla_proj_sample(xs, tab_s, W)
    rw_s, qm_s, gates_s = in_proj_rest(xs, W, DB)
    oa_s = decode_attn(jnp.transpose(qlat_s, (1, 0, 2)), jnp.transpose(qpe_s, (1, 0, 2)),
                       ckv_s.reshape(DB, 1, kv_lora), kpe_s.reshape(DB, 1, QK_ROPE),
                       cache_kv_latent, jnp.swapaxes(cache_k_rope, 2, 3), page_table,
                       W['w_uv']).reshape(DB, MLA_HEADS * V_HEAD)
    oc_s = mem_attn_sample(qm_s.reshape(DB, MEM_HEADS, MEM_HEAD),
                           cache_mem_k.reshape(DB, mem_tokens, MEM_HEADS, MEM_HEAD),
                           cache_mem_v.reshape(DB, mem_tokens, MEM_HEADS, MEM_HEAD)).reshape(DB, mem_w)
    ob_s, st_s = rwkv_sample(rw_s, state_rwkv_shift.reshape(DB, -1),
                             state_rwkv.reshape(DB, RWKV_HEADS, RWKV_HEAD, RWKV_HEAD), W)
    h_s, hn_s, ti_s, tg_s = merge_out(oa_s.astype(BF16), W['w_o_mla'], ob_s, oc_s, gates_s, xs, W, DB)

    hn_all = jnp.concatenate([hn_p, hn_s], axis=0)
    top_i = jnp.concatenate([ti_p, ti_s], axis=0)
    dest, row_tok, block_e, n_used = _routing(top_i, n_e, MOE_TM)
    yb = grouped_mlp(hn_all, row_tok, block_e, n_used, W, MOE_TM)
    n_p = B * T
    y_p = moe_combine(yb, dest[:n_p * TOP_K], tg_p, h_p, g_final, min(128, T))
    y_s = moe_combine(yb, dest[n_p * TOP_K:], tg_s, h_s, g_final, DB)

    zq = z_p.reshape(B, RWKV_HEADS // RWKV_GROUP, RWKV_GROUP, RWKV_HEAD, RWKV_GROUP, RWKV_HEAD)
    st_p = jnp.stack([zq[:, :, q, :, q, :] for q in range(RWKV_GROUP)], axis=2)
    st_p = jnp.swapaxes(st_p.reshape(B, RWKV_HEADS, RWKV_HEAD, RWKV_HEAD), -1, -2)
    lead = lambda a: a[None]
    return (y_p.reshape(B, T, D), y_s.reshape(DB, TS, D),
            lead(ckv_p.reshape(B, T, kv_lora)), lead(kpe_p.reshape(B, T, QK_ROPE)),
            lead(st_p), lead(rw_p.reshape(B, T, -1)[:, -1]),
            lead(mk_p.reshape(B, mem_tokens, MEM_HEADS, MEM_HEAD)),
            lead(mv_p.reshape(B, mem_tokens, MEM_HEADS, MEM_HEAD)),
            lead(ckv_s.reshape(DB, TS, kv_lora)), lead(kpe_s.reshape(DB, TS, QK_ROPE)),
            lead(st_s), lead(rw_s))
```

```python
import functools
import math

import numpy as np
import jax
import jax.numpy as jnp
from jax import lax
from jax.experimental import pallas as pl
from jax.experimental.pallas import tpu as pltpu

F32 = jnp.float32
BF16 = jnp.bfloat16

MLA_HEADS = 8
QK_NOPE = 64
QK_ROPE = 32
V_HEAD = 64
ROPE_THETA = 10000.0
RWKV_HEADS = 8
RWKV_HEAD = 64
DECAY_LORA = 64
AAA_LORA = 64
MEM_HEADS = 4
MEM_HEAD = 128
TOP_K = 4
SWIGLU_ALPHA = 1.702
SWIGLU_LIMIT = 7.0
RMS_EPS = 1e-6
GN_EPS = 64e-5
L2_EPS = 1e-12

LANES = 128
SUBLANES = 8
VMEM_LIMIT = 48 * 1024 * 1024

RWKV_CHUNK = 64
RWKV_GROUP = 2
PROJ_TM = 512
MOE_TM = 512
GMM_BUFS = 3
DECODE_SPLITS = 4
FLASH_TQ = 1024
FLASH_SUBTILES = 2
FLASH_HEADS = 1


def _cparams(sem, vmem=VMEM_LIMIT):
    return pltpu.CompilerParams(dimension_semantics=sem, vmem_limit_bytes=vmem)


def _rms(x, g, eps=RMS_EPS):
    return x * lax.rsqrt(jnp.mean(x * x, axis=-1, keepdims=True) + eps) * g


def _dot(a, b):
    return jnp.dot(a.astype(BF16), b.astype(BF16), preferred_element_type=F32)


def _dot_nt(a, b):
    return lax.dot_general(a.astype(BF16), b.astype(BF16), (((1,), (1,)), ((), ())),
                           preferred_element_type=F32)


def _split(a):
    hi = a.astype(BF16)
    lo = (a - hi.astype(F32)).astype(BF16)
    return hi, lo


def _dot3(a, b):
    ah, al = _split(a)
    bh, bl = _split(b)
    return (jnp.dot(ah, bh, preferred_element_type=F32)
            + (jnp.dot(ah, bl, preferred_element_type=F32)
               + jnp.dot(al, bh, preferred_element_type=F32)))


def _dot2_exact_rhs(a, b):
    ah, al = _split(a)
    bb = b.astype(BF16)
    return jnp.dot(ah, bb, preferred_element_type=F32) + jnp.dot(al, bb, preferred_element_type=F32)


def _dot2_exact_lhs(a, b):
    aa = a.astype(BF16)
    bh, bl = _split(b)
    return jnp.dot(aa, bh, preferred_element_type=F32) + jnp.dot(aa, bl, preferred_element_type=F32)


def _store_row_tiles(ref, x):
    rows, width = x.shape
    assert width == SUBLANES * LANES
    for s in range(SUBLANES):
        ref[pl.ds(s, rows, stride=SUBLANES), :] = x[:, LANES * s:LANES * (s + 1)]


def _load_row_tiles(ref, rows):
    return jnp.concatenate([ref[pl.ds(s, rows, stride=SUBLANES), :] for s in range(SUBLANES)], axis=1)


def _sigmoid(x):
    return 1.0 / (1.0 + jnp.exp(-x))


def _mla_proj_prompt_kernel(x_ref, ga_ref, wcq_ref, gq_ref, wckv_ref, gkv_ref,
                            wkp_ref, wkpr_ref, wkpp_ref, wkprp_ref,
                            wq_ref, wqr_ref, wuk_ref, wuv_ref,
                            cos32_ref, sin32_ref, cost_ref, sint_ref,
                            ckv_out, kpe_out, q_out, k_out, v_out, *, scale):
    xn = _rms(x_ref[...], ga_ref[...]).astype(BF16)
    cqn = _rms(jnp.dot(xn, wcq_ref[...], preferred_element_type=F32), gq_ref[...]).astype(BF16)
    ckvn = _rms(jnp.dot(xn, wckv_ref[...], preferred_element_type=F32), gkv_ref[...])
    ckv_out[...] = ckvn
    kpe_out[...] = (jnp.dot(xn, wkp_ref[...], preferred_element_type=F32) * cos32_ref[...]
                    + jnp.dot(xn, wkpr_ref[...], preferred_element_type=F32) * sin32_ref[...])
    cost = cost_ref[...]
    sint = sint_ref[...]
    kpp = (jnp.dot(xn, wkpp_ref[...], preferred_element_type=F32) * cost
           + jnp.dot(xn, wkprp_ref[...], preferred_element_type=F32) * sint)
    qa = jnp.dot(cqn, wq_ref[...], preferred_element_type=F32)
    qb = jnp.dot(cqn, wqr_ref[...], preferred_element_type=F32)
    ckb = ckvn.astype(BF16)
    ka = jnp.dot(ckb, wuk_ref[...], preferred_element_type=F32)
    va = jnp.dot(ckb, wuv_ref[...], preferred_element_type=F32)
    ones_lane = lax.broadcasted_iota(jnp.int32, cost.shape, 1) == V_HEAD
    for h in range(MLA_HEADS):
        sl = slice(LANES * h, LANES * (h + 1))
        q_out[h] = ((qa[:, sl] * cost + qb[:, sl] * sint) * scale).astype(BF16)
        k_out[h] = (ka[:, sl] + kpp).astype(BF16)
        v_out[h] = jnp.where(ones_lane, 1.0, va[:, sl]).astype(BF16)


def _mla_proj_sample_kernel(x_ref, ga_ref, wcq_ref, gq_ref, wckv_ref, gkv_ref,
                            wkp_ref, wkpr_ref, wqn_ref, wukt_ref, wqp_ref, wqpr_ref,
                            cos32_ref, sin32_ref,
                            ckv_out, kpe_out, qlat_out, qpe_out, *, scale):
    xn = _rms(x_ref[...], ga_ref[...]).astype(BF16)
    cqn = _rms(jnp.dot(xn, wcq_ref[...], preferred_element_type=F32), gq_ref[...]).astype(BF16)
    ckv_out[...] = _rms(jnp.dot(xn, wckv_ref[...], preferred_element_type=F32), gkv_ref[...])
    cos32 = cos32_ref[...]
    sin32 = sin32_ref[...]
    kpe_out[...] = (jnp.dot(xn, wkp_ref[...], preferred_element_type=F32) * cos32
                    + jnp.dot(xn, wkpr_ref[...], preferred_element_type=F32) * sin32)
    qn = jnp.dot(cqn, wqn_ref[...], preferred_element_type=F32)
    for h in range(MLA_HEADS):
        sl = slice(LANES * h, LANES * (h + 1))
        qlat_out[h] = (_dot(qn[:, sl], wukt_ref[h]) * scale).astype(BF16)
        qpe_out[h] = ((jnp.dot(cqn, wqp_ref[h], preferred_element_type=F32) * cos32
                       + jnp.dot(cqn, wqpr_ref[h], preferred_element_type=F32) * sin32) * scale)


def _full(shape):
    nd = len(shape)
    return pl.BlockSpec(shape, lambda *_: (0,) * nd, pipeline_mode=pl.Buffered(1))


def _rope_tables(pos):
    half = QK_ROPE // 2
    inv = jnp.exp(-math.log(ROPE_THETA) * jnp.arange(half, dtype=F32) / half)
    ang = pos.astype(F32)[:, None] * inv[None, :]
    cos = jnp.cos(ang)
    sin = jnp.sin(ang)
    return jnp.concatenate([cos, cos], axis=1), jnp.concatenate([sin, sin], axis=1)


def _rot_cols(w):
    half = QK_ROPE // 2
    return jnp.concatenate([-w[..., half:], w[..., :half]], axis=-1)


def _pad_cols(w, lo, total=LANES):
    pad = [(0, 0)] * (w.ndim - 1) + [(lo, total - lo - w.shape[-1])]
    return jnp.pad(w, pad)


def _mla_weights(W):
    d = W['w_in'].shape[0]
    q_lora = W['w_uq'].shape[0]
    kv_lora = W['w_uk'].shape[0]
    c0, c1, c2 = q_lora, q_lora + kv_lora, q_lora + kv_lora + QK_ROPE
    w_in = W['w_in']
    wcq = w_in[:, :c0].astype(BF16)
    wckv = w_in[:, c0:c1].astype(BF16)
    wkp = w_in[:, c1:c2]
    wuq = W['w_uq'].reshape(q_lora, MLA_HEADS, QK_NOPE + QK_ROPE)
    return dict(d=d, q_lora=q_lora, kv_lora=kv_lora, wcq=wcq, wckv=wckv, wkp=wkp, wuq=wuq,
                ga=W['g_attn_norm'].reshape(1, d), gq=W['g_q_norm'].reshape(1, q_lora),
                gkv=W['g_kv_norm'].reshape(1, kv_lora))


def mla_proj_prompt(x, pos_tables, W, tm):
    n, d = x.shape
    m = _mla_weights(W)
    q_lora, kv_lora = m['q_lora'], m['kv_lora']
    cos32, sin32 = pos_tables
    t = cos32.shape[0]
    ones = jnp.ones((t, QK_NOPE), F32)
    zeros_hi = jnp.zeros((t, LANES - QK_NOPE - QK_ROPE), F32)
    cost = jnp.concatenate([ones, cos32, zeros_hi], axis=1)
    sint = jnp.concatenate([jnp.zeros((t, QK_NOPE), F32), sin32, zeros_hi], axis=1)
    wkp = m['wkp']
    wkpr = _rot_cols(wkp)
    wuq = m['wuq']
    wq = _pad_cols(wuq, 0).reshape(q_lora, MLA_HEADS * LANES).astype(BF16)
    wqr = _pad_cols(_rot_cols(wuq[..., QK_NOPE:]), QK_NOPE).reshape(q_lora, MLA_HEADS * LANES).astype(BF16)
    wuk = _pad_cols(W['w_uk'], 0).reshape(kv_lora, MLA_HEADS * LANES).astype(BF16)
    wuv = _pad_cols(W['w_uv'], 0).reshape(kv_lora, MLA_HEADS * LANES).astype(BF16)
    args = (x, m['ga'], m['wcq'], m['gq'], m['wckv'], m['gkv'],
            wkp.astype(BF16), wkpr.astype(BF16),
            _pad_cols(wkp, QK_NOPE).astype(BF16), _pad_cols(wkpr, QK_NOPE).astype(BF16),
            wq, wqr, wuk, wuv, cos32, sin32, cost, sint)
    nt = t // tm
    row = lambda i: (i, 0)
    tab = lambda i: (i % nt, 0)
    in_specs = [pl.BlockSpec((tm, d), row)] + [_full(a.shape) for a in args[1:14]] + [
        pl.BlockSpec((tm, QK_ROPE), tab), pl.BlockSpec((tm, QK_ROPE), tab),
        pl.BlockSpec((tm, LANES), tab), pl.BlockSpec((tm, LANES), tab)]
    head = pl.BlockSpec((MLA_HEADS, tm, LANES), lambda i: (0, i, 0))
    out_shape = (jax.ShapeDtypeStruct((n, kv_lora), F32), jax.ShapeDtypeStruct((n, QK_ROPE), F32),
                 jax.ShapeDtypeStruct((MLA_HEADS, n, LANES), BF16),
                 jax.ShapeDtypeStruct((MLA_HEADS, n, LANES), BF16),
                 jax.ShapeDtypeStruct((MLA_HEADS, n, LANES), BF16))
    out_specs = (pl.BlockSpec((tm, kv_lora), row), pl.BlockSpec((tm, QK_ROPE), row), head, head, head)
    scale = (QK_NOPE + QK_ROPE) ** -0.5 * math.log2(math.e)
    return pl.pallas_call(
        functools.partial(_mla_proj_prompt_kernel, scale=scale),
        grid=(n // tm,), in_specs=in_specs, out_specs=out_specs, out_shape=out_shape,
        compiler_params=_cparams(("parallel",)), name="mla_proj_prompt")(*args)


def mla_proj_sample(x, pos_tables, W):
    n, d = x.shape
    m = _mla_weights(W)
    q_lora, kv_lora = m['q_lora'], m['kv_lora']
    cos32, sin32 = pos_tables
    wkp = m['wkp']
    wuq = m['wuq']
    wqn = _pad_cols(wuq[..., :QK_NOPE], 0).reshape(q_lora, MLA_HEADS * LANES).astype(BF16)
    wukt = jnp.pad(jnp.transpose(W['w_uk'], (1, 2, 0)), ((0, 0), (0, LANES - QK_NOPE), (0, 0))).astype(BF16)
    wqp = jnp.transpose(wuq[..., QK_NOPE:], (1, 0, 2))
    args = (x, m['ga'], m['wcq'], m['gq'], m['wckv'], m['gkv'],
            wkp.astype(BF16), _rot_cols(wkp).astype(BF16), wqn, wukt,
            wqp.astype(BF16), _rot_cols(wqp).astype(BF16), cos32, sin32)
    out_shape = (jax.ShapeDtypeStruct((n, kv_lora), F32), jax.ShapeDtypeStruct((n, QK_ROPE), F32),
                 jax.ShapeDtypeStruct((MLA_HEADS, n, kv_lora), BF16),
                 jax.ShapeDtypeStruct((MLA_HEADS, n, QK_ROPE), F32))
    scale = (QK_NOPE + QK_ROPE) ** -0.5
    return pl.pallas_call(
        functools.partial(_mla_proj_sample_kernel, scale=scale),
        grid=(1,), in_specs=[_full(a.shape) for a in args],
        out_specs=tuple(_full(s.shape) for s in out_shape), out_shape=out_shape,
        compiler_params=_cparams(("arbitrary",)), name="mla_proj_sample")(*args)


def _in_proj_rest_kernel(x_ref, ga_ref, wrw_ref, wqm_ref, wg_ref, bg_ref,
                         rw_out, qm_out, g_out, *, mem_scale):
    xn = _rms(x_ref[...], ga_ref[...]).astype(BF16)
    rw_out[...] = jnp.dot(xn, wrw_ref[...], preferred_element_type=F32)
    qm_out[...] = (jnp.dot(xn, wqm_ref[...], preferred_element_type=F32) * mem_scale).astype(BF16)
    g_out[...] = _sigmoid(jnp.dot(xn, wg_ref[...], preferred_element_type=F32) + bg_ref[...]).astype(BF16)


def in_proj_rest(x, W, tm):
    n, d = x.shape
    q_lora = W['w_uq'].shape[0]
    kv_lora = W['w_uk'].shape[0]
    rw_cols = W['mu_shift'].shape[0]
    mem_w = MEM_HEADS * MEM_HEAD
    c2 = q_lora + kv_lora + QK_ROPE
    c3 = c2 + rw_cols
    c4 = c3 + mem_w
    w_in = W['w_in']
    n_gate = w_in.shape[1] - c4
    args = (x, W['g_attn_norm'].reshape(1, d), w_in[:, c2:c3].astype(BF16), w_in[:, c3:c4].astype(BF16),
            w_in[:, c4:].astype(BF16), W['b_gate'].reshape(1, n_gate))
    row = lambda i: (i, 0)
    out_shape = (jax.ShapeDtypeStruct((n, rw_cols), F32), jax.ShapeDtypeStruct((n, mem_w), BF16),
                 jax.ShapeDtypeStruct((n, n_gate), BF16))
    return pl.pallas_call(
        functools.partial(_in_proj_rest_kernel, mem_scale=MEM_HEAD ** -0.5),
        grid=(n // tm,),
        in_specs=[pl.BlockSpec((tm, d), row)] + [_full(a.shape) for a in args[1:]],
        out_specs=(pl.BlockSpec((tm, rw_cols), row), pl.BlockSpec((tm, mem_w), row),
                   pl.BlockSpec((tm, n_gate), row)),
        out_shape=out_shape, compiler_params=_cparams(("parallel",)), name="in_proj_rest")(*args)


def _mem_kv_kernel(m_ref, g_ref, wk_ref, wv_ref, k_out, v_out):
    mn = _rms(m_ref[...], g_ref[...]).astype(BF16)
    k_out[...] = jnp.dot(mn, wk_ref[...], preferred_element_type=F32)
    v_out[...] = jnp.dot(mn, wv_ref[...], preferred_element_type=F32)


def mem_kv(mem, W):
    n, d = mem.shape
    mem_w = MEM_HEADS * MEM_HEAD
    args = (mem, W['g_mem_norm'].reshape(1, d), W['w_mem_k'].astype(BF16), W['w_mem_v'].astype(BF16))
    out_shape = (jax.ShapeDtypeStruct((n, mem_w), F32), jax.ShapeDtypeStruct((n, mem_w), F32))
    return pl.pallas_call(
        _mem_kv_kernel, grid=(1,), in_specs=[_full(a.shape) for a in args],
        out_specs=(_full((n, mem_w)), _full((n, mem_w))), out_shape=out_shape,
        compiler_params=_cparams(("arbitrary",)), name="mem_kv")(*args)


def _flash_kernel(q_ref, k_ref, v_ref, o_ref, *scratch, tq, n_sub):
    qi = pl.program_id(2)
    sub = tq // n_sub
    nh, _, dh = q_ref.shape
    n_ch = nh * n_sub
    m_scs = scratch[:n_ch]
    acc_scs = scratch[n_ch:2 * n_ch]
    s_scs = scratch[2 * n_ch:]
    for ch in range(n_ch):
        m_scs[ch][...] = jnp.full(m_scs[ch].shape, -jnp.inf, F32)
        acc_scs[ch][...] = jnp.zeros(acc_scs[ch].shape, F32)

    def scores(j, slot):
        for hd in range(nh):
            kb = k_ref[hd, pl.ds(pl.multiple_of(j * tq, tq), tq), :]
            for u in range(n_sub):
                s_scs[slot][hd * n_sub + u] = lax.dot_general(
                    q_ref[hd, u * sub:(u + 1) * sub, :], kb, (((1,), (1,)), ((), ())), preferred_element_type=F32)

    def consume(j, slot, masked):
        for hd in range(nh):
            vb = v_ref[hd, pl.ds(pl.multiple_of(j * tq, tq), tq), :]
            for u in range(n_sub):
                ch = hd * n_sub + u
                s = s_scs[slot][ch]
                if masked:
                    rows = u * sub + lax.broadcasted_iota(jnp.int32, (sub, tq), 0)
                    cols = lax.broadcasted_iota(jnp.int32, (sub, tq), 1)
                    s = jnp.where(cols <= rows, s, -jnp.inf)
                m_prev = m_scs[ch][...]
                m_new = jnp.maximum(m_prev, jnp.max(s, axis=-1, keepdims=True))
                alpha = jnp.exp2(m_prev - m_new)
                p = jnp.exp2(s - jnp.tile(m_new, (1, tq // LANES)))
                acc_scs[ch][...] = (alpha * acc_scs[ch][...]
                                    + jnp.dot(p.astype(BF16), vb, preferred_element_type=F32))
                m_scs[ch][...] = m_new

    scores(0, 0)

    def body(i, carry):
        scores(2 * i + 1, 1)
        consume(2 * i, 0, False)
        scores(2 * i + 2, 0)
        consume(2 * i + 1, 1, False)
        return carry

    lax.fori_loop(0, qi // 2, body, 0)

    @pl.when(qi % 2 == 1)
    def _():
        scores(qi, 1)
        consume(qi - 1, 0, False)
        consume(qi, 1, True)

    @pl.when(qi % 2 == 0)
    def _():
        consume(qi, 0, True)

    for hd in range(nh):
        for u in range(n_sub):
            acc = acc_scs[hd * n_sub + u][...]
            o_ref[u * sub:(u + 1) * sub, hd * dh:(hd + 1) * dh] = (
                acc / acc[:, V_HEAD:V_HEAD + 1]).astype(o_ref.dtype)


def flash_prompt(q, k, v, batch, tq):
    h, n, dh = q.shape
    t = n // batch
    nq = t // tq
    nh = FLASH_HEADS
    kv_spec = pl.BlockSpec((nh, t, dh), lambda b, hh, i: (hh, b, 0))
    n_sub = FLASH_SUBTILES
    n_ch = nh * n_sub
    sub = tq // n_sub
    return pl.pallas_call(
        functools.partial(_flash_kernel, tq=tq, n_sub=n_sub), grid=(batch, h // nh, nq),
        in_specs=[pl.BlockSpec((nh, tq, dh), lambda b, hh, i: (hh, b * nq + i, 0)), kv_spec, kv_spec],
        out_specs=pl.BlockSpec((tq, nh * dh), lambda b, hh, i: (b * nq + i, hh)),
        out_shape=jax.ShapeDtypeStruct((n, h * dh), BF16),
        scratch_shapes=([pltpu.VMEM((sub, LANES), F32)] * n_ch + [pltpu.VMEM((sub, dh), F32)] * n_ch
                        + [pltpu.VMEM((n_ch, sub, tq), F32)] * 2),
        compiler_params=_cparams(("parallel", "parallel", "arbitrary")), name="flash_prompt")(q, k, v)


def _decode_kernel(pt_ref, qlat_ref, qpe_ref, ckv_ref, kpe_ref, wuv_ref, lat_hbm, rope_hbm, o_ref,
                   lat_buf, rope_buf, sem, *, n_pages, page):
    b = pl.program_id(0)

    def page_copies(seq, slot, i):
        pg = pt_ref[seq, i]
        rows = pl.ds(i * page, page)
        return (pltpu.make_async_copy(lat_hbm.at[0, pg], lat_buf.at[slot, rows], sem.at[slot]),
                pltpu.make_async_copy(rope_hbm.at[0, pg], rope_buf.at[slot, :, rows], sem.at[slot]))

    def start_all(seq, slot):
        for i in range(n_pages):
            for c in page_copies(seq, slot, i):
                c.start()

    @pl.when(b == 0)
    def _():
        start_all(0, 0)

    @pl.when(b + 1 < pl.num_programs(0))
    def _():
        start_all(b + 1, (b + 1) % 2)

    slot = b % 2

    def drain(i, carry):
        for c in page_copies(b, slot, 0):
            c.wait()
        return carry

    lax.fori_loop(0, n_pages, drain, 0)

    qlat = qlat_ref[0]
    qpe = qpe_ref[0]
    ckv = ckv_ref[0]
    kpe = kpe_ref[0]
    s_new = (jnp.sum(qlat.astype(F32) * ckv, axis=-1, keepdims=True)
             + jnp.sum(qpe * kpe, axis=-1, keepdims=True))
    past = n_pages * page
    n_split = DECODE_SPLITS if past % (DECODE_SPLITS * LANES) == 0 else 1
    span = past // n_split
    kparts = [lat_buf[slot, c * span:(c + 1) * span, :].astype(BF16) for c in range(n_split)]
    sparts = [_dot_nt(qlat, kparts[c]) + _dot(qpe, rope_buf[slot, :, c * span:(c + 1) * span])
              for c in range(n_split)]
    mparts = [jnp.max(sp, axis=-1, keepdims=True) for sp in sparts]
    pparts = [jnp.exp(sp - mp) for sp, mp in zip(sparts, mparts)]
    oparts = [jnp.dot(pp.astype(BF16), kp, preferred_element_type=F32) for pp, kp in zip(pparts, kparts)]
    m = s_new
    for mp in mparts:
        m = jnp.maximum(m, mp)
    p_new = jnp.exp(s_new - m)
    l = p_new
    o_acc = p_new * ckv
    for mp, pp, op in zip(mparts, pparts, oparts):
        w = jnp.exp(mp - m)
        l = l + w * jnp.sum(pp, axis=-1, keepdims=True)
        o_acc = o_acc + w * op
    o_lat = o_acc / l
    ov = _dot(o_lat, wuv_ref[...])
    hh = lax.broadcasted_iota(jnp.int32, ov.shape, 0)
    cc = lax.broadcasted_iota(jnp.int32, ov.shape, 1) // V_HEAD
    o_ref[0] = jnp.sum(jnp.where(hh == cc, ov, 0.0), axis=0, keepdims=True)


def decode_attn(qlat, qpe, ckv, kpe, cache_lat, cache_rope, page_table, w_uv):
    db, h, kv = qlat.shape
    _, n_pool, page, _ = cache_lat.shape
    n_pages = page_table.shape[1]
    past = n_pages * page
    wuv = w_uv.reshape(kv, h * V_HEAD).astype(BF16)
    per_b = lambda shape: pl.BlockSpec((1,) + shape, lambda b, pt: (b, 0, 0))
    in_specs = [per_b((h, kv)), per_b((h, QK_ROPE)), per_b((1, kv)), per_b((1, QK_ROPE)),
                pl.BlockSpec(wuv.shape, lambda b, pt: (0, 0)),
                pl.BlockSpec(memory_space=pl.ANY), pl.BlockSpec(memory_space=pl.ANY)]
    grid_spec = pltpu.PrefetchScalarGridSpec(
        num_scalar_prefetch=1, grid=(db,), in_specs=in_specs,
        out_specs=per_b((1, h * V_HEAD)),
        scratch_shapes=[pltpu.VMEM((2, past, kv), F32), pltpu.VMEM((2, QK_ROPE, past), F32),
                        pltpu.SemaphoreType.DMA((2,))])
    return pl.pallas_call(
        functools.partial(_decode_kernel, n_pages=n_pages, page=page), grid_spec=grid_spec,
        out_shape=jax.ShapeDtypeStruct((db, 1, h * V_HEAD), F32),
        compiler_params=_cparams(("arbitrary",)), name="decode_attn")(
            page_table, qlat, qpe, ckv, kpe, wuv, cache_lat, cache_rope)


def _mem_attn_prompt_kernel(q_ref, k_ref, v_ref, o_ref):
    q = q_ref[...]
    k = k_ref[0].astype(BF16)
    v = v_ref[0].astype(BF16)
    for h in range(MEM_HEADS):
        sl = slice(MEM_HEAD * h, MEM_HEAD * (h + 1))
        s = _dot_nt(q[:, sl], k[:, sl])
        p = jnp.exp(s - jnp.max(s, axis=-1, keepdims=True))
        o = jnp.dot(p.astype(BF16), v[:, sl], preferred_element_type=F32) / jnp.sum(p, axis=-1, keepdims=True)
        o_ref[:, sl] = o.astype(o_ref.dtype)


def mem_attn_prompt(qm, mk, mv, tm):
    n, w = qm.shape
    b, m, _ = mk.shape
    nt = n // b // tm
    kv_spec = pl.BlockSpec((1, m, w), lambda i: (i // nt, 0, 0))
    return pl.pallas_call(
        _mem_attn_prompt_kernel, grid=(n // tm,),
        in_specs=[pl.BlockSpec((tm, w), lambda i: (i, 0)), kv_spec, kv_spec],
        out_specs=pl.BlockSpec((tm, w), lambda i: (i, 0)),
        out_shape=jax.ShapeDtypeStruct((n, w), BF16),
        compiler_params=_cparams(("parallel",)), name="mem_attn_prompt")(qm, mk, mv)


def _mem_attn_sample_kernel(q_ref, k_ref, v_ref, o_ref, *, tb):
    for bl in range(tb):
        q = q_ref[bl].astype(F32)
        s = jnp.sum(k_ref[bl] * q, axis=-1, keepdims=True)
        p = jnp.exp(s - jnp.max(s, axis=0, keepdims=True))
        o = jnp.sum(p * v_ref[bl], axis=0) / jnp.sum(p, axis=0)
        o_ref[bl] = o.astype(o_ref.dtype)


def mem_attn_sample(qm, mk, mv, tb=8):
    db, nh, hd = qm.shape
    m = mk.shape[1]
    kv_spec = pl.BlockSpec((tb, m, nh, hd), lambda i: (i, 0, 0, 0))
    q_spec = pl.BlockSpec((tb, nh, hd), lambda i: (i, 0, 0))
    return pl.pallas_call(
        functools.partial(_mem_attn_sample_kernel, tb=tb), grid=(db // tb,),
        in_specs=[q_spec, kv_spec, kv_spec], out_specs=q_spec,
        out_shape=jax.ShapeDtypeStruct((db, nh, hd), BF16),
        compiler_params=_cparams(("parallel",)), name="mem_attn_sample")(qm, mk, mv)


def _rwkv_prep(xr, lora_w, w0, a0, kk_w, ka_w, gmat):
    w = RWKV_HEADS * RWKV_HEAD
    r = xr[:, 0:w]
    k = xr[:, w:2 * w]
    v = xr[:, 2 * w:3 * w]
    tail = xr[:, 3 * w:3 * w + DECAY_LORA + AAA_LORA]
    lane = lax.broadcasted_iota(jnp.int32, tail.shape, 1)
    z = jnp.where(lane < DECAY_LORA, jnp.tanh(tail), tail)
    lo = _dot3(z, lora_w)
    dec = lo[:, :w] + w0
    aaa = lo[:, w:] + a0
    u = -dec
    softplus = jnp.maximum(u, 0.0) + jnp.log(1.0 + jnp.exp(-jnp.abs(u)))
    logw = -jnp.exp(-softplus - 0.5)
    a_sig = _sigmoid(aaa)
    kk = k * kk_w
    norm = jnp.sqrt(_dot2_exact_rhs(kk * kk, gmat))
    kkn = kk / jnp.maximum(norm, L2_EPS)
    k2 = k * (1.0 + (a_sig - 1.0) * ka_w)
    return r, k2, v, logw, -kkn, kkn * a_sig


def _rwkv_post(y, r, k2, v, rk, lnw, lnb, gmat):
    inv_n = 1.0 / RWKV_HEAD
    mu = _dot2_exact_rhs(y, gmat) * inv_n
    d = y - mu
    var = _dot2_exact_rhs(d * d, gmat) * inv_n
    yn = d * lax.rsqrt(var + GN_EPS) * lnw + lnb
    return yn + _dot2_exact_rhs(r * k2 * rk, gmat) * v


def _rwkv_prompt_kernel(rw_ref, mu_ref, lora_ref, w0_ref, a0_ref, kk_ref, ka_ref, rk_ref,
                        lnw_ref, lnb_ref, g_ref, out_ref, zout_ref, prev_sc, z_sc, *, chunk):
    c = chunk
    nb = rw_ref.shape[0]

    @pl.when(pl.program_id(0) == 0)
    def _():
        prev_sc[...] = jnp.zeros(prev_sc.shape, F32)
        z_sc[...] = jnp.zeros(z_sc.shape, F32)

    rw = jnp.concatenate([rw_ref[b] for b in range(nb)], axis=0)
    row = lax.broadcasted_iota(jnp.int32, rw.shape, 0)
    prev = pltpu.roll(rw, 1, 0)
    for b in range(nb):
        prev = jnp.where(row == b * c, prev_sc[b:b + 1, :], prev)
        prev_sc[b:b + 1, :] = rw[(b + 1) * c - 1:(b + 1) * c, :]
    xr = rw + (prev - rw) * mu_ref[...]
    gmat = g_ref[...]
    r, k2, v, logw, av, bv = _rwkv_prep(xr, lora_ref[...], w0_ref[...], a0_ref[...],
                                        kk_ref[...], ka_ref[...], gmat)

    ti = lax.broadcasted_iota(jnp.int32, (nb * c, nb * c), 0)
    si = lax.broadcasted_iota(jnp.int32, (nb * c, nb * c), 1)
    causal = jnp.logical_and(si <= ti, (si // c) == (ti // c))
    cum = _dot2_exact_lhs(causal.astype(F32), logw)
    clast = jnp.concatenate([jnp.broadcast_to(cum[(b + 1) * c - 1:(b + 1) * c, :], (c, cum.shape[1]))
                             for b in range(nb)], axis=0)
    e_neg = jnp.exp(-cum)
    e_rest = jnp.exp(clast - cum)
    at = av * jnp.exp(cum - logw)
    bt = bv * e_neg
    kt = k2 * e_neg
    rt = r * jnp.exp(cum)
    bg = bv * e_rest
    kg = k2 * e_rest
    glast = jnp.exp(clast)

    g = RWKV_GROUP
    gl = g * RWKV_HEAD
    gc = g * c
    n_groups = RWKV_HEADS // g
    emask = (lax.broadcasted_iota(jnp.int32, (gc, gl), 0) // c
             == lax.broadcasted_iota(jnp.int32, (gc, gl), 1) // RWKV_HEAD)
    br = lax.broadcasted_iota(jnp.int32, (gc, gc), 0)
    bc = lax.broadcasted_iota(jnp.int32, (gc, gc), 1)
    same = (br // c) == (bc // c)
    strict = jnp.logical_and(same, (br % c) > (bc % c))
    incl = jnp.logical_and(same, (br % c) >= (bc % c))
    eye_gc = (br == bc).astype(F32)
    lane_head = lax.broadcasted_iota(jnp.int32, (c, gl), 1) // RWKV_HEAD
    eye_gl = (lax.broadcasted_iota(jnp.int32, (gl, gl), 0) == lax.broadcasted_iota(jnp.int32, (gl, gl), 1))

    def dup(x):
        return jnp.concatenate([x] * g, axis=0)

    def expand(x):
        return jnp.where(emask, dup(x), 0.0)

    def per_head(x):
        return jnp.concatenate([jnp.where(lane_head == q, x, 0.0) for q in range(g)], axis=0)

    items = [(b, p) for b in range(nb) for p in range(n_groups)]

    def part(x, item):
        b, p = item
        return x[b * c:(b + 1) * c, gl * p:gl * (p + 1)]

    aps = []
    for it in items:
        lhs = jnp.concatenate([part(at, it), part(rt, it)], axis=0)
        rhs = jnp.concatenate([per_head(part(bt, it)), per_head(part(kt, it))], axis=0)
        aps.append(_dot_nt(lhs, rhs))
    nmats = [jnp.where(strict, dup(ap[0:c, 0:gc]), 0.0) for ap in aps]
    akms = [jnp.where(strict, dup(ap[0:c, gc:2 * gc]), 0.0) for ap in aps]
    rbms = [jnp.where(incl, dup(ap[c:2 * c, 0:gc]), 0.0) for ap in aps]
    rkms = [jnp.where(incl, dup(ap[c:2 * c, gc:2 * gc]), 0.0) for ap in aps]
    tmats = [eye_gc + nm for nm in nmats]
    pws = nmats
    for _ in range(int(math.log2(c)) - 1):
        pws = [_dot(pw, pw) for pw in pws]
        tmats = [tm + _dot(tm, pw) for tm, pw in zip(tmats, pws)]
    ves = [expand(part(v, it)) for it in items]
    akvs = [_dot(akm, ve) for akm, ve in zip(akms, ves)]
    wus = [_dot(tm, jnp.concatenate([expand(part(at, it)), akv], axis=1))
           for tm, it, akv in zip(tmats, items, akvs)]
    bkts = [jnp.concatenate([expand(part(bg, it)), expand(part(kg, it))], axis=0).T
            for it in items]
    gcols = [jnp.sum(jnp.where(eye_gl, part(glast, it)[0:1, :], 0.0), axis=1, keepdims=True) for it in items]
    zs = [z_sc[b, p] for b, p in items]
    ues = [_dot(wu[:, :gl], z) + wu[:, gl:] for wu, z in zip(wus, zs)]
    for n, (b, p) in enumerate(items):
        z_new = gcols[n] * zs[n] + _dot3(bkts[n], jnp.concatenate([ues[n], ves[n]], axis=0))
        z_sc[b, p] = z_new
        zout_ref[b, p] = z_new
    ys = [[None] * n_groups for _ in range(nb)]
    for n, (b, p) in enumerate(items):
        ye = _dot(jnp.concatenate([expand(part(rt, items[n])), rbms[n], rkms[n]], axis=1),
                  jnp.concatenate([zs[n], ues[n], ves[n]], axis=0))
        yp = ye[0:c, :]
        for q in range(1, g):
            yp = yp + ye[q * c:(q + 1) * c, :]
        ys[b][p] = yp
    y = jnp.concatenate([jnp.concatenate(yb, axis=1) for yb in ys], axis=0)
    out = _rwkv_post(y, r, k2, v, rk_ref[...], lnw_ref[...], lnb_ref[...], gmat)
    for b in range(nb):
        out_ref[b] = out[b * c:(b + 1) * c, :].astype(out_ref.dtype)


def _rwkv_params(W):
    w = RWKV_HEADS * RWKV_HEAD
    lora = jnp.zeros((DECAY_LORA + AAA_LORA, 2 * w), F32)
    lora = lora.at[:DECAY_LORA, :w].set(W['w_decay_up']).at[DECAY_LORA:, w:].set(W['w_aaa_up'])
    head = np.arange(w) // RWKV_HEAD
    gmat = jnp.asarray((head[:, None] == head[None, :]).astype(np.float32))
    row = lambda a: a.reshape(1, -1)
    return dict(mu=row(W['mu_shift']), lora=lora, w0=row(W['w0_decay']), a0=row(W['a0']),
                kk=row(W['k_k']), ka=row(W['k_a']), rk=row(W['r_k']), lnw=row(W['ln_x_w']),
                lnb=row(W['ln_x_b']), gmat=gmat)


def rwkv_prompt(rw, batch, W):
    n, cols = rw.shape
    t = n // batch
    c = RWKV_CHUNK
    nc = t // c
    w = RWKV_HEADS * RWKV_HEAD
    pr = _rwkv_params(W)
    args = (rw.reshape(batch, t, cols), pr['mu'], pr['lora'], pr['w0'], pr['a0'], pr['kk'], pr['ka'], pr['rk'],
            pr['lnw'], pr['lnb'], pr['gmat'])
    in_specs = [pl.BlockSpec((batch, c, cols), lambda i: (0, i, 0))] + [
        pl.BlockSpec(a.shape, lambda i: (0, 0)) for a in args[1:]]
    n_groups = RWKV_HEADS // RWKV_GROUP
    gl = RWKV_GROUP * RWKV_HEAD
    out_shape = (jax.ShapeDtypeStruct((batch, t, w), BF16),
                 jax.ShapeDtypeStruct((batch, n_groups, gl, gl), F32))
    ob, z = pl.pallas_call(
        functools.partial(_rwkv_prompt_kernel, chunk=c), grid=(nc,), in_specs=in_specs,
        out_specs=(pl.BlockSpec((batch, c, w), lambda i: (0, i, 0)),
                   pl.BlockSpec((batch, n_groups, gl, gl), lambda i: (0, 0, 0, 0))),
        out_shape=out_shape,
        scratch_shapes=[pltpu.VMEM((batch, cols), F32), pltpu.VMEM((batch, n_groups, gl, gl), F32)],
        compiler_params=_cparams(("arbitrary",)), name="rwkv_prompt")(*args)
    return ob.reshape(n, w), z


def _rwkv_sample_prep_kernel(rw_ref, sh_ref, mu_ref, lora_ref, w0_ref, a0_ref, kk_ref, ka_ref, g_ref,
                             r_out, k_out, v_out, w_out, a_out, b_out):
    rw = rw_ref[...]
    xr = rw + (sh_ref[...] - rw) * mu_ref[...]
    r, k2, v, logw, av, bv = _rwkv_prep(xr, lora_ref[...], w0_ref[...], a0_ref[...],
                                        kk_ref[...], ka_ref[...], g_ref[...])
    r_out[...] = r
    k_out[...] = k2
    v_out[...] = v
    w_out[...] = jnp.exp(logw)
    a_out[...] = av
    b_out[...] = bv


def _rwkv_sample_state_kernel(s_ref, r_ref, k_ref, v_ref, w_ref, a_ref, b_ref, s_out, y_out, *, tb):
    n = RWKV_HEAD
    eye = lax.broadcasted_iota(jnp.int32, (n, n), 0) == lax.broadcasted_iota(jnp.int32, (n, n), 1)
    for bl in range(tb):
        for h in range(RWKV_HEADS):
            s = s_ref[bl, h]
            row = lambda ref: ref[bl, h:h + 1, :]
            sa = jnp.sum(s * row(a_ref), axis=1, keepdims=True)
            vcol = jnp.sum(jnp.where(eye, row(v_ref), 0.0), axis=1, keepdims=True)
            s_new = s * row(w_ref) + sa * row(b_ref) + vcol * row(k_ref)
            s_out[bl, h] = s_new
            ycol = jnp.sum(s_new * row(r_ref), axis=1, keepdims=True)
            y_out[bl, h:h + 1, :] = jnp.sum(jnp.where(eye, ycol, 0.0), axis=0, keepdims=True)


def _rwkv_sample_post_kernel(y_ref, r_ref, k_ref, v_ref, rk_ref, lnw_ref, lnb_ref, g_ref, o_ref):
    o_ref[...] = _rwkv_post(y_ref[...], r_ref[...], k_ref[...], v_ref[...], rk_ref[...],
                            lnw_ref[...], lnb_ref[...], g_ref[...]).astype(o_ref.dtype)


def rwkv_sample(rw, shift0, state, W, tb=8):
    db, cols = rw.shape
    w = RWKV_HEADS * RWKV_HEAD
    pr = _rwkv_params(W)
    args = (rw, shift0, pr['mu'], pr['lora'], pr['w0'], pr['a0'], pr['kk'], pr['ka'], pr['gmat'])
    vec = jax.ShapeDtypeStruct((db, w), F32)
    r, k2, v, wd, av, bv = pl.pallas_call(
        _rwkv_sample_prep_kernel, grid=(1,), in_specs=[_full(a.shape) for a in args],
        out_specs=tuple(_full((db, w)) for _ in range(6)), out_shape=(vec,) * 6,
        compiler_params=_cparams(("arbitrary",)), name="rwkv_sample_prep")(*args)
    heads = lambda a: a.reshape(db, RWKV_HEADS, RWKV_HEAD)
    st_spec = pl.BlockSpec((tb, RWKV_HEADS, RWKV_HEAD, RWKV_HEAD), lambda i: (i, 0, 0, 0))
    hv_spec = pl.BlockSpec((tb, RWKV_HEADS, RWKV_HEAD), lambda i: (i, 0, 0))
    s_new, y = pl.pallas_call(
        functools.partial(_rwkv_sample_state_kernel, tb=tb), grid=(db // tb,),
        in_specs=[st_spec] + [hv_spec] * 6, out_specs=(st_spec, hv_spec),
        out_shape=(jax.ShapeDtypeStruct(state.shape, F32),
                   jax.ShapeDtypeStruct((db, RWKV_HEADS, RWKV_HEAD), F32)),
        compiler_params=_cparams(("parallel",)), name="rwkv_sample_state")(
            state, heads(r), heads(k2), heads(v), heads(wd), heads(av), heads(bv))
    pargs = (y.reshape(db, w), r, k2, v, pr['rk'], pr['lnw'], pr['lnb'], pr['gmat'])
    ob = pl.pallas_call(
        _rwkv_sample_post_kernel, grid=(1,), in_specs=[_full(a.shape) for a in pargs],
        out_specs=_full((db, w)), out_shape=jax.ShapeDtypeStruct((db, w), BF16),
        compiler_params=_cparams(("arbitrary",)), name="rwkv_sample_post")(*pargs)
    return ob, s_new


def _merge_kernel(oa_ref, ob_ref, oc_ref, g_ref, x_ref, woa_ref, wob_ref, woc_ref, wout_ref,
                  gffn_ref, wr_ref, br_ref, h_out, hn_out, ti_out, tg_out, *, d):
    o_a = jnp.dot(oa_ref[...], woa_ref[...], preferred_element_type=F32)
    o_b = jnp.dot(ob_ref[...], wob_ref[...], preferred_element_type=F32)
    o_c = jnp.dot(oc_ref[...], woc_ref[...], preferred_element_type=F32)
    g = g_ref[...].astype(F32)
    merged = g[:, 0:d] * o_a + g[:, d:2 * d] * o_b + g[:, 2 * d:3 * d] * o_c
    h = x_ref[...] + _dot(merged, wout_ref[...])
    h_out[...] = h
    hn = _rms(h, gffn_ref[...])
    _store_row_tiles(hn_out, hn)
    logits = _dot3(hn, wr_ref[...]) + br_ref[...]
    n_e = logits.shape[1]
    lane = lax.broadcasted_iota(jnp.int32, logits.shape, 1)
    cur = logits
    vals, idxs = [], []
    for _ in range(TOP_K):
        mx = jnp.max(cur, axis=-1, keepdims=True)
        ix = jnp.min(jnp.where(cur == mx, lane, n_e), axis=-1, keepdims=True)
        vals.append(mx)
        idxs.append(ix)
        cur = jnp.where(lane == ix, -jnp.inf, cur)
    es = [jnp.exp(vk - vals[0]) for vk in vals]
    den = es[0]
    for e in es[1:]:
        den = den + e
    lane_k = lax.broadcasted_iota(jnp.int32, ti_out.shape, 1)
    ti = jnp.zeros(ti_out.shape, jnp.int32)
    tg = jnp.zeros(tg_out.shape, F32)
    for kk in range(TOP_K):
        ti = jnp.where(lane_k == kk, idxs[kk], ti)
        tg = jnp.where(lane_k == kk, es[kk] / den, tg)
    ti_out[...] = ti
    tg_out[...] = tg


def merge_out(oa, w_oa, ob, oc, gates, x, W, tm):
    n, d = x.shape
    n_e = W['w_router'].shape[1]
    args = (oa, ob, oc, gates, x, w_oa.astype(BF16), W['w_o_rwkv'].astype(BF16), W['w_o_mem'].astype(BF16),
            W['w_out'].astype(BF16), W['g_ffn_norm'].reshape(1, d), W['w_router'], W['b_router'].reshape(1, n_e))
    row = lambda i: (i, 0)
    in_specs = [pl.BlockSpec((tm, a.shape[1]), row) for a in args[:5]] + [_full(a.shape) for a in args[5:]]
    out_shape = (jax.ShapeDtypeStruct((n, d), F32), jax.ShapeDtypeStruct((n * SUBLANES, LANES), F32),
                 jax.ShapeDtypeStruct((n, TOP_K), jnp.int32), jax.ShapeDtypeStruct((n, TOP_K), F32))
    out_specs = (pl.BlockSpec((tm, d), row), pl.BlockSpec((tm * SUBLANES, LANES), row),
                 pl.BlockSpec((tm, TOP_K), row), pl.BlockSpec((tm, TOP_K), row))
    return pl.pallas_call(
        functools.partial(_merge_kernel, d=d), grid=(n // tm,), in_specs=in_specs, out_specs=out_specs,
        out_shape=out_shape, compiler_params=_cparams(("parallel",)), name="merge_out")(*args)


def _pair_split_kernel(w_ref, p_ref, o_ref):
    p = p_ref[...]
    for t in range(w_ref.shape[2] // (2 * LANES)):
        sl = slice(2 * LANES * t, 2 * LANES * (t + 1))
        o_ref[0, :, sl] = jnp.dot(w_ref[0, :, sl].astype(BF16), p, preferred_element_type=F32).astype(BF16)


def pair_split_cast(w, tr=512):
    n_e, d, cols = w.shape
    j = np.arange(2 * LANES)
    src = np.where(j < LANES, 2 * j, 2 * (j - LANES) + 1)
    perm = jnp.asarray((np.arange(2 * LANES)[:, None] == src[None, :]).astype(np.float32), dtype=BF16)
    blk = pl.BlockSpec((1, tr, cols), lambda e, i: (e, i, 0))
    return pl.pallas_call(
        _pair_split_kernel, grid=(n_e, d // tr),
        in_specs=[blk, pl.BlockSpec(perm.shape, lambda e, i: (0, 0))], out_specs=blk,
        out_shape=jax.ShapeDtypeStruct(w.shape, BF16),
        compiler_params=_cparams(("parallel", "parallel")), name="moe_w1_prep")(w, perm)


def _gmm_kernel(be_ref, nu_ref, tok_ref, src_ref, w1_ref, b1_ref, w2_ref, b2_ref,
                y_ref, xbuf, sem, *, tm):
    i = pl.program_id(0)
    n_used = nu_ref[0]

    def row_copy(blk, r, slot):
        return pltpu.make_async_copy(src_ref.at[pl.ds(tok_ref[blk * tm + r] * SUBLANES, SUBLANES)],
                                     xbuf.at[slot, pl.ds(r * SUBLANES, SUBLANES)], sem.at[slot])

    def drain(slot):
        def step(r, carry):
            pltpu.make_async_copy(src_ref.at[pl.ds(0, SUBLANES)], xbuf.at[slot, pl.ds(0, SUBLANES)],
                                  sem.at[slot]).wait()
            return carry
        lax.fori_loop(0, tm, step, 0, unroll=8)

    last = n_used - 1

    @pl.when(jnp.logical_and(i == 0, n_used > 0))
    def _():
        def issue(r, carry):
            row_copy(0, r, 0).start()
            row_copy(jnp.minimum(1, last), r, 1).start()
            return carry
        lax.fori_loop(0, tm, issue, 0, unroll=8)

    @pl.when(i < n_used)
    def _():
        slot = i % GMM_BUFS
        drain(slot)
        x = _load_row_tiles(xbuf.at[slot], tm).astype(BF16)
        nxt = jnp.minimum(i + 2, last)
        nslot = (i + 2) % GMM_BUFS
        for r in range(tm):
            row_copy(nxt, r, nslot).start()
        h = jnp.dot(x, w1_ref[0], preferred_element_type=F32) + b1_ref[0]
        acts = []
        for t in range(w2_ref.shape[1] // LANES):
            x_glu = jnp.minimum(h[:, 2 * LANES * t:2 * LANES * t + LANES], SWIGLU_LIMIT)
            x_lin = jnp.clip(h[:, 2 * LANES * t + LANES:2 * LANES * (t + 1)], -SWIGLU_LIMIT, SWIGLU_LIMIT)
            acts.append((x_glu * _sigmoid(SWIGLU_ALPHA * x_glu) * (x_lin + 1.0)).astype(BF16))
        y = jnp.dot(jnp.concatenate(acts, axis=1), w2_ref[0], preferred_element_type=F32) + b2_ref[0]
        _store_row_tiles(y_ref, y)

    @pl.when(i == last)
    def _():
        drain((i + 1) % GMM_BUFS)
        drain((i + 2) % GMM_BUFS)

    @pl.when(i >= n_used)
    def _():
        y_ref[...] = jnp.zeros(y_ref.shape, y_ref.dtype)


def grouped_mlp(src, row_tok, block_e, n_used, W, tm):
    n_rows = row_tok.shape[0]
    d = W['w_mlp1'].shape[1]
    n_e = W['w_mlp1'].shape[0]
    d_ff = W['w_mlp2'].shape[1]
    w1 = pair_split_cast(W['w_mlp1'])
    b1 = W['b_mlp1'].reshape(n_e, d_ff // LANES, LANES, 2)
    b1 = jnp.swapaxes(b1, 2, 3).reshape(n_e, 1, 2 * d_ff)
    w2 = W['w_mlp2'].astype(BF16)
    b2 = W['b_mlp2'].reshape(n_e, 1, d)
    ex = lambda i, be, nu, tok: (be[i], 0, 0)
    grid_spec = pltpu.PrefetchScalarGridSpec(
        num_scalar_prefetch=3, grid=(n_rows // tm,),
        in_specs=[pl.BlockSpec(memory_space=pl.ANY),
                  pl.BlockSpec((1, d, 2 * d_ff), ex), pl.BlockSpec((1, 1, 2 * d_ff), ex),
                  pl.BlockSpec((1, d_ff, d), ex), pl.BlockSpec((1, 1, d), ex)],
        out_specs=pl.BlockSpec((tm * SUBLANES, LANES), lambda i, be, nu, tok: (i, 0)),
        scratch_shapes=[pltpu.VMEM((GMM_BUFS, tm * SUBLANES, LANES), F32), pltpu.SemaphoreType.DMA((GMM_BUFS,))])
    return pl.pallas_call(
        functools.partial(_gmm_kernel, tm=tm), grid_spec=grid_spec,
        out_shape=jax.ShapeDtypeStruct((n_rows * SUBLANES, LANES), F32),
        compiler_params=_cparams(("arbitrary",)), name="moe_gmm")(
            block_e, n_used, row_tok, src, w1, b1, w2, b2)


def _combine_kernel(dest_ref, gate_ref, h_ref, gf_ref, yb_ref, o_ref, buf, sem, *, tc):
    i = pl.program_id(0)

    def start(step, slot):
        def issue(t, carry):
            for k in range(TOP_K):
                pltpu.make_async_copy(
                    yb_ref.at[pl.ds(dest_ref[(step * tc + t) * TOP_K + k] * SUBLANES, SUBLANES)],
                    buf.at[slot, k, pl.ds(t * SUBLANES, SUBLANES)], sem.at[slot]).start()
            return carry
        lax.fori_loop(0, tc, issue, 0, unroll=4)

    @pl.when(i == 0)
    def _():
        start(0, 0)

    @pl.when(i + 1 < pl.num_programs(0))
    def _():
        start(i + 1, (i + 1) % 2)

    slot = i % 2

    def drain(t, carry):
        for k in range(TOP_K):
            pltpu.make_async_copy(yb_ref.at[pl.ds(0, SUBLANES)], buf.at[slot, k, pl.ds(0, SUBLANES)],
                                  sem.at[slot]).wait()
        return carry

    lax.fori_loop(0, tc, drain, 0, unroll=4)
    gate = gate_ref[...]
    y = h_ref[...]
    for k in range(TOP_K):
        y = y + gate[:, k:k + 1] * _load_row_tiles(buf.at[slot, k], tc)
    o_ref[...] = _rms(y, gf_ref[...])


def moe_combine(yb, dest, gate, h, g_final, tc):
    n, d = h.shape
    grid_spec = pltpu.PrefetchScalarGridSpec(
        num_scalar_prefetch=1, grid=(n // tc,),
        in_specs=[pl.BlockSpec((tc, TOP_K), lambda i, ds: (i, 0)),
                  pl.BlockSpec((tc, d), lambda i, ds: (i, 0)),
                  pl.BlockSpec((1, d), lambda i, ds: (0, 0)),
                  pl.BlockSpec(memory_space=pl.ANY)],
        out_specs=pl.BlockSpec((tc, d), lambda i, ds: (i, 0)),
        scratch_shapes=[pltpu.VMEM((2, TOP_K, tc * SUBLANES, LANES), F32), pltpu.SemaphoreType.DMA((2,))])
    return pl.pallas_call(
        functools.partial(_combine_kernel, tc=tc), grid_spec=grid_spec,
        out_shape=jax.ShapeDtypeStruct((n, d), F32),
        compiler_params=_cparams(("arbitrary",)), name="moe_combine")(dest, gate, h, g_final.reshape(1, d), yb)


def _routing(top_i, n_e, tm):
    n_tok = top_i.shape[0]
    n_assign = n_tok * TOP_K
    flat_e = top_i.reshape(-1)
    onehot = (flat_e[:, None] == jnp.arange(n_e, dtype=jnp.int32)[None, :]).astype(jnp.int32)
    csum = jnp.cumsum(onehot, axis=0)
    counts = csum[-1]
    padded = ((counts + tm - 1) // tm) * tm
    pend = jnp.cumsum(padded)
    pstart = pend - padded
    dest = jnp.sum(onehot * (csum - onehot + pstart[None, :]), axis=1).astype(jnp.int32)
    n_blocks = -(-n_assign // tm) + n_e
    n_rows = n_blocks * tm
    row_tok = jnp.zeros((n_rows,), jnp.int32).at[dest].set(jnp.arange(n_assign, dtype=jnp.int32) // TOP_K)
    block_row0 = jnp.arange(n_blocks, dtype=jnp.int32) * tm
    block_e = jnp.minimum(jnp.sum((pend[None, :] <= block_row0[:, None]).astype(jnp.int32), axis=1), n_e - 1)
    n_used = (pend[-1:] // tm).astype(jnp.int32)
    return dest, row_tok, block_e, n_used


def kernel(x_prompt, x_sample, mem_prompt, cache_kv_latent, cache_k_rope, page_table, state_rwkv, state_rwkv_shift, cache_mem_k, cache_mem_v, g_attn_norm, w_in, b_gate, g_q_norm, w_uq, g_kv_norm, w_uk, w_uv, w_o_mla, mu_shift, w0_decay, w_decay_up, a0, w_aaa_up, k_k, k_a, r_k, ln_x_w, ln_x_b, w_o_rwkv, g_mem_norm, w_mem_k, w_mem_v, w_o_mem, w_out, g_ffn_norm, w_router, b_router, w_mlp1, b_mlp1, w_mlp2, b_mlp2, g_final):
    layer_params = dict(g_attn_norm=g_attn_norm, w_in=w_in, b_gate=b_gate, g_q_norm=g_q_norm, w_uq=w_uq,
                        g_kv_norm=g_kv_norm, w_uk=w_uk, w_uv=w_uv, w_o_mla=w_o_mla, mu_shift=mu_shift,
                        w0_decay=w0_decay, w_decay_up=w_decay_up, a0=a0, w_aaa_up=w_aaa_up, k_k=k_k, k_a=k_a,
                        r_k=r_k, ln_x_w=ln_x_w, ln_x_b=ln_x_b, w_o_rwkv=w_o_rwkv, g_mem_norm=g_mem_norm,
                        w_mem_k=w_mem_k, w_mem_v=w_mem_v, w_o_mem=w_o_mem, w_out=w_out, g_ffn_norm=g_ffn_norm,
                        w_router=w_router, b_router=b_router, w_mlp1=w_mlp1, b_mlp1=b_mlp1,
                        w_mlp2=w_mlp2, b_mlp2=b_mlp2)
    depth = w_in.shape[0]
    assert depth == 1, "single-layer trunk"
    W = {name: arr.reshape(arr.shape[1:]) for name, arr in layer_params.items()}
    B, T, D = x_prompt.shape
    DB, TS, _ = x_sample.shape
    assert TS == 1, "sample group decodes one token per sequence"
    n_pages = page_table.shape[1]
    page = cache_kv_latent.shape[2]
    past_len = n_pages * page
    kv_lora = w_uk.shape[1]
    mem_tokens = mem_prompt.shape[1]
    mem_w = MEM_HEADS * MEM_HEAD
    n_e = w_router.shape[2]
    rw_w = RWKV_HEADS * RWKV_HEAD

    tm = min(PROJ_TM, T)
    xp = x_prompt.reshape(B * T, D)
    xs = x_sample.reshape(DB, D)

    tab_p = _rope_tables(jnp.arange(T, dtype=jnp.int32))
    ckv_p, kpe_p, q_h, k_h, v_h = mla_proj_prompt(xp, tab_p, W, tm)
    rw_p, qm_p, gates_p = in_proj_rest(xp, W, tm)
    mk_p, mv_p = mem_kv(mem_prompt.reshape(B * mem_tokens, D), W)
    tq = min(FLASH_TQ, T)
    oa_p = flash_prompt(q_h, k_h, v_h, B, tq)
    oc_p = mem_attn_prompt(qm_p, mk_p.reshape(B, mem_tokens, mem_w), mv_p.reshape(B, mem_tokens, mem_w), tm)
    ob_p, z_p = rwkv_prompt(rw_p, B, W)
    woa_p = jnp.pad(W['w_o_mla'].reshape(MLA_HEADS, V_HEAD, D),
                    ((0, 0), (0, LANES - V_HEAD), (0, 0))).reshape(MLA_HEADS * LANES, D)
    h_p, hn_p, ti_p, tg_p = merge_out(oa_p, woa_p, ob_p, oc_p, gates_p, xp, W, tm)

    tab_s = _rope_tables(jnp.full((DB,), past_len, jnp.int32))
    ckv_s, kpe_s, qlat_s, qpe_s = mla_proj_sample(xs, tab_s, W)
    rw_s, qm_s, gates_s = in_proj_rest(xs, W, DB)
    oa_s = decode_attn(jnp.transpose(qlat_s, (1, 0, 2)), jnp.transpose(qpe_s, (1, 0, 2)),
                       ckv_s.reshape(DB, 1, kv_lora), kpe_s.reshape(DB, 1, QK_ROPE),
                       cache_kv_latent, jnp.swapaxes(cache_k_rope, 2, 3), page_table,
                       W['w_uv']).reshape(DB, MLA_HEADS * V_HEAD)
    oc_s = mem_attn_sample(qm_s.reshape(DB, MEM_HEADS, MEM_HEAD),
                           cache_mem_k.reshape(DB, mem_tokens, MEM_HEADS, MEM_HEAD),
                           cache_mem_v.reshape(DB, mem_tokens, MEM_HEADS, MEM_HEAD)).reshape(DB, mem_w)
    ob_s, st_s = rwkv_sample(rw_s, state_rwkv_shift.reshape(DB, -1),
                             state_rwkv.reshape(DB, RWKV_HEADS, RWKV_HEAD, RWKV_HEAD), W)
    h_s, hn_s, ti_s, tg_s = merge_out(oa_s.astype(BF16), W['w_o_mla'], ob_s, oc_s, gates_s, xs, W, DB)

    hn_all = jnp.concatenate([hn_p, hn_s], axis=0)
    top_i = jnp.concatenate([ti_p, ti_s], axis=0)
    dest, row_tok, block_e, n_used = _routing(top_i, n_e, MOE_TM)
    yb = grouped_mlp(hn_all, row_tok, block_e, n_used, W, MOE_TM)
    n_p = B * T
    y_p = moe_combine(yb, dest[:n_p * TOP_K], tg_p, h_p, g_final, min(128, T))
    y_s = moe_combine(yb, dest[n_p * TOP_K:], tg_s, h_s, g_final, DB)

    zq = z_p.reshape(B, RWKV_HEADS // RWKV_GROUP, RWKV_GROUP, RWKV_HEAD, RWKV_GROUP, RWKV_HEAD)
    st_p = jnp.stack([zq[:, :, q, :, q, :] for q in range(RWKV_GROUP)], axis=2)
    st_p = jnp.swapaxes(st_p.reshape(B, RWKV_HEADS, RWKV_HEAD, RWKV_HEAD), -1, -2)
    lead = lambda a: a[None]
    return (y_p.reshape(B, T, D), y_s.reshape(DB, TS, D),
            lead(ckv_p.reshape(B, T, kv_lora)), lead(kpe_p.reshape(B, T, QK_ROPE)),
            lead(st_p), lead(rw_p.reshape(B, T, -1)[:, -1]),
            lead(mk_p.reshape(B, mem_tokens, MEM_HEADS, MEM_HEAD)),
            lead(mv_p.reshape(B, mem_tokens, MEM_HEADS, MEM_HEAD)),
            lead(ckv_s.reshape(DB, TS, kv_lora)), lead(kpe_s.reshape(DB, TS, QK_ROPE)),
            lead(st_s), lead(rw_s))
```

```python
import functools
import math

import numpy as np
import jax
import jax.numpy as jnp
from jax import lax
from jax.experimental import pallas as pl
from jax.experimental.pallas import tpu as pltpu

F32 = jnp.float32
BF16 = jnp.bfloat16

MLA_HEADS = 8
QK_NOPE = 64
QK_ROPE = 32
V_HEAD = 64
ROPE_THETA = 10000.0
RWKV_HEADS = 8
RWKV_HEAD = 64
DECAY_LORA = 64
AAA_LORA = 64
MEM_HEADS = 4
MEM_HEAD = 128
TOP_K = 4
SWIGLU_ALPHA = 1.702
SWIGLU_LIMIT = 7.0
RMS_EPS = 1e-6
GN_EPS = 64e-5
L2_EPS = 1e-12

LANES = 128
SUBLANES = 8
VMEM_LIMIT = 48 * 1024 * 1024
GMM_VMEM_LIMIT = 58 * 1024 * 1024

RWKV_CHUNK = 64
RWKV_GROUP = 2
PROJ_TM = 512
MOE_TM = 512
GMM_BUFS = 3
DECODE_SPLITS = 4
FLASH_TQ = 1024
FLASH_SUBTILES = 2
FLASH_HEADS = 1


def _cparams(sem, vmem=VMEM_LIMIT):
    return pltpu.CompilerParams(dimension_semantics=sem, vmem_limit_bytes=vmem)


def _rms(x, g, eps=RMS_EPS):
    return x * lax.rsqrt(jnp.mean(x * x, axis=-1, keepdims=True) + eps) * g


def _dot(a, b):
    return jnp.dot(a.astype(BF16), b.astype(BF16), preferred_element_type=F32)


def _dot_nt(a, b):
    return lax.dot_general(a.astype(BF16), b.astype(BF16), (((1,), (1,)), ((), ())),
                           preferred_element_type=F32)


def _split(a):
    hi = a.astype(BF16)
    lo = (a - hi.astype(F32)).astype(BF16)
    return hi, lo


def _dot3(a, b):
    ah, al = _split(a)
    bh, bl = _split(b)
    return (jnp.dot(ah, bh, preferred_element_type=F32)
            + (jnp.dot(ah, bl, preferred_element_type=F32)
               + jnp.dot(al, bh, preferred_element_type=F32)))


def _dot2_exact_rhs(a, b):
    ah, al = _split(a)
    bb = b.astype(BF16)
    return jnp.dot(ah, bb, preferred_element_type=F32) + jnp.dot(al, bb, preferred_element_type=F32)


def _dot2_exact_lhs(a, b):
    aa = a.astype(BF16)
    bh, bl = _split(b)
    return jnp.dot(aa, bh, preferred_element_type=F32) + jnp.dot(aa, bl, preferred_element_type=F32)


def _store_row_tiles(ref, x):
    rows, width = x.shape
    assert width == SUBLANES * LANES
    for s in range(SUBLANES):
        ref[pl.ds(s, rows, stride=SUBLANES), :] = x[:, LANES * s:LANES * (s + 1)]


def _load_row_tiles(ref, rows):
    return jnp.concatenate([ref[pl.ds(s, rows, stride=SUBLANES), :] for s in range(SUBLANES)], axis=1)


def _sigmoid(x):
    return 1.0 / (1.0 + jnp.exp(-x))


def _mla_proj_prompt_kernel(x_ref, ga_ref, wcq_ref, gq_ref, wckv_ref, gkv_ref,
                            wkp_ref, wkpr_ref, wkpp_ref, wkprp_ref,
                            wq_ref, wqr_ref, wuk_ref, wuv_ref,
                            cos32_ref, sin32_ref, cost_ref, sint_ref,
                            ckv_out, kpe_out, q_out, k_out, v_out, *, scale):
    xn = _rms(x_ref[...], ga_ref[...]).astype(BF16)
    cqn = _rms(jnp.dot(xn, wcq_ref[...], preferred_element_type=F32), gq_ref[...]).astype(BF16)
    ckvn = _rms(jnp.dot(xn, wckv_ref[...], preferred_element_type=F32), gkv_ref[...])
    ckv_out[...] = ckvn
    kpe_out[...] = (jnp.dot(xn, wkp_ref[...], preferred_element_type=F32) * cos32_ref[...]
                    + jnp.dot(xn, wkpr_ref[...], preferred_element_type=F32) * sin32_ref[...])
    cost = cost_ref[...]
    sint = sint_ref[...]
    kpp = (jnp.dot(xn, wkpp_ref[...], preferred_element_type=F32) * cost
           + jnp.dot(xn, wkprp_ref[...], preferred_element_type=F32) * sint)
    qa = jnp.dot(cqn, wq_ref[...], preferred_element_type=F32)
    qb = jnp.dot(cqn, wqr_ref[...], preferred_element_type=F32)
    ckb = ckvn.astype(BF16)
    ka = jnp.dot(ckb, wuk_ref[...], preferred_element_type=F32)
    va = jnp.dot(ckb, wuv_ref[...], preferred_element_type=F32)
    ones_lane = lax.broadcasted_iota(jnp.int32, cost.shape, 1) == V_HEAD
    for h in range(MLA_HEADS):
        sl = slice(LANES * h, LANES * (h + 1))
        q_out[h] = ((qa[:, sl] * cost + qb[:, sl] * sint) * scale).astype(BF16)
        k_out[h] = (ka[:, sl] + kpp).astype(BF16)
        v_out[h] = jnp.where(ones_lane, 1.0, va[:, sl]).astype(BF16)


def _mla_proj_sample_kernel(x_ref, ga_ref, wcq_ref, gq_ref, wckv_ref, gkv_ref,
                            wkp_ref, wkpr_ref, wqn_ref, wukt_ref, wqp_ref, wqpr_ref,
                            cos32_ref, sin32_ref,
                            ckv_out, kpe_out, qlat_out, qpe_out, *, scale):
    xn = _rms(x_ref[...], ga_ref[...]).astype(BF16)
    cqn = _rms(jnp.dot(xn, wcq_ref[...], preferred_element_type=F32), gq_ref[...]).astype(BF16)
    ckv_out[...] = _rms(jnp.dot(xn, wckv_ref[...], preferred_element_type=F32), gkv_ref[...])
    cos32 = cos32_ref[...]
    sin32 = sin32_ref[...]
    kpe_out[...] = (jnp.dot(xn, wkp_ref[...], preferred_element_type=F32) * cos32
                    + jnp.dot(xn, wkpr_ref[...], preferred_element_type=F32) * sin32)
    qn = jnp.dot(cqn, wqn_ref[...], preferred_element_type=F32)
    for h in range(MLA_HEADS):
        sl = slice(LANES * h, LANES * (h + 1))
        qlat_out[h] = (_dot(qn[:, sl], wukt_ref[h]) * scale).astype(BF16)
        qpe_out[h] = ((jnp.dot(cqn, wqp_ref[h], preferred_element_type=F32) * cos32
                       + jnp.dot(cqn, wqpr_ref[h], preferred_element_type=F32) * sin32) * scale)


def _full(shape):
    nd = len(shape)
    return pl.BlockSpec(shape, lambda *_: (0,) * nd, pipeline_mode=pl.Buffered(1))


def _rope_tables(pos):
    half = QK_ROPE // 2
    inv = jnp.exp(-math.log(ROPE_THETA) * jnp.arange(half, dtype=F32) / half)
    ang = pos.astype(F32)[:, None] * inv[None, :]
    cos = jnp.cos(ang)
    sin = jnp.sin(ang)
    return jnp.concatenate([cos, cos], axis=1), jnp.concatenate([sin, sin], axis=1)


def _rot_cols(w):
    half = QK_ROPE // 2
    return jnp.concatenate([-w[..., half:], w[..., :half]], axis=-1)


def _pad_cols(w, lo, total=LANES):
    pad = [(0, 0)] * (w.ndim - 1) + [(lo, total - lo - w.shape[-1])]
    return jnp.pad(w, pad)


def _mla_weights(W):
    d = W['w_in'].shape[0]
    q_lora = W['w_uq'].shape[0]
    kv_lora = W['w_uk'].shape[0]
    c0, c1, c2 = q_lora, q_lora + kv_lora, q_lora + kv_lora + QK_ROPE
    w_in = W['w_in']
    wcq = w_in[:, :c0].astype(BF16)
    wckv = w_in[:, c0:c1].astype(BF16)
    wkp = w_in[:, c1:c2]
    wuq = W['w_uq'].reshape(q_lora, MLA_HEADS, QK_NOPE + QK_ROPE)
    return dict(d=d, q_lora=q_lora, kv_lora=kv_lora, wcq=wcq, wckv=wckv, wkp=wkp, wuq=wuq,
                ga=W['g_attn_norm'].reshape(1, d), gq=W['g_q_norm'].reshape(1, q_lora),
                gkv=W['g_kv_norm'].reshape(1, kv_lora))


def mla_proj_prompt(x, pos_tables, W, tm):
    n, d = x.shape
    m = _mla_weights(W)
    q_lora, kv_lora = m['q_lora'], m['kv_lora']
    cos32, sin32 = pos_tables
    t = cos32.shape[0]
    ones = jnp.ones((t, QK_NOPE), F32)
    zeros_hi = jnp.zeros((t, LANES - QK_NOPE - QK_ROPE), F32)
    cost = jnp.concatenate([ones, cos32, zeros_hi], axis=1)
    sint = jnp.concatenate([jnp.zeros((t, QK_NOPE), F32), sin32, zeros_hi], axis=1)
    wkp = m['wkp']
    wkpr = _rot_cols(wkp)
    wuq = m['wuq']
    wq = _pad_cols(wuq, 0).reshape(q_lora, MLA_HEADS * LANES).astype(BF16)
    wqr = _pad_cols(_rot_cols(wuq[..., QK_NOPE:]), QK_NOPE).reshape(q_lora, MLA_HEADS * LANES).astype(BF16)
    wuk = _pad_cols(W['w_uk'], 0).reshape(kv_lora, MLA_HEADS * LANES).astype(BF16)
    wuv = _pad_cols(W['w_uv'], 0).reshape(kv_lora, MLA_HEADS * LANES).astype(BF16)
    args = (x, m['ga'], m['wcq'], m['gq'], m['wckv'], m['gkv'],
            wkp.astype(BF16), wkpr.astype(BF16),
            _pad_cols(wkp, QK_NOPE).astype(BF16), _pad_cols(wkpr, QK_NOPE).astype(BF16),
            wq, wqr, wuk, wuv, cos32, sin32, cost, sint)
    nt = t // tm
    row = lambda i: (i, 0)
    tab = lambda i: (i % nt, 0)
    in_specs = [pl.BlockSpec((tm, d), row)] + [_full(a.shape) for a in args[1:14]] + [
        pl.BlockSpec((tm, QK_ROPE), tab), pl.BlockSpec((tm, QK_ROPE), tab),
        pl.BlockSpec((tm, LANES), tab), pl.BlockSpec((tm, LANES), tab)]
    head = pl.BlockSpec((MLA_HEADS, tm, LANES), lambda i: (0, i, 0))
    out_shape = (jax.ShapeDtypeStruct((n, kv_lora), F32), jax.ShapeDtypeStruct((n, QK_ROPE), F32),
                 jax.ShapeDtypeStruct((MLA_HEADS, n, LANES), BF16),
                 jax.ShapeDtypeStruct((MLA_HEADS, n, LANES), BF16),
                 jax.ShapeDtypeStruct((MLA_HEADS, n, LANES), BF16))
    out_specs = (pl.BlockSpec((tm, kv_lora), row), pl.BlockSpec((tm, QK_ROPE), row), head, head, head)
    scale = (QK_NOPE + QK_ROPE) ** -0.5 * math.log2(math.e)
    return pl.pallas_call(
        functools.partial(_mla_proj_prompt_kernel, scale=scale),
        grid=(n // tm,), in_specs=in_specs, out_specs=out_specs, out_shape=out_shape,
        compiler_params=_cparams(("parallel",)), name="mla_proj_prompt")(*args)


def mla_proj_sample(x, pos_tables, W):
    n, d = x.shape
    m = _mla_weights(W)
    q_lora, kv_lora = m['q_lora'], m['kv_lora']
    cos32, sin32 = pos_tables
    wkp = m['wkp']
    wuq = m['wuq']
    wqn = _pad_cols(wuq[..., :QK_NOPE], 0).reshape(q_lora, MLA_HEADS * LANES).astype(BF16)
    wukt = jnp.pad(jnp.transpose(W['w_uk'], (1, 2, 0)), ((0, 0), (0, LANES - QK_NOPE), (0, 0))).astype(BF16)
    wqp = jnp.transpose(wuq[..., QK_NOPE:], (1, 0, 2))
    args = (x, m['ga'], m['wcq'], m['gq'], m['wckv'], m['gkv'],
            wkp.astype(BF16), _rot_cols(wkp).astype(BF16), wqn, wukt,
            wqp.astype(BF16), _rot_cols(wqp).astype(BF16), cos32, sin32)
    out_shape = (jax.ShapeDtypeStruct((n, kv_lora), F32), jax.ShapeDtypeStruct((n, QK_ROPE), F32),
                 jax.ShapeDtypeStruct((MLA_HEADS, n, kv_lora), BF16),
                 jax.ShapeDtypeStruct((MLA_HEADS, n, QK_ROPE), F32))
    scale = (QK_NOPE + QK_ROPE) ** -0.5
    return pl.pallas_call(
        functools.partial(_mla_proj_sample_kernel, scale=scale),
        grid=(1,), in_specs=[_full(a.shape) for a in args],
        out_specs=tuple(_full(s.shape) for s in out_shape), out_shape=out_shape,
        compiler_params=_cparams(("arbitrary",)), name="mla_proj_sample")(*args)


def _in_proj_rest_kernel(x_ref, ga_ref, wrw_ref, wqm_ref, wg_ref, bg_ref,
                         rw_out, qm_out, g_out, *, mem_scale):
    xn = _rms(x_ref[...], ga_ref[...]).astype(BF16)
    rw_out[...] = jnp.dot(xn, wrw_ref[...], preferred_element_type=F32)
    qm_out[...] = (jnp.dot(xn, wqm_ref[...], preferred_element_type=F32) * mem_scale).astype(BF16)
    g_out[...] = _sigmoid(jnp.dot(xn, wg_ref[...], preferred_element_type=F32) + bg_ref[...]).astype(BF16)


def in_proj_rest(x, W, tm):
    n, d = x.shape
    q_lora = W['w_uq'].shape[0]
    kv_lora = W['w_uk'].shape[0]
    rw_cols = W['mu_shift'].shape[0]
    mem_w = MEM_HEADS * MEM_HEAD
    c2 = q_lora + kv_lora + QK_ROPE
    c3 = c2 + rw_cols
    c4 = c3 + mem_w
    w_in = W['w_in']
    n_gate = w_in.shape[1] - c4
    args = (x, W['g_attn_norm'].reshape(1, d), w_in[:, c2:c3].astype(BF16), w_in[:, c3:c4].astype(BF16),
            w_in[:, c4:].astype(BF16), W['b_gate'].reshape(1, n_gate))
    row = lambda i: (i, 0)
    out_shape = (jax.ShapeDtypeStruct((n, rw_cols), F32), jax.ShapeDtypeStruct((n, mem_w), BF16),
                 jax.ShapeDtypeStruct((n, n_gate), BF16))
    return pl.pallas_call(
        functools.partial(_in_proj_rest_kernel, mem_scale=MEM_HEAD ** -0.5),
        grid=(n // tm,),
        in_specs=[pl.BlockSpec((tm, d), row)] + [_full(a.shape) for a in args[1:]],
        out_specs=(pl.BlockSpec((tm, rw_cols), row), pl.BlockSpec((tm, mem_w), row),
                   pl.BlockSpec((tm, n_gate), row)),
        out_shape=out_shape, compiler_params=_cparams(("parallel",)), name="in_proj_rest")(*args)


def _mem_kv_kernel(m_ref, g_ref, wk_ref, wv_ref, k_out, v_out):
    mn = _rms(m_ref[...], g_ref[...]).astype(BF16)
    k_out[...] = jnp.dot(mn, wk_ref[...], preferred_element_type=F32)
    v_out[...] = jnp.dot(mn, wv_ref[...], preferred_element_type=F32)


def mem_kv(mem, W):
    n, d = mem.shape
    mem_w = MEM_HEADS * MEM_HEAD
    args = (mem, W['g_mem_norm'].reshape(1, d), W['w_mem_k'].astype(BF16), W['w_mem_v'].astype(BF16))
    out_shape = (jax.ShapeDtypeStruct((n, mem_w), F32), jax.ShapeDtypeStruct((n, mem_w), F32))
    return pl.pallas_call(
        _mem_kv_kernel, grid=(1,), in_specs=[_full(a.shape) for a in args],
        out_specs=(_full((n, mem_w)), _full((n, mem_w))), out_shape=out_shape,
        compiler_params=_cparams(("arbitrary",)), name="mem_kv")(*args)


def _flash_kernel(q_ref, k_ref, v_ref, o_ref, *scratch, tq, n_sub):
    qi = pl.program_id(2)
    sub = tq // n_sub
    nh, _, dh = q_ref.shape
    n_ch = nh * n_sub
    m_scs = scratch[:n_ch]
    acc_scs = scratch[n_ch:2 * n_ch]
    s_scs = scratch[2 * n_ch:]
    for ch in range(n_ch):
        m_scs[ch][...] = jnp.full(m_scs[ch].shape, -jnp.inf, F32)
        acc_scs[ch][...] = jnp.zeros(acc_scs[ch].shape, F32)

    def scores(j, slot):
        for hd in range(nh):
            kb = k_ref[hd, pl.ds(pl.multiple_of(j * tq, tq), tq), :]
            for u in range(n_sub):
                s_scs[slot][hd * n_sub + u] = lax.dot_general(
                    q_ref[hd, u * sub:(u + 1) * sub, :], kb, (((1,), (1,)), ((), ())), preferred_element_type=F32)

    def consume(j, slot, masked):
        for hd in range(nh):
            vb = v_ref[hd, pl.ds(pl.multiple_of(j * tq, tq), tq), :]
            for u in range(n_sub):
                ch = hd * n_sub + u
                s = s_scs[slot][ch]
                if masked:
                    rows = u * sub + lax.broadcasted_iota(jnp.int32, (sub, tq), 0)
                    cols = lax.broadcasted_iota(jnp.int32, (sub, tq), 1)
                    s = jnp.where(cols <= rows, s, -jnp.inf)
                m_prev = m_scs[ch][...]
                m_new = jnp.maximum(m_prev, jnp.max(s, axis=-1, keepdims=True))
                alpha = jnp.exp2(m_prev - m_new)
                p = jnp.exp2(s - jnp.tile(m_new, (1, tq // LANES)))
                acc_scs[ch][...] = (alpha * acc_scs[ch][...]
                                    + jnp.dot(p.astype(BF16), vb, preferred_element_type=F32))
                m_scs[ch][...] = m_new

    scores(0, 0)

    def body(i, carry):
        scores(2 * i + 1, 1)
        consume(2 * i, 0, False)
        scores(2 * i + 2, 0)
        consume(2 * i + 1, 1, False)
        return carry

    lax.fori_loop(0, qi // 2, body, 0)

    @pl.when(qi % 2 == 1)
    def _():
        scores(qi, 1)
        consume(qi - 1, 0, False)
        consume(qi, 1, True)

    @pl.when(qi % 2 == 0)
    def _():
        consume(qi, 0, True)

    for hd in range(nh):
        for u in range(n_sub):
            acc = acc_scs[hd * n_sub + u][...]
            o_ref[u * sub:(u + 1) * sub, hd * dh:(hd + 1) * dh] = (
                acc / acc[:, V_HEAD:V_HEAD + 1]).astype(o_ref.dtype)


def flash_prompt(q, k, v, batch, tq):
    h, n, dh = q.shape
    t = n // batch
    nq = t // tq
    nh = FLASH_HEADS
    kv_spec = pl.BlockSpec((nh, t, dh), lambda b, hh, i: (hh, b, 0))
    n_sub = FLASH_SUBTILES
    n_ch = nh * n_sub
    sub = tq // n_sub
    return pl.pallas_call(
        functools.partial(_flash_kernel, tq=tq, n_sub=n_sub), grid=(batch, h // nh, nq),
        in_specs=[pl.BlockSpec((nh, tq, dh), lambda b, hh, i: (hh, b * nq + i, 0)), kv_spec, kv_spec],
        out_specs=pl.BlockSpec((tq, nh * dh), lambda b, hh, i: (b * nq + i, hh)),
        out_shape=jax.ShapeDtypeStruct((n, h * dh), BF16),
        scratch_shapes=([pltpu.VMEM((sub, LANES), F32)] * n_ch + [pltpu.VMEM((sub, dh), F32)] * n_ch
                        + [pltpu.VMEM((n_ch, sub, tq), F32)] * 2),
        compiler_params=_cparams(("parallel", "parallel", "arbitrary")), name="flash_prompt")(q, k, v)


def _decode_kernel(pt_ref, qlat_ref, qpe_ref, ckv_ref, kpe_ref, wuv_ref, lat_hbm, rope_hbm, o_ref,
                   lat_buf, rope_buf, sem, *, n_pages, page):
    b = pl.program_id(0)

    def page_copies(seq, slot, i):
        pg = pt_ref[seq, i]
        rows = pl.ds(i * page, page)
        return (pltpu.make_async_copy(lat_hbm.at[0, pg], lat_buf.at[slot, rows], sem.at[slot]),
                pltpu.make_async_copy(rope_hbm.at[0, pg], rope_buf.at[slot, :, rows], sem.at[slot]))

    def start_all(seq, slot):
        for i in range(n_pages):
            for c in page_copies(seq, slot, i):
                c.start()

    @pl.when(b == 0)
    def _():
        start_all(0, 0)

    @pl.when(b + 1 < pl.num_programs(0))
    def _():
        start_all(b + 1, (b + 1) % 2)

    slot = b % 2

    def drain(i, carry):
        for c in page_copies(b, slot, 0):
            c.wait()
        return carry

    lax.fori_loop(0, n_pages, drain, 0)

    qlat = qlat_ref[0]
    qpe = qpe_ref[0]
    ckv = ckv_ref[0]
    kpe = kpe_ref[0]
    s_new = (jnp.sum(qlat.astype(F32) * ckv, axis=-1, keepdims=True)
             + jnp.sum(qpe * kpe, axis=-1, keepdims=True))
    past = n_pages * page
    n_split = DECODE_SPLITS if past % (DECODE_SPLITS * LANES) == 0 else 1
    span = past // n_split
    kparts = [lat_buf[slot, c * span:(c + 1) * span, :].astype(BF16) for c in range(n_split)]
    sparts = [_dot_nt(qlat, kparts[c]) + _dot(qpe, rope_buf[slot, :, c * span:(c + 1) * span])
              for c in range(n_split)]
    mparts = [jnp.max(sp, axis=-1, keepdims=True) for sp in sparts]
    pparts = [jnp.exp(sp - mp) for sp, mp in zip(sparts, mparts)]
    oparts = [jnp.dot(pp.astype(BF16), kp, preferred_element_type=F32) for pp, kp in zip(pparts, kparts)]
    m = s_new
    for mp in mparts:
        m = jnp.maximum(m, mp)
    p_new = jnp.exp(s_new - m)
    l = p_new
    o_acc = p_new * ckv
    for mp, pp, op in zip(mparts, pparts, oparts):
        w = jnp.exp(mp - m)
        l = l + w * jnp.sum(pp, axis=-1, keepdims=True)
        o_acc = o_acc + w * op
    o_lat = o_acc / l
    ov = _dot(o_lat, wuv_ref[...])
    hh = lax.broadcasted_iota(jnp.int32, ov.shape, 0)
    cc = lax.broadcasted_iota(jnp.int32, ov.shape, 1) // V_HEAD
    o_ref[0] = jnp.sum(jnp.where(hh == cc, ov, 0.0), axis=0, keepdims=True)


def decode_attn(qlat, qpe, ckv, kpe, cache_lat, cache_rope, page_table, w_uv):
    db, h, kv = qlat.shape
    _, n_pool, page, _ = cache_lat.shape
    n_pages = page_table.shape[1]
    past = n_pages * page
    wuv = w_uv.reshape(kv, h * V_HEAD).astype(BF16)
    per_b = lambda shape: pl.BlockSpec((1,) + shape, lambda b, pt: (b, 0, 0))
    in_specs = [per_b((h, kv)), per_b((h, QK_ROPE)), per_b((1, kv)), per_b((1, QK_ROPE)),
                pl.BlockSpec(wuv.shape, lambda b, pt: (0, 0)),
                pl.BlockSpec(memory_space=pl.ANY), pl.BlockSpec(memory_space=pl.ANY)]
    grid_spec = pltpu.PrefetchScalarGridSpec(
        num_scalar_prefetch=1, grid=(db,), in_specs=in_specs,
        out_specs=per_b((1, h * V_HEAD)),
        scratch_shapes=[pltpu.VMEM((2, past, kv), F32), pltpu.VMEM((2, QK_ROPE, past), F32),
                        pltpu.SemaphoreType.DMA((2,))])
    return pl.pallas_call(
        functools.partial(_decode_kernel, n_pages=n_pages, page=page), grid_spec=grid_spec,
        out_shape=jax.ShapeDtypeStruct((db, 1, h * V_HEAD), F32),
        compiler_params=_cparams(("arbitrary",)), name="decode_attn")(
            page_table, qlat, qpe, ckv, kpe, wuv, cache_lat, cache_rope)


def _mem_attn_prompt_kernel(q_ref, k_ref, v_ref, o_ref):
    q = q_ref[...]
    k = k_ref[0].astype(BF16)
    v = v_ref[0].astype(BF16)
    for h in range(MEM_HEADS):
        sl = slice(MEM_HEAD * h, MEM_HEAD * (h + 1))
        s = _dot_nt(q[:, sl], k[:, sl])
        p = jnp.exp(s - jnp.max(s, axis=-1, keepdims=True))
        o = jnp.dot(p.astype(BF16), v[:, sl], preferred_element_type=F32) / jnp.sum(p, axis=-1, keepdims=True)
        o_ref[:, sl] = o.astype(o_ref.dtype)


def mem_attn_prompt(qm, mk, mv, tm):
    n, w = qm.shape
    b, m, _ = mk.shape
    nt = n // b // tm
    kv_spec = pl.BlockSpec((1, m, w), lambda i: (i // nt, 0, 0))
    return pl.pallas_call(
        _mem_attn_prompt_kernel, grid=(n // tm,),
        in_specs=[pl.BlockSpec((tm, w), lambda i: (i, 0)), kv_spec, kv_spec],
        out_specs=pl.BlockSpec((tm, w), lambda i: (i, 0)),
        out_shape=jax.ShapeDtypeStruct((n, w), BF16),
        compiler_params=_cparams(("parallel",)), name="mem_attn_prompt")(qm, mk, mv)


def _mem_attn_sample_kernel(q_ref, k_ref, v_ref, o_ref, *, tb):
    for bl in range(tb):
        q = q_ref[bl].astype(F32)
        s = jnp.sum(k_ref[bl] * q, axis=-1, keepdims=True)
        p = jnp.exp(s - jnp.max(s, axis=0, keepdims=True))
        o = jnp.sum(p * v_ref[bl], axis=0) / jnp.sum(p, axis=0)
        o_ref[bl] = o.astype(o_ref.dtype)


def mem_attn_sample(qm, mk, mv, tb=8):
    db, nh, hd = qm.shape
    m = mk.shape[1]
    kv_spec = pl.BlockSpec((tb, m, nh, hd), lambda i: (i, 0, 0, 0))
    q_spec = pl.BlockSpec((tb, nh, hd), lambda i: (i, 0, 0))
    return pl.pallas_call(
        functools.partial(_mem_attn_sample_kernel, tb=tb), grid=(db // tb,),
        in_specs=[q_spec, kv_spec, kv_spec], out_specs=q_spec,
        out_shape=jax.ShapeDtypeStruct((db, nh, hd), BF16),
        compiler_params=_cparams(("parallel",)), name="mem_attn_sample")(qm, mk, mv)


def _rwkv_prep(xr, lora_w, w0, a0, kk_w, ka_w, gmat):
    w = RWKV_HEADS * RWKV_HEAD
    r = xr[:, 0:w]
    k = xr[:, w:2 * w]
    v = xr[:, 2 * w:3 * w]
    tail = xr[:, 3 * w:3 * w + DECAY_LORA + AAA_LORA]
    lane = lax.broadcasted_iota(jnp.int32, tail.shape, 1)
    z = jnp.where(lane < DECAY_LORA, jnp.tanh(tail), tail)
    lo = _dot3(z, lora_w)
    dec = lo[:, :w] + w0
    aaa = lo[:, w:] + a0
    u = -dec
    softplus = jnp.maximum(u, 0.0) + jnp.log(1.0 + jnp.exp(-jnp.abs(u)))
    logw = -jnp.exp(-softplus - 0.5)
    a_sig = _sigmoid(aaa)
    kk = k * kk_w
    norm = jnp.sqrt(_dot2_exact_rhs(kk * kk, gmat))
    kkn = kk / jnp.maximum(norm, L2_EPS)
    k2 = k * (1.0 + (a_sig - 1.0) * ka_w)
    return r, k2, v, logw, -kkn, kkn * a_sig


def _rwkv_post(y, r, k2, v, rk, lnw, lnb, gmat):
    inv_n = 1.0 / RWKV_HEAD
    mu = _dot2_exact_rhs(y, gmat) * inv_n
    d = y - mu
    var = _dot2_exact_rhs(d * d, gmat) * inv_n
    yn = d * lax.rsqrt(var + GN_EPS) * lnw + lnb
    return yn + _dot2_exact_rhs(r * k2 * rk, gmat) * v


def _rwkv_prompt_kernel(rw_ref, mu_ref, lora_ref, w0_ref, a0_ref, kk_ref, ka_ref, rk_ref,
                        lnw_ref, lnb_ref, g_ref, out_ref, zout_ref, prev_sc, z_sc, *, chunk):
    c = chunk
    nb = rw_ref.shape[0]

    @pl.when(pl.program_id(0) == 0)
    def _():
        prev_sc[...] = jnp.zeros(prev_sc.shape, F32)
        z_sc[...] = jnp.zeros(z_sc.shape, F32)

    rw = jnp.concatenate([rw_ref[b] for b in range(nb)], axis=0)
    row = lax.broadcasted_iota(jnp.int32, rw.shape, 0)
    prev = pltpu.roll(rw, 1, 0)
    for b in range(nb):
        prev = jnp.where(row == b * c, prev_sc[b:b + 1, :], prev)
        prev_sc[b:b + 1, :] = rw[(b + 1) * c - 1:(b + 1) * c, :]
    xr = rw + (prev - rw) * mu_ref[...]
    gmat = g_ref[...]
    r, k2, v, logw, av, bv = _rwkv_prep(xr, lora_ref[...], w0_ref[...], a0_ref[...],
                                        kk_ref[...], ka_ref[...], gmat)

    ti = lax.broadcasted_iota(jnp.int32, (nb * c, nb * c), 0)
    si = lax.broadcasted_iota(jnp.int32, (nb * c, nb * c), 1)
    causal = jnp.logical_and(si <= ti, (si // c) == (ti // c))
    cum = _dot2_exact_lhs(causal.astype(F32), logw)
    clast = jnp.concatenate([jnp.broadcast_to(cum[(b + 1) * c - 1:(b + 1) * c, :], (c, cum.shape[1]))
                             for b in range(nb)], axis=0)
    e_neg = jnp.exp(-cum)
    e_rest = jnp.exp(clast - cum)
    at = av * jnp.exp(cum - logw)
    bt = bv * e_neg
    kt = k2 * e_neg
    rt = r * jnp.exp(cum)
    bg = bv * e_rest
    kg = k2 * e_rest
    glast = jnp.exp(clast)

    g = RWKV_GROUP
    gl = g * RWKV_HEAD
    gc = g * c
    n_groups = RWKV_HEADS // g
    emask = (lax.broadcasted_iota(jnp.int32, (gc, gl), 0) // c
             == lax.broadcasted_iota(jnp.int32, (gc, gl), 1) // RWKV_HEAD)
    br = lax.broadcasted_iota(jnp.int32, (gc, gc), 0)
    bc = lax.broadcasted_iota(jnp.int32, (gc, gc), 1)
    same = (br // c) == (bc // c)
    strict = jnp.logical_and(same, (br % c) > (bc % c))
    incl = jnp.logical_and(same, (br % c) >= (bc % c))
    eye_gc = (br == bc).astype(F32)
    lane_head = lax.broadcasted_iota(jnp.int32, (c, gl), 1) // RWKV_HEAD
    eye_gl = (lax.broadcasted_iota(jnp.int32, (gl, gl), 0) == lax.broadcasted_iota(jnp.int32, (gl, gl), 1))

    def dup(x):
        return jnp.concatenate([x] * g, axis=0)

    def expand(x):
        return jnp.where(emask, dup(x), 0.0)

    def per_head(x):
        return jnp.concatenate([jnp.where(lane_head == q, x, 0.0) for q in range(g)], axis=0)

    items = [(b, p) for b in range(nb) for p in range(n_groups)]

    def part(x, item):
        b, p = item
        return x[b * c:(b + 1) * c, gl * p:gl * (p + 1)]

    aps = []
    for it in items:
        lhs = jnp.concatenate([part(at, it), part(rt, it)], axis=0)
        rhs = jnp.concatenate([per_head(part(bt, it)), per_head(part(kt, it))], axis=0)
        aps.append(_dot_nt(lhs, rhs))
    nmats = [jnp.where(strict, dup(ap[0:c, 0:gc]), 0.0) for ap in aps]
    akms = [jnp.where(strict, dup(ap[0:c, gc:2 * gc]), 0.0) for ap in aps]
    rbms = [jnp.where(incl, dup(ap[c:2 * c, 0:gc]), 0.0) for ap in aps]
    rkms = [jnp.where(incl, dup(ap[c:2 * c, gc:2 * gc]), 0.0) for ap in aps]
    tmats = [eye_gc + nm for nm in nmats]
    pws = nmats
    for _ in range(int(math.log2(c)) - 1):
        pws = [_dot(pw, pw) for pw in pws]
        tmats = [tm + _dot(tm, pw) for tm, pw in zip(tmats, pws)]
    ves = [expand(part(v, it)) for it in items]
    akvs = [_dot(akm, ve) for akm, ve in zip(akms, ves)]
    wus = [_dot(tm, jnp.concatenate([expand(part(at, it)), akv], axis=1))
           for tm, it, akv in zip(tmats, items, akvs)]
    bkts = [jnp.concatenate([expand(part(bg, it)), expand(part(kg, it))], axis=0).T
            for it in items]
    gcols = [jnp.sum(jnp.where(eye_gl, part(glast, it)[0:1, :], 0.0), axis=1, keepdims=True) for it in items]
    zs = [z_sc[b, p] for b, p in items]
    ues = [_dot(wu[:, :gl], z) + wu[:, gl:] for wu, z in zip(wus, zs)]
    for n, (b, p) in enumerate(items):
        z_new = gcols[n] * zs[n] + _dot3(bkts[n], jnp.concatenate([ues[n], ves[n]], axis=0))
        z_sc[b, p] = z_new
        zout_ref[b, p] = z_new
    ys = [[None] * n_groups for _ in range(nb)]
    for n, (b, p) in enumerate(items):
        ye = _dot(jnp.concatenate([expand(part(rt, items[n])), rbms[n], rkms[n]], axis=1),
                  jnp.concatenate([zs[n], ues[n], ves[n]], axis=0))
        yp = ye[0:c, :]
        for q in range(1, g):
            yp = yp + ye[q * c:(q + 1) * c, :]
        ys[b][p] = yp
    y = jnp.concatenate([jnp.concatenate(yb, axis=1) for yb in ys], axis=0)
    out = _rwkv_post(y, r, k2, v, rk_ref[...], lnw_ref[...], lnb_ref[...], gmat)
    for b in range(nb):
        out_ref[b] = out[b * c:(b + 1) * c, :].astype(out_ref.dtype)


def _rwkv_params(W):
    w = RWKV_HEADS * RWKV_HEAD
    lora = jnp.zeros((DECAY_LORA + AAA_LORA, 2 * w), F32)
    lora = lora.at[:DECAY_LORA, :w].set(W['w_decay_up']).at[DECAY_LORA:, w:].set(W['w_aaa_up'])
    head = np.arange(w) // RWKV_HEAD
    gmat = jnp.asarray((head[:, None] == head[None, :]).astype(np.float32))
    row = lambda a: a.reshape(1, -1)
    return dict(mu=row(W['mu_shift']), lora=lora, w0=row(W['w0_decay']), a0=row(W['a0']),
                kk=row(W['k_k']), ka=row(W['k_a']), rk=row(W['r_k']), lnw=row(W['ln_x_w']),
                lnb=row(W['ln_x_b']), gmat=gmat)


def rwkv_prompt(rw, batch, W):
    n, cols = rw.shape
    t = n // batch
    c = RWKV_CHUNK
    nc = t // c
    w = RWKV_HEADS * RWKV_HEAD
    pr = _rwkv_params(W)
    args = (rw.reshape(batch, t, cols), pr['mu'], pr['lora'], pr['w0'], pr['a0'], pr['kk'], pr['ka'], pr['rk'],
            pr['lnw'], pr['lnb'], pr['gmat'])
    in_specs = [pl.BlockSpec((batch, c, cols), lambda i: (0, i, 0))] + [
        pl.BlockSpec(a.shape, lambda i: (0, 0)) for a in args[1:]]
    n_groups = RWKV_HEADS // RWKV_GROUP
    gl = RWKV_GROUP * RWKV_HEAD
    out_shape = (jax.ShapeDtypeStruct((batch, t, w), BF16),
                 jax.ShapeDtypeStruct((batch, n_groups, gl, gl), F32))
    ob, z = pl.pallas_call(
        functools.partial(_rwkv_prompt_kernel, chunk=c), grid=(nc,), in_specs=in_specs,
        out_specs=(pl.BlockSpec((batch, c, w), lambda i: (0, i, 0)),
                   pl.BlockSpec((batch, n_groups, gl, gl), lambda i: (0, 0, 0, 0))),
        out_shape=out_shape,
        scratch_shapes=[pltpu.VMEM((batch, cols), F32), pltpu.VMEM((batch, n_groups, gl, gl), F32)],
        compiler_params=_cparams(("arbitrary",)), name="rwkv_prompt")(*args)
    return ob.reshape(n, w), z


def _rwkv_sample_prep_kernel(rw_ref, sh_ref, mu_ref, lora_ref, w0_ref, a0_ref, kk_ref, ka_ref, g_ref,
                             r_out, k_out, v_out, w_out, a_out, b_out):
    rw = rw_ref[...]
    xr = rw + (sh_ref[...] - rw) * mu_ref[...]
    r, k2, v, logw, av, bv = _rwkv_prep(xr, lora_ref[...], w0_ref[...], a0_ref[...],
                                        kk_ref[...], ka_ref[...], g_ref[...])
    r_out[...] = r
    k_out[...] = k2
    v_out[...] = v
    w_out[...] = jnp.exp(logw)
    a_out[...] = av
    b_out[...] = bv


def _rwkv_sample_state_kernel(s_ref, r_ref, k_ref, v_ref, w_ref, a_ref, b_ref, s_out, y_out, *, tb):
    n = RWKV_HEAD
    eye = lax.broadcasted_iota(jnp.int32, (n, n), 0) == lax.broadcasted_iota(jnp.int32, (n, n), 1)
    for bl in range(tb):
        for h in range(RWKV_HEADS):
            s = s_ref[bl, h]
            row = lambda ref: ref[bl, h:h + 1, :]
            sa = jnp.sum(s * row(a_ref), axis=1, keepdims=True)
            vcol = jnp.sum(jnp.where(eye, row(v_ref), 0.0), axis=1, keepdims=True)
            s_new = s * row(w_ref) + sa * row(b_ref) + vcol * row(k_ref)
            s_out[bl, h] = s_new
            ycol = jnp.sum(s_new * row(r_ref), axis=1, keepdims=True)
            y_out[bl, h:h + 1, :] = jnp.sum(jnp.where(eye, ycol, 0.0), axis=0, keepdims=True)


def _rwkv_sample_post_kernel(y_ref, r_ref, k_ref, v_ref, rk_ref, lnw_ref, lnb_ref, g_ref, o_ref):
    o_ref[...] = _rwkv_post(y_ref[...], r_ref[...], k_ref[...], v_ref[...], rk_ref[...],
                            lnw_ref[...], lnb_ref[...], g_ref[...]).astype(o_ref.dtype)


def rwkv_sample(rw, shift0, state, W, tb=8):
    db, cols = rw.shape
    w = RWKV_HEADS * RWKV_HEAD
    pr = _rwkv_params(W)
    args = (rw, shift0, pr['mu'], pr['lora'], pr['w0'], pr['a0'], pr['kk'], pr['ka'], pr['gmat'])
    vec = jax.ShapeDtypeStruct((db, w), F32)
    r, k2, v, wd, av, bv = pl.pallas_call(
        _rwkv_sample_prep_kernel, grid=(1,), in_specs=[_full(a.shape) for a in args],
        out_specs=tuple(_full((db, w)) for _ in range(6)), out_shape=(vec,) * 6,
        compiler_params=_cparams(("arbitrary",)), name="rwkv_sample_prep")(*args)
    heads = lambda a: a.reshape(db, RWKV_HEADS, RWKV_HEAD)
    st_spec = pl.BlockSpec((tb, RWKV_HEADS, RWKV_HEAD, RWKV_HEAD), lambda i: (i, 0, 0, 0))
    hv_spec = pl.BlockSpec((tb, RWKV_HEADS, RWKV_HEAD), lambda i: (i, 0, 0))
    s_new, y = pl.pallas_call(
        functools.partial(_rwkv_sample_state_kernel, tb=tb), grid=(db // tb,),
        in_specs=[st_spec] + [hv_spec] * 6, out_specs=(st_spec, hv_spec),
        out_shape=(jax.ShapeDtypeStruct(state.shape, F32),
                   jax.ShapeDtypeStruct((db, RWKV_HEADS, RWKV_HEAD), F32)),
        compiler_params=_cparams(("parallel",)), name="rwkv_sample_state")(
            state, heads(r), heads(k2), heads(v), heads(wd), heads(av), heads(bv))
    pargs = (y.reshape(db, w), r, k2, v, pr['rk'], pr['lnw'], pr['lnb'], pr['gmat'])
    ob = pl.pallas_call(
        _rwkv_sample_post_kernel, grid=(1,), in_specs=[_full(a.shape) for a in pargs],
        out_specs=_full((db, w)), out_shape=jax.ShapeDtypeStruct((db, w), BF16),
        compiler_params=_cparams(("arbitrary",)), name="rwkv_sample_post")(*pargs)
    return ob, s_new


def _merge_kernel(oa_ref, ob_ref, oc_ref, g_ref, x_ref, woa_ref, wob_ref, woc_ref, wout_ref,
                  gffn_ref, wr_ref, br_ref, h_out, hn_out, ti_out, tg_out, *, d):
    o_a = jnp.dot(oa_ref[...], woa_ref[...], preferred_element_type=F32)
    o_b = jnp.dot(ob_ref[...], wob_ref[...], preferred_element_type=F32)
    o_c = jnp.dot(oc_ref[...], woc_ref[...], preferred_element_type=F32)
    g = g_ref[...].astype(F32)
    merged = g[:, 0:d] * o_a + g[:, d:2 * d] * o_b + g[:, 2 * d:3 * d] * o_c
    h = x_ref[...] + _dot(merged, wout_ref[...])
    h_out[...] = h
    hn = _rms(h, gffn_ref[...])
    _store_row_tiles(hn_out, hn)
    logits = _dot3(hn, wr_ref[...]) + br_ref[...]
    n_e = logits.shape[1]
    lane = lax.broadcasted_iota(jnp.int32, logits.shape, 1)
    cur = logits
    vals, idxs = [], []
    for _ in range(TOP_K):
        mx = jnp.max(cur, axis=-1, keepdims=True)
        ix = jnp.min(jnp.where(cur == mx, lane, n_e), axis=-1, keepdims=True)
        vals.append(mx)
        idxs.append(ix)
        cur = jnp.where(lane == ix, -jnp.inf, cur)
    es = [jnp.exp(vk - vals[0]) for vk in vals]
    den = es[0]
    for e in es[1:]:
        den = den + e
    lane_k = lax.broadcasted_iota(jnp.int32, ti_out.shape, 1)
    ti = jnp.zeros(ti_out.shape, jnp.int32)
    tg = jnp.zeros(tg_out.shape, F32)
    for kk in range(TOP_K):
        ti = jnp.where(lane_k == kk, idxs[kk], ti)
        tg = jnp.where(lane_k == kk, es[kk] / den, tg)
    ti_out[...] = ti
    tg_out[...] = tg


def merge_out(oa, w_oa, ob, oc, gates, x, W, tm):
    n, d = x.shape
    n_e = W['w_router'].shape[1]
    args = (oa, ob, oc, gates, x, w_oa.astype(BF16), W['w_o_rwkv'].astype(BF16), W['w_o_mem'].astype(BF16),
            W['w_out'].astype(BF16), W['g_ffn_norm'].reshape(1, d), W['w_router'], W['b_router'].reshape(1, n_e))
    row = lambda i: (i, 0)
    in_specs = [pl.BlockSpec((tm, a.shape[1]), row) for a in args[:5]] + [_full(a.shape) for a in args[5:]]
    out_shape = (jax.ShapeDtypeStruct((n, d), F32), jax.ShapeDtypeStruct((n * SUBLANES, LANES), F32),
                 jax.ShapeDtypeStruct((n, TOP_K), jnp.int32), jax.ShapeDtypeStruct((n, TOP_K), F32))
    out_specs = (pl.BlockSpec((tm, d), row), pl.BlockSpec((tm * SUBLANES, LANES), row),
                 pl.BlockSpec((tm, TOP_K), row), pl.BlockSpec((tm, TOP_K), row))
    return pl.pallas_call(
        functools.partial(_merge_kernel, d=d), grid=(n // tm,), in_specs=in_specs, out_specs=out_specs,
        out_shape=out_shape, compiler_params=_cparams(("parallel",)), name="merge_out")(*args)


def _gmm_kernel(be_ref, nu_ref, tok_ref, src_ref, w1_ref, perm_ref, b1_ref, w2_ref, b2_ref,
                y_ref, xbuf, w1b, w2b, sem, *, tm):
    i = pl.program_id(0)
    n_used = nu_ref[0]

    new_expert = jnp.logical_or(i == 0, be_ref[i] != be_ref[jnp.maximum(i - 1, 0)])

    @pl.when(jnp.logical_and(new_expert, i < n_used))
    def _():
        perm = perm_ref[...]
        for t in range(w1_ref.shape[2] // (2 * LANES)):
            sl = slice(2 * LANES * t, 2 * LANES * (t + 1))
            w1b[:, sl] = jnp.dot(w1_ref[0, :, sl].astype(BF16), perm, preferred_element_type=F32).astype(BF16)
        w2b[...] = w2_ref[0].astype(BF16)

    def row_copy(blk, r, slot):
        return pltpu.make_async_copy(src_ref.at[pl.ds(tok_ref[blk * tm + r] * SUBLANES, SUBLANES)],
                                     xbuf.at[slot, pl.ds(r * SUBLANES, SUBLANES)], sem.at[slot])

    def drain(slot):
        def step(r, carry):
            pltpu.make_async_copy(src_ref.at[pl.ds(0, SUBLANES)], xbuf.at[slot, pl.ds(0, SUBLANES)],
                                  sem.at[slot]).wait()
            return carry
        lax.fori_loop(0, tm, step, 0, unroll=8)

    last = n_used - 1

    @pl.when(jnp.logical_and(i == 0, n_used > 0))
    def _():
        def issue(r, carry):
            row_copy(0, r, 0).start()
            row_copy(jnp.minimum(1, last), r, 1).start()
            return carry
        lax.fori_loop(0, tm, issue, 0, unroll=8)

    @pl.when(i < n_used)
    def _():
        slot = i % GMM_BUFS
        drain(slot)
        x = _load_row_tiles(xbuf.at[slot], tm).astype(BF16)
        nxt = jnp.minimum(i + 2, last)
        nslot = (i + 2) % GMM_BUFS
        for r in range(tm):
            row_copy(nxt, r, nslot).start()
        h = jnp.dot(x, w1b[...], preferred_element_type=F32) + b1_ref[0]
        acts = []
        for t in range(w2_ref.shape[1] // LANES):
            x_glu = jnp.minimum(h[:, 2 * LANES * t:2 * LANES * t + LANES], SWIGLU_LIMIT)
            x_lin = jnp.clip(h[:, 2 * LANES * t + LANES:2 * LANES * (t + 1)], -SWIGLU_LIMIT, SWIGLU_LIMIT)
            acts.append((x_glu * _sigmoid(SWIGLU_ALPHA * x_glu) * (x_lin + 1.0)).astype(BF16))
        y = jnp.dot(jnp.concatenate(acts, axis=1), w2b[...], preferred_element_type=F32) + b2_ref[0]
        _store_row_tiles(y_ref, y)

    @pl.when(i == last)
    def _():
        drain((i + 1) % GMM_BUFS)
        drain((i + 2) % GMM_BUFS)

    @pl.when(i >= n_used)
    def _():
        y_ref[...] = jnp.zeros(y_ref.shape, y_ref.dtype)


def grouped_mlp(src, row_tok, block_e, n_used, W, tm):
    n_rows = row_tok.shape[0]
    d = W['w_mlp1'].shape[1]
    n_e = W['w_mlp1'].shape[0]
    d_ff = W['w_mlp2'].shape[1]
    j = np.arange(2 * LANES)
    col_src = np.where(j < LANES, 2 * j, 2 * (j - LANES) + 1)
    perm = jnp.asarray((np.arange(2 * LANES)[:, None] == col_src[None, :]).astype(np.float32), dtype=BF16)
    b1 = W['b_mlp1'].reshape(n_e, d_ff // LANES, LANES, 2)
    b1 = jnp.swapaxes(b1, 2, 3).reshape(n_e, 1, 2 * d_ff)
    b2 = W['b_mlp2'].reshape(n_e, 1, d)
    ex = lambda i, be, nu, tok: (be[i], 0, 0)
    grid_spec = pltpu.PrefetchScalarGridSpec(
        num_scalar_prefetch=3, grid=(n_rows // tm,),
        in_specs=[pl.BlockSpec(memory_space=pl.ANY),
                  pl.BlockSpec((1, d, 2 * d_ff), ex), pl.BlockSpec(perm.shape, lambda i, be, nu, tok: (0, 0)),
                  pl.BlockSpec((1, 1, 2 * d_ff), ex),
                  pl.BlockSpec((1, d_ff, d), ex), pl.BlockSpec((1, 1, d), ex)],
        out_specs=pl.BlockSpec((tm * SUBLANES, LANES), lambda i, be, nu, tok: (i, 0)),
        scratch_shapes=[pltpu.VMEM((GMM_BUFS, tm * SUBLANES, LANES), F32), pltpu.VMEM((d, 2 * d_ff), BF16),
                        pltpu.VMEM((d_ff, d), BF16), pltpu.SemaphoreType.DMA((GMM_BUFS,))])
    return pl.pallas_call(
        functools.partial(_gmm_kernel, tm=tm), grid_spec=grid_spec,
        out_shape=jax.ShapeDtypeStruct((n_rows * SUBLANES, LANES), F32),
        compiler_params=_cparams(("arbitrary",), GMM_VMEM_LIMIT), name="moe_gmm")(
            block_e, n_used, row_tok, src, W['w_mlp1'], perm, b1, W['w_mlp2'], b2)


def _combine_kernel(dest_ref, gate_ref, h_ref, gf_ref, yb_ref, o_ref, buf, sem, *, tc):
    i = pl.program_id(0)

    def start(step, slot):
        def issue(t, carry):
            for k in range(TOP_K):
                pltpu.make_async_copy(
                    yb_ref.at[pl.ds(dest_ref[(step * tc + t) * TOP_K + k] * SUBLANES, SUBLANES)],
                    buf.at[slot, k, pl.ds(t * SUBLANES, SUBLANES)], sem.at[slot]).start()
            return carry
        lax.fori_loop(0, tc, issue, 0, unroll=4)

    @pl.when(i == 0)
    def _():
        start(0, 0)

    @pl.when(i + 1 < pl.num_programs(0))
    def _():
        start(i + 1, (i + 1) % 2)

    slot = i % 2

    def drain(t, carry):
        for k in range(TOP_K):
            pltpu.make_async_copy(yb_ref.at[pl.ds(0, SUBLANES)], buf.at[slot, k, pl.ds(0, SUBLANES)],
                                  sem.at[slot]).wait()
        return carry

    lax.fori_loop(0, tc, drain, 0, unroll=4)
    gate = gate_ref[...]
    y = h_ref[...]
    for k in range(TOP_K):
        y = y + gate[:, k:k + 1] * _load_row_tiles(buf.at[slot, k], tc)
    o_ref[...] = _rms(y, gf_ref[...])


def moe_combine(yb, dest, gate, h, g_final, tc):
    n, d = h.shape
    grid_spec = pltpu.PrefetchScalarGridSpec(
        num_scalar_prefetch=1, grid=(n // tc,),
        in_specs=[pl.BlockSpec((tc, TOP_K), lambda i, ds: (i, 0)),
                  pl.BlockSpec((tc, d), lambda i, ds: (i, 0)),
                  pl.BlockSpec((1, d), lambda i, ds: (0, 0)),
                  pl.BlockSpec(memory_space=pl.ANY)],
        out_specs=pl.BlockSpec((tc, d), lambda i, ds: (i, 0)),
        scratch_shapes=[pltpu.VMEM((2, TOP_K, tc * SUBLANES, LANES), F32), pltpu.SemaphoreType.DMA((2,))])
    return pl.pallas_call(
        functools.partial(_combine_kernel, tc=tc), grid_spec=grid_spec,
        out_shape=jax.ShapeDtypeStruct((n, d), F32),
        compiler_params=_cparams(("arbitrary",)), name="moe_combine")(dest, gate, h, g_final.reshape(1, d), yb)


def _routing(top_i, n_e, tm):
    n_tok = top_i.shape[0]
    n_assign = n_tok * TOP_K
    flat_e = top_i.reshape(-1)
    onehot = (flat_e[:, None] == jnp.arange(n_e, dtype=jnp.int32)[None, :]).astype(jnp.int32)
    csum = jnp.cumsum(onehot, axis=0)
    counts = csum[-1]
    padded = ((counts + tm - 1) // tm) * tm
    pend = jnp.cumsum(padded)
    pstart = pend - padded
    dest = jnp.sum(onehot * (csum - onehot + pstart[None, :]), axis=1).astype(jnp.int32)
    n_blocks = -(-n_assign // tm) + n_e
    n_rows = n_blocks * tm
    row_tok = jnp.zeros((n_rows,), jnp.int32).at[dest].set(jnp.arange(n_assign, dtype=jnp.int32) // TOP_K)
    block_row0 = jnp.arange(n_blocks, dtype=jnp.int32) * tm
    block_e = jnp.minimum(jnp.sum((pend[None, :] <= block_row0[:, None]).astype(jnp.int32), axis=1), n_e - 1)
    n_used = (pend[-1:] // tm).astype(jnp.int32)
    return dest, row_tok, block_e, n_used


def kernel(x_prompt, x_sample, mem_prompt, cache_kv_latent, cache_k_rope, page_table, state_rwkv, state_rwkv_shift, cache_mem_k, cache_mem_v, g_attn_norm, w_in, b_gate, g_q_norm, w_uq, g_kv_norm, w_uk, w_uv, w_o_mla, mu_shift, w0_decay, w_decay_up, a0, w_aaa_up, k_k, k_a, r_k, ln_x_w, ln_x_b, w_o_rwkv, g_mem_norm, w_mem_k, w_mem_v, w_o_mem, w_out, g_ffn_norm, w_router, b_router, w_mlp1, b_mlp1, w_mlp2, b_mlp2, g_final):
    layer_params = dict(g_attn_norm=g_attn_norm, w_in=w_in, b_gate=b_gate, g_q_norm=g_q_norm, w_uq=w_uq,
                        g_kv_norm=g_kv_norm, w_uk=w_uk, w_uv=w_uv, w_o_mla=w_o_mla, mu_shift=mu_shift,
                        w0_decay=w0_decay, w_decay_up=w_decay_up, a0=a0, w_aaa_up=w_aaa_up, k_k=k_k, k_a=k_a,
                        r_k=r_k, ln_x_w=ln_x_w, ln_x_b=ln_x_b, w_o_rwkv=w_o_rwkv, g_mem_norm=g_mem_norm,
                        w_mem_k=w_mem_k, w_mem_v=w_mem_v, w_o_mem=w_o_mem, w_out=w_out, g_ffn_norm=g_ffn_norm,
                        w_router=w_router, b_router=b_router, w_mlp1=w_mlp1, b_mlp1=b_mlp1,
                        w_mlp2=w_mlp2, b_mlp2=b_mlp2)
    depth = w_in.shape[0]
    assert depth == 1, "single-layer trunk"
    W = {name: arr.reshape(arr.shape[1:]) for name, arr in layer_params.items()}
    B, T, D = x_prompt.shape
    DB, TS, _ = x_sample.shape
    assert TS == 1, "sample group decodes one token per sequence"
    n_pages = page_table.shape[1]
    page = cache_kv_latent.shape[2]
    past_len = n_pages * page
    kv_lora = w_uk.shape[1]
    mem_tokens = mem_prompt.shape[1]
    mem_w = MEM_HEADS * MEM_HEAD
    n_e = w_router.shape[2]
    rw_w = RWKV_HEADS * RWKV_HEAD

    tm = min(PROJ_TM, T)
    xp = x_prompt.reshape(B * T, D)
    xs = x_sample.reshape(DB, D)

    tab_p = _rope_tables(jnp.arange(T, dtype=jnp.int32))
    ckv_p, kpe_p, q_h, k_h, v_h = mla_proj_prompt(xp, tab_p, W, tm)
    rw_p, qm_p, gates_p = in_proj_rest(xp, W, tm)
    mk_p, mv_p = mem_kv(mem_prompt.reshape(B * mem_tokens, D), W)
    tq = min(FLASH_TQ, T)
    oa_p = flash_prompt(q_h, k_h, v_h, B, tq)
    oc_p = mem_attn_prompt(qm_p, mk_p.reshape(B, mem_tokens, mem_w), mv_p.reshape(B, mem_tokens, mem_w), tm)
    ob_p, z_p = rwkv_prompt(rw_p, B, W)
    woa_p = jnp.pad(W['w_o_mla'].reshape(MLA_HEADS, V_HEAD, D),
                    ((0, 0), (0, LANES - V_HEAD), (0, 0))).reshape(MLA_HEADS * LANES, D)
    h_p, hn_p, ti_p, tg_p = merge_out(oa_p, woa_p, ob_p, oc_p, gates_p, xp, W, tm)

    tab_s = _rope_tables(jnp.full((DB,), past_len, jnp.int32))
    ckv_s, kpe_s, qlat_s, qpe_s = mla_proj_sample(xs, tab_s, W)
    rw_s, qm_s, gates_s = in_proj_rest(xs, W, DB)
    oa_s = decode_attn(jnp.transpose(qlat_s, (1, 0, 2)), jnp.transpose(qpe_s, (1, 0, 2)),
                       ckv_s.reshape(DB, 1, kv_lora), kpe_s.reshape(DB, 1, QK_ROPE),
                       cache_kv_latent, jnp.swapaxes(cache_k_rope, 2, 3), page_table,
                       W['w_uv']).reshape(DB, MLA_HEADS * V_HEAD)
    oc_s = mem_attn_sample(qm_s.reshape(DB, MEM_HEADS, MEM_HEAD),
                           cache_mem_k.reshape(DB, mem_tokens, MEM_HEADS, MEM_HEAD),
                           cache_mem_v.reshape(DB, mem_tokens, MEM_HEADS, MEM_HEAD)).reshape(DB, mem_w)
    ob_s, st_s = rwkv_sample(rw_s, state_rwkv_shift.reshape(DB, -1),
                             state_rwkv.reshape(DB, RWKV_HEADS, RWKV_HEAD, RWKV_HEAD), W)
    h_s, hn_s, ti_s, tg_s = merge_out(oa_s.astype(BF16), W['w_o_mla'], ob_s, oc_s, gates_s, xs, W, DB)

    hn_all = jnp.concatenate([hn_p, hn_s], axis=0)
    top_i = jnp.concatenate([ti_p, ti_s], axis=0)
    dest, row_tok, block_e, n_used = _routing(top_i, n_e, MOE_TM)
    yb = grouped_mlp(hn_all, row_tok, block_e, n_used, W, MOE_TM)
    n_p = B * T
    y_p = moe_combine(yb, dest[:n_p * TOP_K], tg_p, h_p, g_final, min(128, T))
    y_s = moe_combine(yb, dest[n_p * TOP_K:], tg_s, h_s, g_final, DB)

    zq = z_p.reshape(B, RWKV_HEADS // RWKV_GROUP, RWKV_GROUP, RWKV_HEAD, RWKV_GROUP, RWKV_HEAD)
    st_p = jnp.stack([zq[:, :, q, :, q, :] for q in range(RWKV_GROUP)], axis=2)
    st_p = jnp.swapaxes(st_p.reshape(B, RWKV_HEADS, RWKV_HEAD, RWKV_HEAD), -1, -2)
    lead = lambda a: a[None]
    return (y_p.reshape(B, T, D), y_s.reshape(DB, TS, D),
            lead(ckv_p.reshape(B, T, kv_lora)), lead(kpe_p.reshape(B, T, QK_ROPE)),
            lead(st_p), lead(rw_p.reshape(B, T, -1)[:, -1]),
            lead(mk_p.reshape(B, mem_tokens, MEM_HEADS, MEM_HEAD)),
            lead(mv_p.reshape(B, mem_tokens, MEM_HEADS, MEM_HEAD)),
            lead(ckv_s.reshape(DB, TS, kv_lora)), lead(kpe_s.reshape(DB, TS, QK_ROPE)),
            lead(st_s), lead(rw_s))
```

```python
import functools
import math

import numpy as np
import jax
import jax.numpy as jnp
from jax import lax
from jax.experimental import pallas as pl
from jax.experimental.pallas import tpu as pltpu

F32 = jnp.float32
BF16 = jnp.bfloat16

MLA_HEADS = 8
QK_NOPE = 64
QK_ROPE = 32
V_HEAD = 64
ROPE_THETA = 10000.0
RWKV_HEADS = 8
RWKV_HEAD = 64
DECAY_LORA = 64
AAA_LORA = 64
MEM_HEADS = 4
MEM_HEAD = 128
TOP_K = 4
SWIGLU_ALPHA = 1.702
SWIGLU_LIMIT = 7.0
RMS_EPS = 1e-6
GN_EPS = 64e-5
L2_EPS = 1e-12

LANES = 128
SUBLANES = 8
VMEM_LIMIT = 48 * 1024 * 1024
GMM_VMEM_LIMIT = 58 * 1024 * 1024

RWKV_CHUNK = 64
RWKV_GROUP = 2
PROJ_TM = 512
MOE_TM = 512
GMM_BUFS = 3
DECODE_SPLITS = 4
FLASH_TQ = 1024
FLASH_SUBTILES = 2
FLASH_HEADS = 1


def _cparams(sem, vmem=VMEM_LIMIT):
    return pltpu.CompilerParams(dimension_semantics=sem, vmem_limit_bytes=vmem)


def _rms(x, g, eps=RMS_EPS):
    return x * lax.rsqrt(jnp.mean(x * x, axis=-1, keepdims=True) + eps) * g


def _dot(a, b):
    return jnp.dot(a.astype(BF16), b.astype(BF16), preferred_element_type=F32)


def _dot_nt(a, b):
    return lax.dot_general(a.astype(BF16), b.astype(BF16), (((1,), (1,)), ((), ())),
                           preferred_element_type=F32)


def _split(a):
    hi = a.astype(BF16)
    lo = (a - hi.astype(F32)).astype(BF16)
    return hi, lo


def _dot3(a, b):
    ah, al = _split(a)
    bh, bl = _split(b)
    return (jnp.dot(ah, bh, preferred_element_type=F32)
            + (jnp.dot(ah, bl, preferred_element_type=F32)
               + jnp.dot(al, bh, preferred_element_type=F32)))


def _dot2_exact_rhs(a, b):
    ah, al = _split(a)
    bb = b.astype(BF16)
    return jnp.dot(ah, bb, preferred_element_type=F32) + jnp.dot(al, bb, preferred_element_type=F32)


def _dot2_exact_lhs(a, b):
    aa = a.astype(BF16)
    bh, bl = _split(b)
    return jnp.dot(aa, bh, preferred_element_type=F32) + jnp.dot(aa, bl, preferred_element_type=F32)


def _store_row_tiles(ref, x):
    rows, width = x.shape
    assert width == SUBLANES * LANES
    for s in range(SUBLANES):
        ref[pl.ds(s, rows, stride=SUBLANES), :] = x[:, LANES * s:LANES * (s + 1)]


def _load_row_tiles(ref, rows):
    return jnp.concatenate([ref[pl.ds(s, rows, stride=SUBLANES), :] for s in range(SUBLANES)], axis=1)


def _sigmoid(x):
    return 1.0 / (1.0 + jnp.exp(-x))


def _mla_proj_prompt_kernel(x_ref, ga_ref, wcq_ref, gq_ref, wckv_ref, gkv_ref,
                            wkp_ref, wkpr_ref, wkpp_ref, wkprp_ref,
                            wq_ref, wqr_ref, wuk_ref, wuv_ref,
                            cos32_ref, sin32_ref, cost_ref, sint_ref,
                            ckv_out, kpe_out, q_out, k_out, v_out, *, scale):
    xn = _rms(x_ref[...], ga_ref[...]).astype(BF16)
    cqn = _rms(jnp.dot(xn, wcq_ref[...], preferred_element_type=F32), gq_ref[...]).astype(BF16)
    ckvn = _rms(jnp.dot(xn, wckv_ref[...], preferred_element_type=F32), gkv_ref[...])
    ckv_out[...] = ckvn
    kpe_out[...] = (jnp.dot(xn, wkp_ref[...], preferred_element_type=F32) * cos32_ref[...]
                    + jnp.dot(xn, wkpr_ref[...], preferred_element_type=F32) * sin32_ref[...])
    cost = cost_ref[...]
    sint = sint_ref[...]
    kpp = (jnp.dot(xn, wkpp_ref[...], preferred_element_type=F32) * cost
           + jnp.dot(xn, wkprp_ref[...], preferred_element_type=F32) * sint)
    qa = jnp.dot(cqn, wq_ref[...], preferred_element_type=F32)
    qb = jnp.dot(cqn, wqr_ref[...], preferred_element_type=F32)
    ckb = ckvn.astype(BF16)
    ka = jnp.dot(ckb, wuk_ref[...], preferred_element_type=F32)
    va = jnp.dot(ckb, wuv_ref[...], preferred_element_type=F32)
    ones_lane = lax.broadcasted_iota(jnp.int32, cost.shape, 1) == V_HEAD
    for h in range(MLA_HEADS):
        sl = slice(LANES * h, LANES * (h + 1))
        q_out[h] = ((qa[:, sl] * cost + qb[:, sl] * sint) * scale).astype(BF16)
        k_out[h] = (ka[:, sl] + kpp).astype(BF16)
        v_out[h] = jnp.where(ones_lane, 1.0, va[:, sl]).astype(BF16)


def _mla_proj_sample_kernel(x_ref, ga_ref, wcq_ref, gq_ref, wckv_ref, gkv_ref,
                            wkp_ref, wkpr_ref, wqn_ref, wukt_ref, wqp_ref, wqpr_ref,
                            cos32_ref, sin32_ref,
                            ckv_out, kpe_out, qlat_out, qpe_out, *, scale):
    xn = _rms(x_ref[...], ga_ref[...]).astype(BF16)
    cqn = _rms(jnp.dot(xn, wcq_ref[...], preferred_element_type=F32), gq_ref[...]).astype(BF16)
    ckv_out[...] = _rms(jnp.dot(xn, wckv_ref[...], preferred_element_type=F32), gkv_ref[...])
    cos32 = cos32_ref[...]
    sin32 = sin32_ref[...]
    kpe_out[...] = (jnp.dot(xn, wkp_ref[...], preferred_element_type=F32) * cos32
                    + jnp.dot(xn, wkpr_ref[...], preferred_element_type=F32) * sin32)
    qn = jnp.dot(cqn, wqn_ref[...], preferred_element_type=F32)
    for h in range(MLA_HEADS):
        sl = slice(LANES * h, LANES * (h + 1))
        qlat_out[h] = (_dot(qn[:, sl], wukt_ref[h]) * scale).astype(BF16)
        qpe_out[h] = ((jnp.dot(cqn, wqp_ref[h], preferred_element_type=F32) * cos32
                       + jnp.dot(cqn, wqpr_ref[h], preferred_element_type=F32) * sin32) * scale)


def _full(shape):
    nd = len(shape)
    return pl.BlockSpec(shape, lambda *_: (0,) * nd, pipeline_mode=pl.Buffered(1))


def _rope_tables(pos):
    half = QK_ROPE // 2
    inv = jnp.exp(-math.log(ROPE_THETA) * jnp.arange(half, dtype=F32) / half)
    ang = pos.astype(F32)[:, None] * inv[None, :]
    cos = jnp.cos(ang)
    sin = jnp.sin(ang)
    return jnp.concatenate([cos, cos], axis=1), jnp.concatenate([sin, sin], axis=1)


def _rot_cols(w):
    half = QK_ROPE // 2
    return jnp.concatenate([-w[..., half:], w[..., :half]], axis=-1)


def _pad_cols(w, lo, total=LANES):
    pad = [(0, 0)] * (w.ndim - 1) + [(lo, total - lo - w.shape[-1])]
    return jnp.pad(w, pad)


def _mla_weights(W):
    d = W['w_in'].shape[0]
    q_lora = W['w_uq'].shape[0]
    kv_lora = W['w_uk'].shape[0]
    c0, c1, c2 = q_lora, q_lora + kv_lora, q_lora + kv_lora + QK_ROPE
    w_in = W['w_in']
    wcq = w_in[:, :c0].astype(BF16)
    wckv = w_in[:, c0:c1].astype(BF16)
    wkp = w_in[:, c1:c2]
    wuq = W['w_uq'].reshape(q_lora, MLA_HEADS, QK_NOPE + QK_ROPE)
    return dict(d=d, q_lora=q_lora, kv_lora=kv_lora, wcq=wcq, wckv=wckv, wkp=wkp, wuq=wuq,
                ga=W['g_attn_norm'].reshape(1, d), gq=W['g_q_norm'].reshape(1, q_lora),
                gkv=W['g_kv_norm'].reshape(1, kv_lora))


def mla_proj_prompt(x, pos_tables, W, tm):
    n, d = x.shape
    m = _mla_weights(W)
    q_lora, kv_lora = m['q_lora'], m['kv_lora']
    cos32, sin32 = pos_tables
    t = cos32.shape[0]
    ones = jnp.ones((t, QK_NOPE), F32)
    zeros_hi = jnp.zeros((t, LANES - QK_NOPE - QK_ROPE), F32)
    cost = jnp.concatenate([ones, cos32, zeros_hi], axis=1)
    sint = jnp.concatenate([jnp.zeros((t, QK_NOPE), F32), sin32, zeros_hi], axis=1)
    wkp = m['wkp']
    wkpr = _rot_cols(wkp)
    wuq = m['wuq']
    wq = _pad_cols(wuq, 0).reshape(q_lora, MLA_HEADS * LANES).astype(BF16)
    wqr = _pad_cols(_rot_cols(wuq[..., QK_NOPE:]), QK_NOPE).reshape(q_lora, MLA_HEADS * LANES).astype(BF16)
    wuk = _pad_cols(W['w_uk'], 0).reshape(kv_lora, MLA_HEADS * LANES).astype(BF16)
    wuv = _pad_cols(W['w_uv'], 0).reshape(kv_lora, MLA_HEADS * LANES).astype(BF16)
    args = (x, m['ga'], m['wcq'], m['gq'], m['wckv'], m['gkv'],
            wkp.astype(BF16), wkpr.astype(BF16),
            _pad_cols(wkp, QK_NOPE).astype(BF16), _pad_cols(wkpr, QK_NOPE).astype(BF16),
            wq, wqr, wuk, wuv, cos32, sin32, cost, sint)
    nt = t // tm
    row = lambda i: (i, 0)
    tab = lambda i: (i % nt, 0)
    in_specs = [pl.BlockSpec((tm, d), row)] + [_full(a.shape) for a in args[1:14]] + [
        pl.BlockSpec((tm, QK_ROPE), tab), pl.BlockSpec((tm, QK_ROPE), tab),
        pl.BlockSpec((tm, LANES), tab), pl.BlockSpec((tm, LANES), tab)]
    head = pl.BlockSpec((MLA_HEADS, tm, LANES), lambda i: (0, i, 0))
    out_shape = (jax.ShapeDtypeStruct((n, kv_lora), F32), jax.ShapeDtypeStruct((n, QK_ROPE), F32),
                 jax.ShapeDtypeStruct((MLA_HEADS, n, LANES), BF16),
                 jax.ShapeDtypeStruct((MLA_HEADS, n, LANES), BF16),
                 jax.ShapeDtypeStruct((MLA_HEADS, n, LANES), BF16))
    out_specs = (pl.BlockSpec((tm, kv_lora), row), pl.BlockSpec((tm, QK_ROPE), row), head, head, head)
    scale = (QK_NOPE + QK_ROPE) ** -0.5 * math.log2(math.e)
    return pl.pallas_call(
        functools.partial(_mla_proj_prompt_kernel, scale=scale),
        grid=(n // tm,), in_specs=in_specs, out_specs=out_specs, out_shape=out_shape,
        compiler_params=_cparams(("parallel",)), name="mla_proj_prompt")(*args)


def mla_proj_sample(x, pos_tables, W):
    n, d = x.shape
    m = _mla_weights(W)
    q_lora, kv_lora = m['q_lora'], m['kv_lora']
    cos32, sin32 = pos_tables
    wkp = m['wkp']
    wuq = m['wuq']
    wqn = _pad_cols(wuq[..., :QK_NOPE], 0).reshape(q_lora, MLA_HEADS * LANES).astype(BF16)
    wukt = jnp.pad(jnp.transpose(W['w_uk'], (1, 2, 0)), ((0, 0), (0, LANES - QK_NOPE), (0, 0))).astype(BF16)
    wqp = jnp.transpose(wuq[..., QK_NOPE:], (1, 0, 2))
    args = (x, m['ga'], m['wcq'], m['gq'], m['wckv'], m['gkv'],
            wkp.astype(BF16), _rot_cols(wkp).astype(BF16), wqn, wukt,
            wqp.astype(BF16), _rot_cols(wqp).astype(BF16), cos32, sin32)
    out_shape = (jax.ShapeDtypeStruct((n, kv_lora), F32), jax.ShapeDtypeStruct((n, QK_ROPE), F32),
                 jax.ShapeDtypeStruct((MLA_HEADS, n, kv_lora), BF16),
                 jax.ShapeDtypeStruct((MLA_HEADS, n, QK_ROPE), F32))
    scale = (QK_NOPE + QK_ROPE) ** -0.5
    return pl.pallas_call(
        functools.partial(_mla_proj_sample_kernel, scale=scale),
        grid=(1,), in_specs=[_full(a.shape) for a in args],
        out_specs=tuple(_full(s.shape) for s in out_shape), out_shape=out_shape,
        compiler_params=_cparams(("arbitrary",)), name="mla_proj_sample")(*args)


def _in_proj_rest_kernel(x_ref, ga_ref, wrw_ref, wqm_ref, wg_ref, bg_ref,
                         rw_out, qm_out, g_out, *, mem_scale):
    xn = _rms(x_ref[...], ga_ref[...]).astype(BF16)
    rw_out[...] = jnp.dot(xn, wrw_ref[...], preferred_element_type=F32)
    qm_out[...] = (jnp.dot(xn, wqm_ref[...], preferred_element_type=F32) * mem_scale).astype(BF16)
    g_out[...] = _sigmoid(jnp.dot(xn, wg_ref[...], preferred_element_type=F32) + bg_ref[...]).astype(BF16)


def in_proj_rest(x, W, tm):
    n, d = x.shape
    q_lora = W['w_uq'].shape[0]
    kv_lora = W['w_uk'].shape[0]
    rw_cols = W['mu_shift'].shape[0]
    mem_w = MEM_HEADS * MEM_HEAD
    c2 = q_lora + kv_lora + QK_ROPE
    c3 = c2 + rw_cols
    c4 = c3 + mem_w
    w_in = W['w_in']
    n_gate = w_in.shape[1] - c4
    args = (x, W['g_attn_norm'].reshape(1, d), w_in[:, c2:c3].astype(BF16), w_in[:, c3:c4].astype(BF16),
            w_in[:, c4:].astype(BF16), W['b_gate'].reshape(1, n_gate))
    row = lambda i: (i, 0)
    out_shape = (jax.ShapeDtypeStruct((n, rw_cols), F32), jax.ShapeDtypeStruct((n, mem_w), BF16),
                 jax.ShapeDtypeStruct((n, n_gate), BF16))
    return pl.pallas_call(
        functools.partial(_in_proj_rest_kernel, mem_scale=MEM_HEAD ** -0.5),
        grid=(n // tm,),
        in_specs=[pl.BlockSpec((tm, d), row)] + [_full(a.shape) for a in args[1:]],
        out_specs=(pl.BlockSpec((tm, rw_cols), row), pl.BlockSpec((tm, mem_w), row),
                   pl.BlockSpec((tm, n_gate), row)),
        out_shape=out_shape, compiler_params=_cparams(("parallel",)), name="in_proj_rest")(*args)


def _mem_kv_kernel(m_ref, g_ref, wk_ref, wv_ref, k_out, v_out):
    mn = _rms(m_ref[...], g_ref[...]).astype(BF16)
    k_out[...] = jnp.dot(mn, wk_ref[...], preferred_element_type=F32)
    v_out[...] = jnp.dot(mn, wv_ref[...], preferred_element_type=F32)


def mem_kv(mem, W):
    n, d = mem.shape
    mem_w = MEM_HEADS * MEM_HEAD
    args = (mem, W['g_mem_norm'].reshape(1, d), W['w_mem_k'].astype(BF16), W['w_mem_v'].astype(BF16))
    out_shape = (jax.ShapeDtypeStruct((n, mem_w), F32), jax.ShapeDtypeStruct((n, mem_w), F32))
    return pl.pallas_call(
        _mem_kv_kernel, grid=(1,), in_specs=[_full(a.shape) for a in args],
        out_specs=(_full((n, mem_w)), _full((n, mem_w))), out_shape=out_shape,
        compiler_params=_cparams(("arbitrary",)), name="mem_kv")(*args)


def _flash_kernel(q_ref, k_ref, v_ref, o_ref, *scratch, tq, n_sub):
    qi = pl.program_id(2)
    sub = tq // n_sub
    nh, _, dh = q_ref.shape
    n_ch = nh * n_sub
    m_scs = scratch[:n_ch]
    acc_scs = scratch[n_ch:2 * n_ch]
    s_scs = scratch[2 * n_ch:]
    for ch in range(n_ch):
        m_scs[ch][...] = jnp.full(m_scs[ch].shape, -jnp.inf, F32)
        acc_scs[ch][...] = jnp.zeros(acc_scs[ch].shape, F32)

    def scores(j, slot, diagonal=False):
        for hd in range(nh):
            kb = k_ref[hd, pl.ds(pl.multiple_of(j * tq, tq), tq), :]
            for u in range(n_sub):
                ncols = (u + 1) * sub if diagonal else tq
                s_scs[slot][hd * n_sub + u, :, 0:ncols] = lax.dot_general(
                    q_ref[hd, u * sub:(u + 1) * sub, :], kb[0:ncols], (((1,), (1,)), ((), ())),
                    preferred_element_type=F32)

    def consume(j, slot, masked):
        for hd in range(nh):
            vb = v_ref[hd, pl.ds(pl.multiple_of(j * tq, tq), tq), :]
            for u in range(n_sub):
                ch = hd * n_sub + u
                ncols = (u + 1) * sub if masked else tq
                s = s_scs[slot][ch, :, 0:ncols]
                if masked:
                    rows = u * sub + lax.broadcasted_iota(jnp.int32, (sub, ncols), 0)
                    cols = lax.broadcasted_iota(jnp.int32, (sub, ncols), 1)
                    s = jnp.where(cols <= rows, s, -jnp.inf)
                m_prev = m_scs[ch][...]
                m_new = jnp.maximum(m_prev, jnp.max(s, axis=-1, keepdims=True))
                alpha = jnp.exp2(m_prev - m_new)
                p = jnp.exp2(s - jnp.tile(m_new, (1, ncols // LANES)))
                acc_scs[ch][...] = (alpha * acc_scs[ch][...]
                                    + jnp.dot(p.astype(BF16), vb[0:ncols], preferred_element_type=F32))
                m_scs[ch][...] = m_new

    scores(0, 0)

    def body(i, carry):
        scores(2 * i + 1, 1)
        consume(2 * i, 0, False)
        scores(2 * i + 2, 0)
        consume(2 * i + 1, 1, False)
        return carry

    lax.fori_loop(0, qi // 2, body, 0)

    @pl.when(qi % 2 == 1)
    def _():
        scores(qi, 1, diagonal=True)
        consume(qi - 1, 0, False)
        consume(qi, 1, True)

    @pl.when(qi % 2 == 0)
    def _():
        consume(qi, 0, True)

    for hd in range(nh):
        for u in range(n_sub):
            acc = acc_scs[hd * n_sub + u][...]
            o_ref[u * sub:(u + 1) * sub, hd * dh:(hd + 1) * dh] = (
                acc / acc[:, V_HEAD:V_HEAD + 1]).astype(o_ref.dtype)


def flash_prompt(q, k, v, batch, tq):
    h, n, dh = q.shape
    t = n // batch
    nq = t // tq
    nh = FLASH_HEADS
    kv_spec = pl.BlockSpec((nh, t, dh), lambda b, hh, i: (hh, b, 0))
    n_sub = FLASH_SUBTILES
    n_ch = nh * n_sub
    sub = tq // n_sub
    return pl.pallas_call(
        functools.partial(_flash_kernel, tq=tq, n_sub=n_sub), grid=(batch, h // nh, nq),
        in_specs=[pl.BlockSpec((nh, tq, dh), lambda b, hh, i: (hh, b * nq + i, 0)), kv_spec, kv_spec],
        out_specs=pl.BlockSpec((tq, nh * dh), lambda b, hh, i: (b * nq + i, hh)),
        out_shape=jax.ShapeDtypeStruct((n, h * dh), BF16),
        scratch_shapes=([pltpu.VMEM((sub, LANES), F32)] * n_ch + [pltpu.VMEM((sub, dh), F32)] * n_ch
                        + [pltpu.VMEM((n_ch, sub, tq), F32)] * 2),
        compiler_params=_cparams(("parallel", "parallel", "arbitrary")), name="flash_prompt")(q, k, v)


def _decode_kernel(pt_ref, qlat_ref, qpe_ref, ckv_ref, kpe_ref, wuv_ref, lat_hbm, rope_hbm, o_ref,
                   lat_buf, rope_buf, sem, *, n_pages, page):
    b = pl.program_id(0)

    def page_copies(seq, slot, i):
        pg = pt_ref[seq, i]
        rows = pl.ds(i * page, page)
        return (pltpu.make_async_copy(lat_hbm.at[0, pg], lat_buf.at[slot, rows], sem.at[slot]),
                pltpu.make_async_copy(rope_hbm.at[0, pg], rope_buf.at[slot, :, rows], sem.at[slot]))

    def start_all(seq, slot):
        for i in range(n_pages):
            for c in page_copies(seq, slot, i):
                c.start()

    @pl.when(b == 0)
    def _():
        start_all(0, 0)

    @pl.when(b + 1 < pl.num_programs(0))
    def _():
        start_all(b + 1, (b + 1) % 2)

    slot = b % 2

    def drain(i, carry):
        for c in page_copies(b, slot, 0):
            c.wait()
        return carry

    lax.fori_loop(0, n_pages, drain, 0)

    qlat = qlat_ref[0]
    qpe = qpe_ref[0]
    ckv = ckv_ref[0]
    kpe = kpe_ref[0]
    s_new = (jnp.sum(qlat.astype(F32) * ckv, axis=-1, keepdims=True)
             + jnp.sum(qpe * kpe, axis=-1, keepdims=True))
    past = n_pages * page
    n_split = DECODE_SPLITS if past % (DECODE_SPLITS * LANES) == 0 else 1
    span = past // n_split
    kparts = [lat_buf[slot, c * span:(c + 1) * span, :].astype(BF16) for c in range(n_split)]
    sparts = [_dot_nt(qlat, kparts[c]) + _dot(qpe, rope_buf[slot, :, c * span:(c + 1) * span])
              for c in range(n_split)]
    mparts = [jnp.max(sp, axis=-1, keepdims=True) for sp in sparts]
    pparts = [jnp.exp(sp - mp) for sp, mp in zip(sparts, mparts)]
    oparts = [jnp.dot(pp.astype(BF16), kp, preferred_element_type=F32) for pp, kp in zip(pparts, kparts)]
    m = s_new
    for mp in mparts:
        m = jnp.maximum(m, mp)
    p_new = jnp.exp(s_new - m)
    l = p_new
    o_acc = p_new * ckv
    for mp, pp, op in zip(mparts, pparts, oparts):
        w = jnp.exp(mp - m)
        l = l + w * jnp.sum(pp, axis=-1, keepdims=True)
        o_acc = o_acc + w * op
    o_lat = o_acc / l
    ov = _dot(o_lat, wuv_ref[...])
    hh = lax.broadcasted_iota(jnp.int32, ov.shape, 0)
    cc = lax.broadcasted_iota(jnp.int32, ov.shape, 1) // V_HEAD
    o_ref[0] = jnp.sum(jnp.where(hh == cc, ov, 0.0), axis=0, keepdims=True)


def decode_attn(qlat, qpe, ckv, kpe, cache_lat, cache_rope, page_table, w_uv):
    db, h, kv = qlat.shape
    _, n_pool, page, _ = cache_lat.shape
    n_pages = page_table.shape[1]
    past = n_pages * page
    wuv = w_uv.reshape(kv, h * V_HEAD).astype(BF16)
    per_b = lambda shape: pl.BlockSpec((1,) + shape, lambda b, pt: (b, 0, 0))
    in_specs = [per_b((h, kv)), per_b((h, QK_ROPE)), per_b((1, kv)), per_b((1, QK_ROPE)),
                pl.BlockSpec(wuv.shape, lambda b, pt: (0, 0)),
                pl.BlockSpec(memory_space=pl.ANY), pl.BlockSpec(memory_space=pl.ANY)]
    grid_spec = pltpu.PrefetchScalarGridSpec(
        num_scalar_prefetch=1, grid=(db,), in_specs=in_specs,
        out_specs=per_b((1, h * V_HEAD)),
        scratch_shapes=[pltpu.VMEM((2, past, kv), F32), pltpu.VMEM((2, QK_ROPE, past), F32),
                        pltpu.SemaphoreType.DMA((2,))])
    return pl.pallas_call(
        functools.partial(_decode_kernel, n_pages=n_pages, page=page), grid_spec=grid_spec,
        out_shape=jax.ShapeDtypeStruct((db, 1, h * V_HEAD), F32),
        compiler_params=_cparams(("arbitrary",)), name="decode_attn")(
            page_table, qlat, qpe, ckv, kpe, wuv, cache_lat, cache_rope)


def _mem_attn_prompt_kernel(q_ref, k_ref, v_ref, o_ref):
    q = q_ref[...]
    k = k_ref[0].astype(BF16)
    v = v_ref[0].astype(BF16)
    for h in range(MEM_HEADS):
        sl = slice(MEM_HEAD * h, MEM_HEAD * (h + 1))
        s = _dot_nt(q[:, sl], k[:, sl])
        p = jnp.exp(s - jnp.max(s, axis=-1, keepdims=True))
        o = jnp.dot(p.astype(BF16), v[:, sl], preferred_element_type=F32) / jnp.sum(p, axis=-1, keepdims=True)
        o_ref[:, sl] = o.astype(o_ref.dtype)


def mem_attn_prompt(qm, mk, mv, tm):
    n, w = qm.shape
    b, m, _ = mk.shape
    nt = n // b // tm
    kv_spec = pl.BlockSpec((1, m, w), lambda i: (i // nt, 0, 0))
    return pl.pallas_call(
        _mem_attn_prompt_kernel, grid=(n // tm,),
        in_specs=[pl.BlockSpec((tm, w), lambda i: (i, 0)), kv_spec, kv_spec],
        out_specs=pl.BlockSpec((tm, w), lambda i: (i, 0)),
        out_shape=jax.ShapeDtypeStruct((n, w), BF16),
        compiler_params=_cparams(("parallel",)), name="mem_attn_prompt")(qm, mk, mv)


def _mem_attn_sample_kernel(q_ref, k_ref, v_ref, o_ref, *, tb):
    for bl in range(tb):
        q = q_ref[bl].astype(F32)
        s = jnp.sum(k_ref[bl] * q, axis=-1, keepdims=True)
        p = jnp.exp(s - jnp.max(s, axis=0, keepdims=True))
        o = jnp.sum(p * v_ref[bl], axis=0) / jnp.sum(p, axis=0)
        o_ref[bl] = o.astype(o_ref.dtype)


def mem_attn_sample(qm, mk, mv, tb=8):
    db, nh, hd = qm.shape
    m = mk.shape[1]
    kv_spec = pl.BlockSpec((tb, m, nh, hd), lambda i: (i, 0, 0, 0))
    q_spec = pl.BlockSpec((tb, nh, hd), lambda i: (i, 0, 0))
    return pl.pallas_call(
        functools.partial(_mem_attn_sample_kernel, tb=tb), grid=(db // tb,),
        in_specs=[q_spec, kv_spec, kv_spec], out_specs=q_spec,
        out_shape=jax.ShapeDtypeStruct((db, nh, hd), BF16),
        compiler_params=_cparams(("parallel",)), name="mem_attn_sample")(qm, mk, mv)


def _rwkv_prep(xr, lora_w, w0, a0, kk_w, ka_w, gmat):
    w = RWKV_HEADS * RWKV_HEAD
    r = xr[:, 0:w]
    k = xr[:, w:2 * w]
    v = xr[:, 2 * w:3 * w]
    tail = xr[:, 3 * w:3 * w + DECAY_LORA + AAA_LORA]
    lane = lax.broadcasted_iota(jnp.int32, tail.shape, 1)
    z = jnp.where(lane < DECAY_LORA, jnp.tanh(tail), tail)
    lo = _dot3(z, lora_w)
    dec = lo[:, :w] + w0
    aaa = lo[:, w:] + a0
    u = -dec
    softplus = jnp.maximum(u, 0.0) + jnp.log(1.0 + jnp.exp(-jnp.abs(u)))
    logw = -jnp.exp(-softplus - 0.5)
    a_sig = _sigmoid(aaa)
    kk = k * kk_w
    norm = jnp.sqrt(_dot2_exact_rhs(kk * kk, gmat))
    kkn = kk / jnp.maximum(norm, L2_EPS)
    k2 = k * (1.0 + (a_sig - 1.0) * ka_w)
    return r, k2, v, logw, -kkn, kkn * a_sig


def _rwkv_post(y, r, k2, v, rk, lnw, lnb, gmat):
    inv_n = 1.0 / RWKV_HEAD
    mu = _dot2_exact_rhs(y, gmat) * inv_n
    d = y - mu
    var = _dot2_exact_rhs(d * d, gmat) * inv_n
    yn = d * lax.rsqrt(var + GN_EPS) * lnw + lnb
    return yn + _dot2_exact_rhs(r * k2 * rk, gmat) * v


def _rwkv_prompt_kernel(rw_ref, mu_ref, lora_ref, w0_ref, a0_ref, kk_ref, ka_ref, rk_ref,
                        lnw_ref, lnb_ref, g_ref, out_ref, zout_ref, prev_sc, z_sc, *, chunk):
    c = chunk
    nb = rw_ref.shape[0]

    @pl.when(pl.program_id(0) == 0)
    def _():
        prev_sc[...] = jnp.zeros(prev_sc.shape, F32)
        z_sc[...] = jnp.zeros(z_sc.shape, F32)

    rw = jnp.concatenate([rw_ref[b] for b in range(nb)], axis=0)
    row = lax.broadcasted_iota(jnp.int32, rw.shape, 0)
    prev = pltpu.roll(rw, 1, 0)
    for b in range(nb):
        prev = jnp.where(row == b * c, prev_sc[b:b + 1, :], prev)
        prev_sc[b:b + 1, :] = rw[(b + 1) * c - 1:(b + 1) * c, :]
    xr = rw + (prev - rw) * mu_ref[...]
    gmat = g_ref[...]
    r, k2, v, logw, av, bv = _rwkv_prep(xr, lora_ref[...], w0_ref[...], a0_ref[...],
                                        kk_ref[...], ka_ref[...], gmat)

    ti = lax.broadcasted_iota(jnp.int32, (nb * c, nb * c), 0)
    si = lax.broadcasted_iota(jnp.int32, (nb * c, nb * c), 1)
    causal = jnp.logical_and(si <= ti, (si // c) == (ti // c))
    cum = _dot2_exact_lhs(causal.astype(F32), logw)
    clast = jnp.concatenate([jnp.broadcast_to(cum[(b + 1) * c - 1:(b + 1) * c, :], (c, cum.shape[1]))
                             for b in range(nb)], axis=0)
    e_neg = jnp.exp(-cum)
    e_rest = jnp.exp(clast - cum)
    at = av * jnp.exp(cum - logw)
    bt = bv * e_neg
    kt = k2 * e_neg
    rt = r * jnp.exp(cum)
    bg = bv * e_rest
    kg = k2 * e_rest
    glast = jnp.exp(clast)

    g = RWKV_GROUP
    gl = g * RWKV_HEAD
    gc = g * c
    n_groups = RWKV_HEADS // g
    emask = (lax.broadcasted_iota(jnp.int32, (gc, gl), 0) // c
             == lax.broadcasted_iota(jnp.int32, (gc, gl), 1) // RWKV_HEAD)
    br = lax.broadcasted_iota(jnp.int32, (gc, gc), 0)
    bc = lax.broadcasted_iota(jnp.int32, (gc, gc), 1)
    same = (br // c) == (bc // c)
    strict = jnp.logical_and(same, (br % c) > (bc % c))
    incl = jnp.logical_and(same, (br % c) >= (bc % c))
    eye_gc = (br == bc).astype(F32)
    lane_head = lax.broadcasted_iota(jnp.int32, (c, gl), 1) // RWKV_HEAD
    eye_gl = (lax.broadcasted_iota(jnp.int32, (gl, gl), 0) == lax.broadcasted_iota(jnp.int32, (gl, gl), 1))

    def dup(x):
        return jnp.concatenate([x] * g, axis=0)

    def expand(x):
        return jnp.where(emask, dup(x), 0.0)

    def per_head(x):
        return jnp.concatenate([jnp.where(lane_head == q, x, 0.0) for q in range(g)], axis=0)

    items = [(b, p) for b in range(nb) for p in range(n_groups)]

    def part(x, item):
        b, p = item
        return x[b * c:(b + 1) * c, gl * p:gl * (p + 1)]

    aps = []
    for it in items:
        lhs = jnp.concatenate([part(at, it), part(rt, it)], axis=0)
        rhs = jnp.concatenate([per_head(part(bt, it)), per_head(part(kt, it))], axis=0)
        aps.append(_dot_nt(lhs, rhs))
    nmats = [jnp.where(strict, dup(ap[0:c, 0:gc]), 0.0) for ap in aps]
    akms = [jnp.where(strict, dup(ap[0:c, gc:2 * gc]), 0.0) for ap in aps]
    rbms = [jnp.where(incl, dup(ap[c:2 * c, 0:gc]), 0.0) for ap in aps]
    rkms = [jnp.where(incl, dup(ap[c:2 * c, gc:2 * gc]), 0.0) for ap in aps]
    tmats = [eye_gc + nm for nm in nmats]
    pws = nmats
    for _ in range(int(math.log2(c)) - 1):
        pws = [_dot(pw, pw) for pw in pws]
        tmats = [tm + _dot(tm, pw) for tm, pw in zip(tmats, pws)]
    ves = [expand(part(v, it)) for it in items]
    akvs = [_dot(akm, ve) for akm, ve in zip(akms, ves)]
    wus = [_dot(tm, jnp.concatenate([expand(part(at, it)), akv], axis=1))
           for tm, it, akv in zip(tmats, items, akvs)]
    bkts = [jnp.concatenate([expand(part(bg, it)), expand(part(kg, it))], axis=0).T
            for it in items]
    gcols = [jnp.sum(jnp.where(eye_gl, part(glast, it)[0:1, :], 0.0), axis=1, keepdims=True) for it in items]
    zs = [z_sc[b, p] for b, p in items]
    ues = [_dot(wu[:, :gl], z) + wu[:, gl:] for wu, z in zip(wus, zs)]
    for n, (b, p) in enumerate(items):
        z_new = gcols[n] * zs[n] + _dot3(bkts[n], jnp.concatenate([ues[n], ves[n]], axis=0))
        z_sc[b, p] = z_new
        zout_ref[b, p] = z_new
    ys = [[None] * n_groups for _ in range(nb)]
    for n, (b, p) in enumerate(items):
        ye = _dot(jnp.concatenate([expand(part(rt, items[n])), rbms[n], rkms[n]], axis=1),
                  jnp.concatenate([zs[n], ues[n], ves[n]], axis=0))
        yp = ye[0:c, :]
        for q in range(1, g):
            yp = yp + ye[q * c:(q + 1) * c, :]
        ys[b][p] = yp
    y = jnp.concatenate([jnp.concatenate(yb, axis=1) for yb in ys], axis=0)
    out = _rwkv_post(y, r, k2, v, rk_ref[...], lnw_ref[...], lnb_ref[...], gmat)
    for b in range(nb):
        out_ref[b] = out[b * c:(b + 1) * c, :].astype(out_ref.dtype)


def _rwkv_params(W):
    w = RWKV_HEADS * RWKV_HEAD
    lora = jnp.zeros((DECAY_LORA + AAA_LORA, 2 * w), F32)
    lora = lora.at[:DECAY_LORA, :w].set(W['w_decay_up']).at[DECAY_LORA:, w:].set(W['w_aaa_up'])
    head = np.arange(w) // RWKV_HEAD
    gmat = jnp.asarray((head[:, None] == head[None, :]).astype(np.float32))
    row = lambda a: a.reshape(1, -1)
    return dict(mu=row(W['mu_shift']), lora=lora, w0=row(W['w0_decay']), a0=row(W['a0']),
                kk=row(W['k_k']), ka=row(W['k_a']), rk=row(W['r_k']), lnw=row(W['ln_x_w']),
                lnb=row(W['ln_x_b']), gmat=gmat)


def rwkv_prompt(rw, batch, W):
    n, cols = rw.shape
    t = n // batch
    c = RWKV_CHUNK
    nc = t // c
    w = RWKV_HEADS * RWKV_HEAD
    pr = _rwkv_params(W)
    args = (rw.reshape(batch, t, cols), pr['mu'], pr['lora'], pr['w0'], pr['a0'], pr['kk'], pr['ka'], pr['rk'],
            pr['lnw'], pr['lnb'], pr['gmat'])
    in_specs = [pl.BlockSpec((batch, c, cols), lambda i: (0, i, 0))] + [
        pl.BlockSpec(a.shape, lambda i: (0, 0)) for a in args[1:]]
    n_groups = RWKV_HEADS // RWKV_GROUP
    gl = RWKV_GROUP * RWKV_HEAD
    out_shape = (jax.ShapeDtypeStruct((batch, t, w), BF16),
                 jax.ShapeDtypeStruct((batch, n_groups, gl, gl), F32))
    ob, z = pl.pallas_call(
        functools.partial(_rwkv_prompt_kernel, chunk=c), grid=(nc,), in_specs=in_specs,
        out_specs=(pl.BlockSpec((batch, c, w), lambda i: (0, i, 0)),
                   pl.BlockSpec((batch, n_groups, gl, gl), lambda i: (0, 0, 0, 0))),
        out_shape=out_shape,
        scratch_shapes=[pltpu.VMEM((batch, cols), F32), pltpu.VMEM((batch, n_groups, gl, gl), F32)],
        compiler_params=_cparams(("arbitrary",)), name="rwkv_prompt")(*args)
    return ob.reshape(n, w), z


def _rwkv_sample_prep_kernel(rw_ref, sh_ref, mu_ref, lora_ref, w0_ref, a0_ref, kk_ref, ka_ref, g_ref,
                             r_out, k_out, v_out, w_out, a_out, b_out):
    rw = rw_ref[...]
    xr = rw + (sh_ref[...] - rw) * mu_ref[...]
    r, k2, v, logw, av, bv = _rwkv_prep(xr, lora_ref[...], w0_ref[...], a0_ref[...],
                                        kk_ref[...], ka_ref[...], g_ref[...])
    r_out[...] = r
    k_out[...] = k2
    v_out[...] = v
    w_out[...] = jnp.exp(logw)
    a_out[...] = av
    b_out[...] = bv


def _rwkv_sample_state_kernel(s_ref, r_ref, k_ref, v_ref, w_ref, a_ref, b_ref, s_out, y_out, *, tb):
    n = RWKV_HEAD
    eye = lax.broadcasted_iota(jnp.int32, (n, n), 0) == lax.broadcasted_iota(jnp.int32, (n, n), 1)
    for bl in range(tb):
        for h in range(RWKV_HEADS):
            s = s_ref[bl, h]
            row = lambda ref: ref[bl, h:h + 1, :]
            sa = jnp.sum(s * row(a_ref), axis=1, keepdims=True)
            vcol = jnp.sum(jnp.where(eye, row(v_ref), 0.0), axis=1, keepdims=True)
            s_new = s * row(w_ref) + sa * row(b_ref) + vcol * row(k_ref)
            s_out[bl, h] = s_new
            ycol = jnp.sum(s_new * row(r_ref), axis=1, keepdims=True)
            y_out[bl, h:h + 1, :] = jnp.sum(jnp.where(eye, ycol, 0.0), axis=0, keepdims=True)


def _rwkv_sample_post_kernel(y_ref, r_ref, k_ref, v_ref, rk_ref, lnw_ref, lnb_ref, g_ref, o_ref):
    o_ref[...] = _rwkv_post(y_ref[...], r_ref[...], k_ref[...], v_ref[...], rk_ref[...],
                            lnw_ref[...], lnb_ref[...], g_ref[...]).astype(o_ref.dtype)


def rwkv_sample(rw, shift0, state, W, tb=8):
    db, cols = rw.shape
    w = RWKV_HEADS * RWKV_HEAD
    pr = _rwkv_params(W)
    args = (rw, shift0, pr['mu'], pr['lora'], pr['w0'], pr['a0'], pr['kk'], pr['ka'], pr['gmat'])
    vec = jax.ShapeDtypeStruct((db, w), F32)
    r, k2, v, wd, av, bv = pl.pallas_call(
        _rwkv_sample_prep_kernel, grid=(1,), in_specs=[_full(a.shape) for a in args],
        out_specs=tuple(_full((db, w)) for _ in range(6)), out_shape=(vec,) * 6,
        compiler_params=_cparams(("arbitrary",)), name="rwkv_sample_prep")(*args)
    heads = lambda a: a.reshape(db, RWKV_HEADS, RWKV_HEAD)
    st_spec = pl.BlockSpec((tb, RWKV_HEADS, RWKV_HEAD, RWKV_HEAD), lambda i: (i, 0, 0, 0))
    hv_spec = pl.BlockSpec((tb, RWKV_HEADS, RWKV_HEAD), lambda i: (i, 0, 0))
    s_new, y = pl.pallas_call(
        functools.partial(_rwkv_sample_state_kernel, tb=tb), grid=(db // tb,),
        in_specs=[st_spec] + [hv_spec] * 6, out_specs=(st_spec, hv_spec),
        out_shape=(jax.ShapeDtypeStruct(state.shape, F32),
                   jax.ShapeDtypeStruct((db, RWKV_HEADS, RWKV_HEAD), F32)),
        compiler_params=_cparams(("parallel",)), name="rwkv_sample_state")(
            state, heads(r), heads(k2), heads(v), heads(wd), heads(av), heads(bv))
    pargs = (y.reshape(db, w), r, k2, v, pr['rk'], pr['lnw'], pr['lnb'], pr['gmat'])
    ob = pl.pallas_call(
        _rwkv_sample_post_kernel, grid=(1,), in_specs=[_full(a.shape) for a in pargs],
        out_specs=_full((db, w)), out_shape=jax.ShapeDtypeStruct((db, w), BF16),
        compiler_params=_cparams(("arbitrary",)), name="rwkv_sample_post")(*pargs)
    return ob, s_new


def _merge_kernel(oa_ref, ob_ref, oc_ref, g_ref, x_ref, woa_ref, wob_ref, woc_ref, wout_ref,
                  gffn_ref, wr_ref, br_ref, h_out, hn_out, ti_out, tg_out, *, d):
    o_a = jnp.dot(oa_ref[...], woa_ref[...], preferred_element_type=F32)
    o_b = jnp.dot(ob_ref[...], wob_ref[...], preferred_element_type=F32)
    o_c = jnp.dot(oc_ref[...], woc_ref[...], preferred_element_type=F32)
    g = g_ref[...].astype(F32)
    merged = g[:, 0:d] * o_a + g[:, d:2 * d] * o_b + g[:, 2 * d:3 * d] * o_c
    h = x_ref[...] + _dot(merged, wout_ref[...])
    h_out[...] = h
    hn = _rms(h, gffn_ref[...])
    _store_row_tiles(hn_out, hn)
    logits = _dot3(hn, wr_ref[...]) + br_ref[...]
    n_e = logits.shape[1]
    lane = lax.broadcasted_iota(jnp.int32, logits.shape, 1)
    cur = logits
    vals, idxs = [], []
    for _ in range(TOP_K):
        mx = jnp.max(cur, axis=-1, keepdims=True)
        ix = jnp.min(jnp.where(cur == mx, lane, n_e), axis=-1, keepdims=True)
        vals.append(mx)
        idxs.append(ix)
        cur = jnp.where(lane == ix, -jnp.inf, cur)
    es = [jnp.exp(vk - vals[0]) for vk in vals]
    den = es[0]
    for e in es[1:]:
        den = den + e
    lane_k = lax.broadcasted_iota(jnp.int32, ti_out.shape, 1)
    ti = jnp.zeros(ti_out.shape, jnp.int32)
    tg = jnp.zeros(tg_out.shape, F32)
    for kk in range(TOP_K):
        ti = jnp.where(lane_k == kk, idxs[kk], ti)
        tg = jnp.where(lane_k == kk, es[kk] / den, tg)
    ti_out[...] = ti
    tg_out[...] = tg


def merge_out(oa, w_oa, ob, oc, gates, x, W, tm):
    n, d = x.shape
    n_e = W['w_router'].shape[1]
    args = (oa, ob, oc, gates, x, w_oa.astype(BF16), W['w_o_rwkv'].astype(BF16), W['w_o_mem'].astype(BF16),
            W['w_out'].astype(BF16), W['g_ffn_norm'].reshape(1, d), W['w_router'], W['b_router'].reshape(1, n_e))
    row = lambda i: (i, 0)
    in_specs = [pl.BlockSpec((tm, a.shape[1]), row) for a in args[:5]] + [_full(a.shape) for a in args[5:]]
    out_shape = (jax.ShapeDtypeStruct((n, d), F32), jax.ShapeDtypeStruct((n * SUBLANES, LANES), F32),
                 jax.ShapeDtypeStruct((n, TOP_K), jnp.int32), jax.ShapeDtypeStruct((n, TOP_K), F32))
    out_specs = (pl.BlockSpec((tm, d), row), pl.BlockSpec((tm * SUBLANES, LANES), row),
                 pl.BlockSpec((tm, TOP_K), row), pl.BlockSpec((tm, TOP_K), row))
    return pl.pallas_call(
        functools.partial(_merge_kernel, d=d), grid=(n // tm,), in_specs=in_specs, out_specs=out_specs,
        out_shape=out_shape, compiler_params=_cparams(("parallel",)), name="merge_out")(*args)


def _gmm_kernel(be_ref, nu_ref, tok_ref, src_ref, w1_ref, perm_ref, b1_ref, w2_ref, b2_ref,
                y_ref, xbuf, w1b, w2b, sem, *, tm):
    i = pl.program_id(0)
    n_used = nu_ref[0]

    new_expert = jnp.logical_or(i == 0, be_ref[i] != be_ref[jnp.maximum(i - 1, 0)])

    @pl.when(jnp.logical_and(new_expert, i < n_used))
    def _():
        perm = perm_ref[...]
        for t in range(w1_ref.shape[2] // (2 * LANES)):
            sl = slice(2 * LANES * t, 2 * LANES * (t + 1))
            w1b[:, sl] = jnp.dot(w1_ref[0, :, sl].astype(BF16), perm, preferred_element_type=F32).astype(BF16)
        w2b[...] = w2_ref[0].astype(BF16)

    def row_copy(blk, r, slot):
        return pltpu.make_async_copy(src_ref.at[pl.ds(tok_ref[blk * tm + r] * SUBLANES, SUBLANES)],
                                     xbuf.at[slot, pl.ds(r * SUBLANES, SUBLANES)], sem.at[slot])

    def drain(slot):
        def step(r, carry):
            pltpu.make_async_copy(src_ref.at[pl.ds(0, SUBLANES)], xbuf.at[slot, pl.ds(0, SUBLANES)],
                                  sem.at[slot]).wait()
            return carry
        lax.fori_loop(0, tm, step, 0, unroll=8)

    last = n_used - 1

    @pl.when(jnp.logical_and(i == 0, n_used > 0))
    def _():
        def issue(r, carry):
            row_copy(0, r, 0).start()
            row_copy(jnp.minimum(1, last), r, 1).start()
            return carry
        lax.fori_loop(0, tm, issue, 0, unroll=8)

    @pl.when(i < n_used)
    def _():
        slot = i % GMM_BUFS
        drain(slot)
        x = _load_row_tiles(xbuf.at[slot], tm).astype(BF16)
        nxt = jnp.minimum(i + 2, last)
        nslot = (i + 2) % GMM_BUFS
        for r in range(tm):
            row_copy(nxt, r, nslot).start()
        h = jnp.dot(x, w1b[...], preferred_element_type=F32) + b1_ref[0]
        acts = []
        for t in range(w2_ref.shape[1] // LANES):
            x_glu = jnp.minimum(h[:, 2 * LANES * t:2 * LANES * t + LANES], SWIGLU_LIMIT)
            x_lin = jnp.clip(h[:, 2 * LANES * t + LANES:2 * LANES * (t + 1)], -SWIGLU_LIMIT, SWIGLU_LIMIT)
            acts.append((x_glu * _sigmoid(SWIGLU_ALPHA * x_glu) * (x_lin + 1.0)).astype(BF16))
        y = jnp.dot(jnp.concatenate(acts, axis=1), w2b[...], preferred_element_type=F32) + b2_ref[0]
        _store_row_tiles(y_ref, y)

    @pl.when(i == last)
    def _():
        drain((i + 1) % GMM_BUFS)
        drain((i + 2) % GMM_BUFS)

    @pl.when(i >= n_used)
    def _():
        y_ref[...] = jnp.zeros(y_ref.shape, y_ref.dtype)


def grouped_mlp(src, row_tok, block_e, n_used, W, tm):
    n_rows = row_tok.shape[0]
    d = W['w_mlp1'].shape[1]
    n_e = W['w_mlp1'].shape[0]
    d_ff = W['w_mlp2'].shape[1]
    j = np.arange(2 * LANES)
    col_src = np.where(j < LANES, 2 * j, 2 * (j - LANES) + 1)
    perm = jnp.asarray((np.arange(2 * LANES)[:, None] == col_src[None, :]).astype(np.float32), dtype=BF16)
    b1 = W['b_mlp1'].reshape(n_e, d_ff // LANES, LANES, 2)
    b1 = jnp.swapaxes(b1, 2, 3).reshape(n_e, 1, 2 * d_ff)
    b2 = W['b_mlp2'].reshape(n_e, 1, d)
    ex = lambda i, be, nu, tok: (be[i], 0, 0)
    grid_spec = pltpu.PrefetchScalarGridSpec(
        num_scalar_prefetch=3, grid=(n_rows // tm,),
        in_specs=[pl.BlockSpec(memory_space=pl.ANY),
                  pl.BlockSpec((1, d, 2 * d_ff), ex), pl.BlockSpec(perm.shape, lambda i, be, nu, tok: (0, 0)),
                  pl.BlockSpec((1, 1, 2 * d_ff), ex),
                  pl.BlockSpec((1, d_ff, d), ex), pl.BlockSpec((1, 1, d), ex)],
        out_specs=pl.BlockSpec((tm * SUBLANES, LANES), lambda i, be, nu, tok: (i, 0)),
        scratch_shapes=[pltpu.VMEM((GMM_BUFS, tm * SUBLANES, LANES), F32), pltpu.VMEM((d, 2 * d_ff), BF16),
                        pltpu.VMEM((d_ff, d), BF16), pltpu.SemaphoreType.DMA((GMM_BUFS,))])
    return pl.pallas_call(
        functools.partial(_gmm_kernel, tm=tm), grid_spec=grid_spec,
        out_shape=jax.ShapeDtypeStruct((n_rows * SUBLANES, LANES), F32),
        compiler_params=_cparams(("arbitrary",), GMM_VMEM_LIMIT), name="moe_gmm")(
            block_e, n_used, row_tok, src, W['w_mlp1'], perm, b1, W['w_mlp2'], b2)


def _combine_kernel(dest_ref, gate_ref, h_ref, gf_ref, yb_ref, o_ref, buf, sem, *, tc):
    i = pl.program_id(0)

    def start(step, slot):
        def issue(t, carry):
            for k in range(TOP_K):
                pltpu.make_async_copy(
                    yb_ref.at[pl.ds(dest_ref[(step * tc + t) * TOP_K + k] * SUBLANES, SUBLANES)],
                    buf.at[slot, k, pl.ds(t * SUBLANES, SUBLANES)], sem.at[slot]).start()
            return carry
        lax.fori_loop(0, tc, issue, 0, unroll=4)

    @pl.when(i == 0)
    def _():
        start(0, 0)

    @pl.when(i + 1 < pl.num_programs(0))
    def _():
        start(i + 1, (i + 1) % 2)

    slot = i % 2

    def drain(t, carry):
        for k in range(TOP_K):
            pltpu.make_async_copy(yb_ref.at[pl.ds(0, SUBLANES)], buf.at[slot, k, pl.ds(0, SUBLANES)],
                                  sem.at[slot]).wait()
        return carry

    lax.fori_loop(0, tc, drain, 0, unroll=4)
    gate = gate_ref[...]
    y = h_ref[...]
    for k in range(TOP_K):
        y = y + gate[:, k:k + 1] * _load_row_tiles(buf.at[slot, k], tc)
    o_ref[...] = _rms(y, gf_ref[...])


def moe_combine(yb, dest, gate, h, g_final, tc):
    n, d = h.shape
    grid_spec = pltpu.PrefetchScalarGridSpec(
        num_scalar_prefetch=1, grid=(n // tc,),
        in_specs=[pl.BlockSpec((tc, TOP_K), lambda i, ds: (i, 0)),
                  pl.BlockSpec((tc, d), lambda i, ds: (i, 0)),
                  pl.BlockSpec((1, d), lambda i, ds: (0, 0)),
                  pl.BlockSpec(memory_space=pl.ANY)],
        out_specs=pl.BlockSpec((tc, d), lambda i, ds: (i, 0)),
        scratch_shapes=[pltpu.VMEM((2, TOP_K, tc * SUBLANES, LANES), F32), pltpu.SemaphoreType.DMA((2,))])
    return pl.pallas_call(
        functools.partial(_combine_kernel, tc=tc), grid_spec=grid_spec,
        out_shape=jax.ShapeDtypeStruct((n, d), F32),
        compiler_params=_cparams(("arbitrary",)), name="moe_combine")(dest, gate, h, g_final.reshape(1, d), yb)


def _routing(top_i, n_e, tm):
    n_tok = top_i.shape[0]
    n_assign = n_tok * TOP_K
    flat_e = top_i.reshape(-1)
    onehot = (flat_e[:, None] == jnp.arange(n_e, dtype=jnp.int32)[None, :]).astype(jnp.int32)
    csum = jnp.cumsum(onehot, axis=0)
    counts = csum[-1]
    padded = ((counts + tm - 1) // tm) * tm
    pend = jnp.cumsum(padded)
    pstart = pend - padded
    dest = jnp.sum(onehot * (csum - onehot + pstart[None, :]), axis=1).astype(jnp.int32)
    n_blocks = -(-n_assign // tm) + n_e
    n_rows = n_blocks * tm
    row_tok = jnp.zeros((n_rows,), jnp.int32).at[dest].set(jnp.arange(n_assign, dtype=jnp.int32) // TOP_K)
    block_row0 = jnp.arange(n_blocks, dtype=jnp.int32) * tm
    block_e = jnp.minimum(jnp.sum((pend[None, :] <= block_row0[:, None]).astype(jnp.int32), axis=1), n_e - 1)
    n_used = (pend[-1:] // tm).astype(jnp.int32)
    return dest, row_tok, block_e, n_used


def kernel(x_prompt, x_sample, mem_prompt, cache_kv_latent, cache_k_rope, page_table, state_rwkv, state_rwkv_shift, cache_mem_k, cache_mem_v, g_attn_norm, w_in, b_gate, g_q_norm, w_uq, g_kv_norm, w_uk, w_uv, w_o_mla, mu_shift, w0_decay, w_decay_up, a0, w_aaa_up, k_k, k_a, r_k, ln_x_w, ln_x_b, w_o_rwkv, g_mem_norm, w_mem_k, w_mem_v, w_o_mem, w_out, g_ffn_norm, w_router, b_router, w_mlp1, b_mlp1, w_mlp2, b_mlp2, g_final):
    layer_params = dict(g_attn_norm=g_attn_norm, w_in=w_in, b_gate=b_gate, g_q_norm=g_q_norm, w_uq=w_uq,
                        g_kv_norm=g_kv_norm, w_uk=w_uk, w_uv=w_uv, w_o_mla=w_o_mla, mu_shift=mu_shift,
                        w0_decay=w0_decay, w_decay_up=w_decay_up, a0=a0, w_aaa_up=w_aaa_up, k_k=k_k, k_a=k_a,
                        r_k=r_k, ln_x_w=ln_x_w, ln_x_b=ln_x_b, w_o_rwkv=w_o_rwkv, g_mem_norm=g_mem_norm,
                        w_mem_k=w_mem_k, w_mem_v=w_mem_v, w_o_mem=w_o_mem, w_out=w_out, g_ffn_norm=g_ffn_norm,
                        w_router=w_router, b_router=b_router, w_mlp1=w_mlp1, b_mlp1=b_mlp1,
                        w_mlp2=w_mlp2, b_mlp2=b_mlp2)
    depth = w_in.shape[0]
    assert depth == 1, "single-layer trunk"
    W = {name: arr.reshape(arr.shape[1:]) for name, arr in layer_params.items()}
    B, T, D = x_prompt.shape
    DB, TS, _ = x_sample.shape
    assert TS == 1, "sample group decodes one token per sequence"
    n_pages = page_table.shape[1]
    page = cache_kv_latent.shape[2]
    past_len = n_pages * page
    kv_lora = w_uk.shape[1]
    mem_tokens = mem_prompt.shape[1]
    mem_w = MEM_HEADS * MEM_HEAD
    n_e = w_router.shape[2]
    rw_w = RWKV_HEADS * RWKV_HEAD

    tm = min(PROJ_TM, T)
    xp = x_prompt.reshape(B * T, D)
    xs = x_sample.reshape(DB, D)

    tab_p = _rope_tables(jnp.arange(T, dtype=jnp.int32))
    ckv_p, kpe_p, q_h, k_h, v_h = mla_proj_prompt(xp, tab_p, W, tm)
    rw_p, qm_p, gates_p = in_proj_rest(xp, W, tm)
    mk_p, mv_p = mem_kv(mem_prompt.reshape(B * mem_tokens, D), W)
    tq = min(FLASH_TQ, T)
    oa_p = flash_prompt(q_h, k_h, v_h, B, tq)
    oc_p = mem_attn_prompt(qm_p, mk_p.reshape(B, mem_tokens, mem_w), mv_p.reshape(B, mem_tokens, mem_w), tm)
    ob_p, z_p = rwkv_prompt(rw_p, B, W)
    woa_p = jnp.pad(W['w_o_mla'].reshape(MLA_HEADS, V_HEAD, D),
                    ((0, 0), (0, LANES - V_HEAD), (0, 0))).reshape(MLA_HEADS * LANES, D)
    h_p, hn_p, ti_p, tg_p = merge_out(oa_p, woa_p, ob_p, oc_p, gates_p, xp, W, tm)

    tab_s = _rope_tables(jnp.full((DB,), past_len, jnp.int32))
    ckv_s, kpe_s, qlat_s, qpe_s = mla_proj_sample(xs, tab_s, W)
    rw_s, qm_s, gates_s = in_proj_rest(xs, W, DB)
    oa_s = decode_attn(jnp.transpose(qlat_s, (1, 0, 2)), jnp.transpose(qpe_s, (1, 0, 2)),
                       ckv_s.reshape(DB, 1, kv_lora), kpe_s.reshape(DB, 1, QK_ROPE),
                       cache_kv_latent, jnp.swapaxes(cache_k_rope, 2, 3), page_table,
                       W['w_uv']).reshape(DB, MLA_HEADS * V_HEAD)
    oc_s = mem_attn_sample(qm_s.reshape(DB, MEM_HEADS, MEM_HEAD),
                           cache_mem_k.reshape(DB, mem_tokens, MEM_HEADS, MEM_HEAD),
                           cache_mem_v.reshape(DB, mem_tokens, MEM_HEADS, MEM_HEAD)).reshape(DB, mem_w)
    ob_s, st_s = rwkv_sample(rw_s, state_rwkv_shift.reshape(DB, -1),
                             state_rwkv.reshape(DB, RWKV_HEADS, RWKV_HEAD, RWKV_HEAD), W)
    h_s, hn_s, ti_s, tg_s = merge_out(oa_s.astype(BF16), W['w_o_mla'], ob_s, oc_s, gates_s, xs, W, DB)

    hn_all = jnp.concatenate([hn_p, hn_s], axis=0)
    top_i = jnp.concatenate([ti_p, ti_s], axis=0)
    dest, row_tok, block_e, n_used = _routing(top_i, n_e, MOE_TM)
    yb = grouped_mlp(hn_all, row_tok, block_e, n_used, W, MOE_TM)
    n_p = B * T
    y_p = moe_combine(yb, dest[:n_p * TOP_K], tg_p, h_p, g_final, min(128, T))
    y_s = moe_combine(yb, dest[n_p * TOP_K:], tg_s, h_s, g_final, DB)

    zq = z_p.reshape(B, RWKV_HEADS // RWKV_GROUP, RWKV_GROUP, RWKV_HEAD, RWKV_GROUP, RWKV_HEAD)
    st_p = jnp.stack([zq[:, :, q, :, q, :] for q in range(RWKV_GROUP)], axis=2)
    st_p = jnp.swapaxes(st_p.reshape(B, RWKV_HEADS, RWKV_HEAD, RWKV_HEAD), -1, -2)
    lead = lambda a: a[None]
    return (y_p.reshape(B, T, D), y_s.reshape(DB, TS, D),
            lead(ckv_p.reshape(B, T, kv_lora)), lead(kpe_p.reshape(B, T, QK_ROPE)),
            lead(st_p), lead(rw_p.reshape(B, T, -1)[:, -1]),
            lead(mk_p.reshape(B, mem_tokens, MEM_HEADS, MEM_HEAD)),
            lead(mv_p.reshape(B, mem_tokens, MEM_HEADS, MEM_HEAD)),
            lead(ckv_s.reshape(DB, TS, kv_lora)), lead(kpe_s.reshape(DB, TS, QK_ROPE)),
            lead(st_s), lead(rw_s))
```
